```python
import math
import jax
import jax.numpy as jnp
from jax import lax
import numpy as np

D_MODEL = 4096
BATCH = 1
SEQ = 8192
DEPTH = 1
DEC_BATCH = 32
DEC_SEQ = 1
PAST_LEN = 8192
PAGE_SIZE = 128

A_HEADS = 16
A_KV_HEADS = 4
A_GROUP = A_HEADS // A_KV_HEADS
A_HEAD_DIM = 128
A_VDIM = 2 * A_HEAD_DIM
ROPE_THETA = 10000.0
Q_BLOCK = 128
R_HEAD = 64
R_HEADS = D_MODEL // R_HEAD
R_WIDTH = R_HEADS * R_HEAD
R_DECAY_LORA = max(32, round(math.sqrt(D_MODEL) * 1.8 / 32) * 32)
R_AAA_LORA = max(32, round(math.sqrt(D_MODEL) * 1.8 / 32) * 32)
R_GATE_LORA = max(32, round(D_MODEL ** 0.8 * 0.6 / 32) * 32)
GN_EPS = 64e-5
N_GROUPS = 8
EXPERTS_PER_GROUP = 8
N_EXPERTS = N_GROUPS * EXPERTS_PER_GROUP
TOP_K = 2
D_EXPERT = 1024
MOE_BLOCK = 128
NORM_EPS = 1e-6
Q_COLS = A_HEADS * 2 * A_HEAD_DIM
K_COLS = A_KV_HEADS * 2 * A_HEAD_DIM
V_COLS = A_KV_HEADS * A_VDIM
RWKV_COLS = 3 * R_WIDTH + R_DECAY_LORA + R_AAA_LORA + R_GATE_LORA
GATE_COLS = 2 * D_MODEL
K_OFF = Q_COLS
V_OFF = K_OFF + K_COLS
R_OFF = V_OFF + V_COLS
G_OFF = R_OFF + RWKV_COLS
PROJ_COLS = G_OFF + GATE_COLS
F32 = jnp.float32

kernel_name = 'hybrid_diffattn_rwkv7_hmoe_step'


def _rms(x, w, eps=NORM_EPS):
    xf = x.astype(F32)
    xf = xf * lax.rsqrt(jnp.mean(xf * xf, axis=-1, keepdims=True) + eps)
    return (xf * w.astype(F32)).astype(x.dtype)


def _rope_tables(positions):
    inv = 1.0 / (ROPE_THETA ** (jnp.arange(0, A_HEAD_DIM, 2, dtype=F32) / A_HEAD_DIM))
    ang = positions.astype(F32)[:, None] * inv[None, :]
    return jnp.cos(ang), jnp.sin(ang)


def _rope(x, cos, sin):
    xf = x.astype(F32)
    x1, x2 = jnp.split(xf, 2, axis=-1)
    return jnp.concatenate([x1 * cos - x2 * sin, x2 * cos + x1 * sin], axis=-1).astype(x.dtype)


def _project(x, norm_w, w_in, q_norm_w, k_norm_w, positions):
    B, T, _ = x.shape
    h = _rms(x, norm_w)
    p = h @ w_in
    q = p[..., :K_OFF].reshape(B, T, A_KV_HEADS, A_GROUP, 2, A_HEAD_DIM)
    k = p[..., K_OFF:V_OFF].reshape(B, T, A_KV_HEADS, 2, A_HEAD_DIM)
    v = p[..., V_OFF:R_OFF].reshape(B, T, A_KV_HEADS, A_VDIM)
    pr = p[..., R_OFF:G_OFF]
    gates = p[..., G_OFF:]
    cos, sin = _rope_tables(positions)
    q = _rope(_rms(q, q_norm_w), cos[None, :, None, None, None, :], sin[None, :, None, None, None, :])
    k = _rope(_rms(k, k_norm_w), cos[None, :, None, None, :], sin[None, :, None, None, :])
    return q, k, v, pr, gates


def _diff_lambda(lq1, lk1, lq2, lk2, lam_init):
    e1 = jnp.exp(jnp.sum(lq1.astype(F32) * lk1.astype(F32)))
    e2 = jnp.exp(jnp.sum(lq2.astype(F32) * lk2.astype(F32)))
    return e1 - e2 + lam_init


def _diff_attn_prompt(q, k, v, lam):
    B, T = q.shape[:2]
    nb = T // Q_BLOCK
    scale = A_HEAD_DIM ** -0.5
    qb = jnp.moveaxis(q.reshape(B, nb, Q_BLOCK, A_KV_HEADS, A_GROUP, 2, A_HEAD_DIM), 1, 0)
    kpos = jnp.arange(T)

    def block(args):
        qi, i = args
        s = jnp.einsum('bqhgmd,bkhmd->bhgmqk', qi, k).astype(F32) * scale
        qpos = i * Q_BLOCK + jnp.arange(Q_BLOCK)
        s = jnp.where(kpos[None, :] <= qpos[:, None], s, -jnp.inf)
        p = jax.nn.softmax(s, axis=-1)
        a = (p[:, :, :, 0] - lam * p[:, :, :, 1]).astype(v.dtype)
        return jnp.einsum('bhgqk,bkhe->bqhge', a, v)

    o = lax.map(block, (qb, jnp.arange(nb)))
    return jnp.moveaxis(o, 0, 1).reshape(B, T, A_KV_HEADS, A_GROUP, A_VDIM)


def _diff_attn_sample(q, k_new, v_new, k_past, v_past, lam):
    S = q.shape[1]
    P = k_past.shape[1]
    scale = A_HEAD_DIM ** -0.5
    s_past = jnp.einsum('bqhgmd,bkhmd->bhgmqk', q, k_past).astype(F32) * scale
    s_new = jnp.einsum('bqhgmd,bkhmd->bhgmqk', q, k_new).astype(F32) * scale
    causal = jnp.arange(S)[None, :] <= jnp.arange(S)[:, None]
    s_new = jnp.where(causal, s_new, -jnp.inf)
    p = jax.nn.softmax(jnp.concatenate([s_past, s_new], axis=-1), axis=-1)
    a = (p[:, :, :, 0] - lam * p[:, :, :, 1]).astype(v_past.dtype)
    return (jnp.einsum('bhgqk,bkhe->bqhge', a[..., :P], v_past)
            + jnp.einsum('bhgqk,bkhe->bqhge', a[..., P:], v_new))


def _attn_branch(o, subln_w, lam_init, w_branch):
    B, T = o.shape[:2]
    o = _rms(o, subln_w) * (1.0 - lam_init)
    return o.reshape(B, T, A_HEADS * A_VDIM) @ w_branch


def _rwkv7(pr, pr_prev, S0, mu, w0, w2, a0, a2, g2, k_k, k_a, r_k, ln_w, ln_b):
    B, T, _ = pr.shape
    xm = pr + (pr_prev - pr) * mu
    r = xm[..., :R_WIDTH]
    k = xm[..., R_WIDTH:2 * R_WIDTH]
    v = xm[..., 2 * R_WIDTH:3 * R_WIDTH]
    o1 = 3 * R_WIDTH
    xw = xm[..., o1:o1 + R_DECAY_LORA]
    xa = xm[..., o1 + R_DECAY_LORA:o1 + R_DECAY_LORA + R_AAA_LORA]
    xg = xm[..., o1 + R_DECAY_LORA + R_AAA_LORA:]
    w = -jax.nn.softplus(-(w0 + jnp.tanh(xw) @ w2)) - 0.5
    a = jax.nn.sigmoid(a0 + xa @ a2)
    g = jax.nn.sigmoid(xg) @ g2

    def hd(t):
        return t.reshape(B, T, R_HEADS, R_HEAD).astype(F32)

    kk = hd(k * k_k)
    kk = kk / jnp.maximum(jnp.sqrt(jnp.sum(kk * kk, axis=-1, keepdims=True)), 1e-12)
    k = k * (1.0 + (a - 1.0) * k_a)
    decay = jnp.exp(-jnp.exp(hd(w)))
    rf, kf, vf, af = hd(r), hd(k), hd(v), hd(a)
    seq = tuple(jnp.moveaxis(t, 1, 0) for t in (rf, decay, kf, vf, kk, af))

    def step(S, inp):
        r_t, w_t, k_t, v_t, kk_t, a_t = inp
        sa = jnp.einsum('bhvk,bhk->bhv', S, -kk_t)
        S = (S * w_t[:, :, None, :] + sa[..., :, None] * (kk_t * a_t)[..., None, :]
             + v_t[..., :, None] * k_t[..., None, :])
        return S, jnp.einsum('bhvk,bhk->bhv', S, r_t)

    S, y = lax.scan(step, S0.astype(F32), seq)
    y = jnp.moveaxis(y, 0, 1)
    mean = jnp.mean(y, axis=-1, keepdims=True)
    var = jnp.mean(jnp.square(y - mean), axis=-1, keepdims=True)
    y = ((y - mean) * lax.rsqrt(var + GN_EPS)).reshape(B, T, R_WIDTH) * ln_w.astype(F32) + ln_b.astype(F32)
    bonus = jnp.sum(rf * kf * r_k.astype(F32), axis=-1, keepdims=True) * vf
    out = (y + bonus.reshape(B, T, R_WIDTH)) * g.astype(F32)
    return out.astype(pr.dtype), S.astype(S0.dtype)


def _merge(x, o_attn, o_rwkv, gates, gate_b, w_out):
    gl = gates + gate_b
    mixed = jax.nn.sigmoid(gl[..., :D_MODEL]) * o_attn + jax.nn.sigmoid(gl[..., D_MODEL:]) * o_rwkv
    return x + mixed @ w_out


def _moe_apply(xf, expert_idx, weights, wg, wu, wd, l):
    T, D = xf.shape
    TK = T * TOP_K
    E = wg.shape[1]
    n = -(-TK // E)
    blk = min(MOE_BLOCK, max(8, 1 << (n - 1).bit_length()))
    nb = -(-TK // blk) + E
    P = nb * blk
    flat_e = expert_idx.reshape(-1)
    flat_w = weights.reshape(-1).astype(xf.dtype)
    flat_t = jnp.repeat(jnp.arange(T, dtype=jnp.int32), TOP_K)
    order = jnp.argsort(flat_e)
    se = flat_e[order]
    counts = jnp.bincount(flat_e, length=E)
    pcounts = (counts + blk - 1) // blk * blk
    pend = jnp.cumsum(pcounts)
    pstart = pend - pcounts
    start = jnp.cumsum(counts) - counts
    dest = pstart[se] + jnp.arange(TK) - start[se]
    slot_t = jnp.full((P,), T, jnp.int32).at[dest].set(flat_t[order])
    slot_w = jnp.zeros((P,), xf.dtype).at[dest].set(flat_w[order])
    blk_e = jnp.minimum(jnp.searchsorted(pend, jnp.arange(nb) * blk, side='right'), E - 1)
    xs = jnp.concatenate([xf, jnp.zeros((1, D), xf.dtype)], axis=0)[slot_t].reshape(nb, blk, D)

    def run(args):
        xb, e = args
        hb = jax.nn.silu(xb @ wg[l, e]) * (xb @ wu[l, e])
        return hb @ wd[l, e]

    ys = lax.map(run, (xs, blk_e)).reshape(P, D) * slot_w[:, None]
    return jax.ops.segment_sum(ys, slot_t, num_segments=T + 1)[:T]


def _hier_moe(h, l, rg_w, rg_b, re_w, re_b, wg, wu, wd):
    shp = h.shape
    xf = h.reshape(-1, shp[-1])
    T = xf.shape[0]
    gprob = jax.nn.softmax((xf @ rg_w[l]).astype(F32) + rg_b[l].astype(F32), axis=-1)
    gsel = jnp.argmax(gprob, axis=-1)
    gp = jnp.take_along_axis(gprob, gsel[:, None], axis=-1)[:, 0]
    elog = ((xf @ re_w[l]).astype(F32) + re_b[l].astype(F32)).reshape(T, N_GROUPS, EXPERTS_PER_GROUP)
    elog = jnp.take_along_axis(elog, gsel[:, None, None], axis=1)[:, 0]
    top_p, top_i = lax.top_k(jax.nn.softmax(elog, axis=-1), TOP_K)
    top_p = top_p / jnp.sum(top_p, axis=-1, keepdims=True)
    weights = gp[:, None] * top_p
    expert_idx = gsel[:, None].astype(jnp.int32) * EXPERTS_PER_GROUP + top_i.astype(jnp.int32)
    return _moe_apply(xf, expert_idx, weights, wg, wu, wd, l).reshape(shp)


def setup_inputs(seed: int = 0) -> dict:
    key = jax.random.key(seed)
    ks = iter(jax.random.split(key, 48))

    def nrm(shape, scale):
        return jax.random.normal(next(ks), shape, F32) * scale

    def gain(shape):
        return 1.0 + nrm(shape, 0.02)

    n_pages = PAST_LEN // PAGE_SIZE
    n_used = DEC_BATCH * n_pages
    n_phys = n_used + max(1, n_used // 4)
    L = DEPTH
    return {
        'x_prompt': nrm((BATCH, SEQ, D_MODEL), 1.0),
        'x_sample': nrm((DEC_BATCH, DEC_SEQ, D_MODEL), 1.0),
        'cache_k': nrm((L, n_phys, PAGE_SIZE, A_KV_HEADS, 2, A_HEAD_DIM), 1.0),
        'cache_v': nrm((L, n_phys, PAGE_SIZE, A_KV_HEADS, A_VDIM), 1.0),
        'state_rwkv': nrm((L, DEC_BATCH, R_HEADS, R_HEAD, R_HEAD), 0.5),
        'state_shift': nrm((L, DEC_BATCH, RWKV_COLS), 1.0),
        'page_table': jax.random.permutation(next(ks), n_phys)[:n_used].reshape(DEC_BATCH, n_pages).astype(jnp.int32),
        'norm1_w': gain((L, D_MODEL)),
        'w_in': nrm((L, D_MODEL, PROJ_COLS), D_MODEL ** -0.5),
        'gate_b': nrm((L, GATE_COLS), 0.01),
        'q_norm_w': gain((L, A_HEAD_DIM)),
        'k_norm_w': gain((L, A_HEAD_DIM)),
        'lambda_q1': nrm((L, A_HEAD_DIM), 0.1),
        'lambda_k1': nrm((L, A_HEAD_DIM), 0.1),
        'lambda_q2': nrm((L, A_HEAD_DIM), 0.1),
        'lambda_k2': nrm((L, A_HEAD_DIM), 0.1),
        'subln_w': gain((L, A_VDIM)),
        'rwkv_mu': jax.random.uniform(next(ks), (L, RWKV_COLS), F32),
        'rwkv_w0': jax.random.uniform(next(ks), (L, R_WIDTH), F32, -6.0, 1.0),
        'rwkv_w2': nrm((L, R_DECAY_LORA, R_WIDTH), 0.1 * R_DECAY_LORA ** -0.5),
        'rwkv_a0': nrm((L, R_WIDTH), 0.1),
        'rwkv_a2': nrm((L, R_AAA_LORA, R_WIDTH), R_AAA_LORA ** -0.5),
        'rwkv_g2': nrm((L, R_GATE_LORA, R_WIDTH), R_GATE_LORA ** -0.5),
        'rwkv_k_k': 0.85 + nrm((L, R_WIDTH), 0.02),
        'rwkv_k_a': gain((L, R_WIDTH)),
        'rwkv_r_k': nrm((L, R_HEADS, R_HEAD), 0.1),
        'rwkv_ln_w': gain((L, R_WIDTH)),
        'rwkv_ln_b': nrm((L, R_WIDTH), 0.01),
        'w_branch_attn': nrm((L, A_HEADS * A_VDIM, D_MODEL), (A_HEADS * A_VDIM) ** -0.5),
        'w_branch_rwkv': nrm((L, R_WIDTH, D_MODEL), R_WIDTH ** -0.5),
        'w_out': nrm((L, D_MODEL, D_MODEL), D_MODEL ** -0.5),
        'norm2_w': gain((L, D_MODEL)),
        'router_group_w': nrm((L, D_MODEL, N_GROUPS), D_MODEL ** -0.5),
        'router_group_b': nrm((L, N_GROUPS), 0.01),
        'router_expert_w': nrm((L, D_MODEL, N_EXPERTS), D_MODEL ** -0.5),
        'router_expert_b': nrm((L, N_EXPERTS), 0.01),
        'expert_w_gate': nrm((L, N_EXPERTS, D_MODEL, D_EXPERT), D_MODEL ** -0.5),
        'expert_w_up': nrm((L, N_EXPERTS, D_MODEL, D_EXPERT), D_MODEL ** -0.5),
        'expert_w_down': nrm((L, N_EXPERTS, D_EXPERT, D_MODEL), D_EXPERT ** -0.5),
    }


def reference(x_prompt, x_sample, cache_k, cache_v, state_rwkv, state_shift, page_table,
              norm1_w, w_in, gate_b, q_norm_w, k_norm_w, lambda_q1, lambda_k1, lambda_q2, lambda_k2,
              subln_w, rwkv_mu, rwkv_w0, rwkv_w2, rwkv_a0, rwkv_a2, rwkv_g2, rwkv_k_k, rwkv_k_a,
              rwkv_r_k, rwkv_ln_w, rwkv_ln_b, w_branch_attn, w_branch_rwkv, w_out, norm2_w,
              router_group_w, router_group_b, router_expert_w, router_expert_b,
              expert_w_gate, expert_w_up, expert_w_down):
    B, T, _ = x_prompt.shape
    DB, S, _ = x_sample.shape
    n_pages = page_table.shape[1]
    past = n_pages * cache_k.shape[2]
    pos_p = jnp.arange(T)
    pos_s = past + jnp.arange(S)
    xp, xs = x_prompt, x_sample
    nk_p, nv_p, nr_p, nsh_p, nk_s, nv_s, nr_s, nsh_s = [], [], [], [], [], [], [], []
    for l in range(DEPTH):
        lam_init = 0.8 - 0.6 * math.exp(-0.3 * l)
        lam = _diff_lambda(lambda_q1[l], lambda_k1[l], lambda_q2[l], lambda_k2[l], lam_init)
        rw = (rwkv_mu[l], rwkv_w0[l], rwkv_w2[l], rwkv_a0[l], rwkv_a2[l], rwkv_g2[l], rwkv_k_k[l],
              rwkv_k_a[l], rwkv_r_k[l], rwkv_ln_w[l], rwkv_ln_b[l])
        moe_w = (router_group_w, router_group_b, router_expert_w, router_expert_b,
                 expert_w_gate, expert_w_up, expert_w_down)

        q, k, v, pr, gates = _project(xp, norm1_w[l], w_in[l], q_norm_w[l], k_norm_w[l], pos_p)
        o_a = _attn_branch(_diff_attn_prompt(q, k, v, lam), subln_w[l], lam_init, w_branch_attn[l])
        pr_prev = jnp.concatenate([jnp.zeros_like(pr[:, :1]), pr[:, :-1]], axis=1)
        o_r, s_fin = _rwkv7(pr, pr_prev, jnp.zeros((B, R_HEADS, R_HEAD, R_HEAD), pr.dtype), *rw)
        xp = _merge(xp, o_a, o_r @ w_branch_rwkv[l], gates, gate_b[l], w_out[l])
        xp = xp + _hier_moe(_rms(xp, norm2_w[l]), l, *moe_w)
        nk_p.append(k)
        nv_p.append(v)
        nr_p.append(s_fin)
        nsh_p.append(pr[:, -1])

        q, k, v, pr, gates = _project(xs, norm1_w[l], w_in[l], q_norm_w[l], k_norm_w[l], pos_s)
        k_past = cache_k[l, page_table].reshape(DB, past, A_KV_HEADS, 2, A_HEAD_DIM)
        v_past = cache_v[l, page_table].reshape(DB, past, A_KV_HEADS, A_VDIM)
        o_a = _attn_branch(_diff_attn_sample(q, k, v, k_past, v_past, lam), subln_w[l], lam_init, w_branch_attn[l])
        pr_prev = jnp.concatenate([state_shift[l][:, None].astype(pr.dtype), pr[:, :-1]], axis=1)
        o_r, s_fin = _rwkv7(pr, pr_prev, state_rwkv[l], *rw)
        xs = _merge(xs, o_a, o_r @ w_branch_rwkv[l], gates, gate_b[l], w_out[l])
        xs = xs + _hier_moe(_rms(xs, norm2_w[l]), l, *moe_w)
        nk_s.append(k)
        nv_s.append(v)
        nr_s.append(s_fin)
        nsh_s.append(pr[:, -1])

    y_prompt = xp
    y_sample = xs
    return (y_prompt, y_sample, jnp.stack(nk_p), jnp.stack(nv_p), jnp.stack(nr_p), jnp.stack(nsh_p),
            jnp.stack(nk_s), jnp.stack(nv_s), jnp.stack(nr_s), jnp.stack(nsh_s))
```

```python
import functools
import math

import jax
import jax.numpy as jnp
from jax import lax
from jax.experimental import pallas as pl
from jax.experimental.pallas import tpu as pltpu

F32 = jnp.float32
BF16 = jnp.bfloat16
HIGHEST = lax.Precision.HIGHEST

LANES = 128
SUBLANES = 8
VMEM_LIMIT = 56 * 1024 * 1024

ROPE_THETA = 10000.0
NORM_EPS = 1e-6
GN_EPS = 64e-5
TOP_K = 2

_NN = (((1,), (0,)), ((), ()))
_NT = (((1,), (1,)), ((), ()))
_TN = (((0,), (0,)), ((), ()))


def _cparams(*sem):
    return pltpu.CompilerParams(dimension_semantics=sem, vmem_limit_bytes=VMEM_LIMIT)


def _pick(n, pref, unit=LANES):
    if n <= pref:
        return n
    t = pref // unit * unit
    while t > unit and n % t:
        t -= unit
    assert n % t == 0, (n, pref)
    return t


def _dg(a, b, dn, precise):
    if precise:
        return lax.dot_general(a.astype(F32), b.astype(F32), dn, precision=HIGHEST,
                               preferred_element_type=F32)
    return lax.dot_general(a.astype(BF16), b.astype(BF16), dn, preferred_element_type=F32)


def _sigmoid(x):
    return 1.0 / (1.0 + jnp.exp(-x))


def _softplus(x):
    return jnp.maximum(x, 0.0) + jnp.log(1.0 + jnp.exp(-jnp.abs(x)))


def _rms_kernel(x_ref, w_ref, o_ref, *, eps):
    x = x_ref[...]
    ms = jnp.mean(x * x, axis=-1, keepdims=True)
    o_ref[...] = (x * lax.rsqrt(ms + eps) * w_ref[...]).astype(o_ref.dtype)


def _rmsnorm(x, w, out_dtype, tm=512):
    M, D = x.shape
    tm = _pick(M, tm, SUBLANES)
    return pl.pallas_call(
        functools.partial(_rms_kernel, eps=NORM_EPS),
        grid=(M // tm,),
        in_specs=[pl.BlockSpec((tm, D), lambda i: (i, 0)),
                  pl.BlockSpec((1, D), lambda i: (0, 0))],
        out_specs=pl.BlockSpec((tm, D), lambda i: (i, 0)),
        out_shape=jax.ShapeDtypeStruct((M, D), out_dtype),
        compiler_params=_cparams("parallel"),
        name="rmsnorm",
    )(x, w.reshape(1, D))


def _mm_kernel(a_ref, b_ref, o_ref, *, precise):
    o_ref[...] = _dg(a_ref[...], b_ref[...], _NN, precise).astype(o_ref.dtype)


def _matmul(a, b, n, col0=0, *, precise, out_dtype=F32, tm=1024, tn=512):
    M, K = a.shape
    tm = _pick(M, tm, SUBLANES)
    tn = _pick(n, tn)
    if col0 % tn:
        b, col0 = b[:, col0:col0 + n], 0
    c0 = col0 // tn
    return pl.pallas_call(
        functools.partial(_mm_kernel, precise=precise),
        grid=(M // tm, n // tn),
        in_specs=[pl.BlockSpec((tm, K), lambda i, j: (i, 0)),
                  pl.BlockSpec((K, tn), lambda i, j: (0, c0 + j))],
        out_specs=pl.BlockSpec((tm, tn), lambda i, j: (i, j)),
        out_shape=jax.ShapeDtypeStruct((M, n), out_dtype),
        compiler_params=_cparams("parallel", "arbitrary"),
        name="matmul",
    )(a, b)


def _merge_kernel(a1_ref, b1_ref, a2_ref, b2_ref, g1_ref, g2_ref, gb1_ref, gb2_ref, o_ref, *, precise):
    o1 = _dg(a1_ref[...], b1_ref[...], _NN, precise)
    o2 = _dg(a2_ref[...], b2_ref[...], _NN, precise)
    s1 = _sigmoid(g1_ref[...] + gb1_ref[...])
    s2 = _sigmoid(g2_ref[...] + gb2_ref[...])
    o_ref[...] = (s1 * o1 + s2 * o2).astype(o_ref.dtype)


def _merge(a1, b1, a2, b2, gates, gate_b, *, precise, out_dtype, tm=512, tn=512):
    M, K = a1.shape
    K2 = a2.shape[1]
    D = b1.shape[1]
    tm = _pick(M, tm, SUBLANES)
    tn = _pick(D, tn)
    nj = D // tn
    gb = gate_b.reshape(1, 2 * D)
    return pl.pallas_call(
        functools.partial(_merge_kernel, precise=precise),
        grid=(M // tm, nj),
        in_specs=[pl.BlockSpec((tm, K), lambda i, j: (i, 0)),
                  pl.BlockSpec((K, tn), lambda i, j: (0, j)),
                  pl.BlockSpec((tm, K2), lambda i, j: (i, 0)),
                  pl.BlockSpec((K2, tn), lambda i, j: (0, j)),
                  pl.BlockSpec((tm, tn), lambda i, j: (i, j)),
                  pl.BlockSpec((tm, tn), lambda i, j: (i, nj + j)),
                  pl.BlockSpec((1, tn), lambda i, j: (0, j)),
                  pl.BlockSpec((1, tn), lambda i, j: (0, nj + j))],
        out_specs=pl.BlockSpec((tm, tn), lambda i, j: (i, j)),
        out_shape=jax.ShapeDtypeStruct((M, D), out_dtype),
        compiler_params=_cparams("parallel", "arbitrary"),
        name="merge",
    )(a1, b1, a2, b2, gates, gates, gb, gb)


def _mm_res_kernel(a_ref, b_ref, r_ref, o_ref, *, precise):
    o_ref[...] = r_ref[...] + _dg(a_ref[...], b_ref[...], _NN, precise)


def _matmul_residual(a, b, res, *, precise, tm=1024, tn=512):
    M, K = a.shape
    N = b.shape[1]
    tm = _pick(M, tm, SUBLANES)
    tn = _pick(N, tn)
    return pl.pallas_call(
        functools.partial(_mm_res_kernel, precise=precise),
        grid=(M // tm, N // tn),
        in_specs=[pl.BlockSpec((tm, K), lambda i, j: (i, 0)),
                  pl.BlockSpec((K, tn), lambda i, j: (0, j)),
                  pl.BlockSpec((tm, tn), lambda i, j: (i, j))],
        out_specs=pl.BlockSpec((tm, tn), lambda i, j: (i, j)),
        out_shape=jax.ShapeDtypeStruct((M, N), F32),
        compiler_params=_cparams("parallel", "arbitrary"),
        name="matmul_residual",
    )(a, b, res)


def _qkrope_kernel(qkv_ref, cos_ref, sin_ref, qw_ref, kw_ref, *out_refs, nq, nk, hd, vcols, scale, emit_bf16):
    if emit_bf16:
        q_ref, k_ref, v_ref, kb_ref, vb_ref = out_refs
    else:
        q_ref, k_ref, v_ref = out_refs
    cos = cos_ref[...]
    sin = sin_ref[...]

    def norm_rope(x, w):
        ms = jnp.mean(x * x, axis=-1, keepdims=True)
        xn = x * lax.rsqrt(ms + NORM_EPS) * w
        return xn * cos + pltpu.roll(xn, hd // 2, axis=1) * sin

    qw = qw_ref[...]
    kw = kw_ref[...]
    for c in range(nq):
        o = norm_rope(qkv_ref[:, c * hd:(c + 1) * hd], qw) * scale
        q_ref[:, c * hd:(c + 1) * hd] = o.astype(q_ref.dtype)
    for c in range(nk):
        o = norm_rope(qkv_ref[:, (nq + c) * hd:(nq + c + 1) * hd], kw)
        k_ref[:, c * hd:(c + 1) * hd] = o
        if emit_bf16:
            kb_ref[:, c * hd:(c + 1) * hd] = o.astype(BF16)
    v = qkv_ref[:, (nq + nk) * hd:(nq + nk) * hd + vcols]
    v_ref[...] = v
    if emit_bf16:
        vb_ref[...] = v.astype(BF16)


def _qk_rope(qkv, cos_full, sin_signed, q_norm_w, k_norm_w, qcols, kcols, vcols, *, q_dtype, emit_bf16, tq=256):
    T = qkv.shape[0]
    hd = q_norm_w.shape[0]
    tq = _pick(T, tq, SUBLANES)
    nq, nk = qcols // hd, kcols // hd
    row = lambda i: (i, 0)
    out_shape = [jax.ShapeDtypeStruct((T, qcols), q_dtype),
                 jax.ShapeDtypeStruct((T, kcols), F32),
                 jax.ShapeDtypeStruct((T, vcols), F32)]
    out_specs = [pl.BlockSpec((tq, qcols), row), pl.BlockSpec((tq, kcols), row), pl.BlockSpec((tq, vcols), row)]
    if emit_bf16:
        out_shape += [jax.ShapeDtypeStruct((T, kcols), BF16), jax.ShapeDtypeStruct((T, vcols), BF16)]
        out_specs += [pl.BlockSpec((tq, kcols), row), pl.BlockSpec((tq, vcols), row)]
    return pl.pallas_call(
        functools.partial(_qkrope_kernel, nq=nq, nk=nk, hd=hd, vcols=vcols, scale=hd ** -0.5,
                          emit_bf16=emit_bf16),
        grid=(T // tq,),
        in_specs=[pl.BlockSpec((tq, qkv.shape[1]), row),
                  pl.BlockSpec((tq, hd), row), pl.BlockSpec((tq, hd), row),
                  pl.BlockSpec((1, hd), lambda i: (0, 0)), pl.BlockSpec((1, hd), lambda i: (0, 0))],
        out_specs=out_specs,
        out_shape=out_shape,
        compiler_params=_cparams("parallel"),
        name="qk_norm_rope",
    )(qkv, cos_full, sin_signed, q_norm_w.reshape(1, hd), k_norm_w.reshape(1, hd))


def _rope_tables(positions, hd):
    inv = 1.0 / (ROPE_THETA ** (jnp.arange(0, hd, 2, dtype=F32) / hd))
    ang = positions.astype(F32)[:, None] * inv[None, :]
    cos, sin = jnp.cos(ang), jnp.sin(ang)
    return jnp.concatenate([cos, cos], axis=-1), jnp.concatenate([-sin, sin], axis=-1)


def _diff_lambda_in_kernel(lq1_ref, lk1_ref, lq2_ref, lk2_ref, lam_init):
    e1 = jnp.exp(jnp.sum(lq1_ref[...] * lk1_ref[...], axis=-1, keepdims=True))
    e2 = jnp.exp(jnp.sum(lq2_ref[...] * lk2_ref[...], axis=-1, keepdims=True))
    return e1 - e2 + lam_init


def _attn_finalize(acc1, l1, acc2, l2, lam, subw, lam_init):
    o = acc1 / l1 - lam * (acc2 / l2)
    ms = jnp.mean(o * o, axis=-1, keepdims=True)
    return o * lax.rsqrt(ms + NORM_EPS) * subw * (1.0 - lam_init)


def _flash_kernel(q_ref, k_ref, v_ref, lq1_ref, lk1_ref, lq2_ref, lk2_ref, subw_ref, o_ref,
                  m_sc, l_sc, acc_sc, *, G, hd, vd, tq, lam_init):
    qi = pl.program_id(1)
    ki = pl.program_id(2)

    @pl.when(ki == 0)
    def _():
        m_sc[...] = jnp.full(m_sc.shape, -jnp.inf, F32)
        l_sc[...] = jnp.zeros(l_sc.shape, F32)
        acc_sc[...] = jnp.zeros(acc_sc.shape, F32)

    @pl.when(ki <= qi)
    def _():
        rows = lax.broadcasted_iota(jnp.int32, (tq, tq), 0)
        cols = lax.broadcasted_iota(jnp.int32, (tq, tq), 1)
        allowed = jnp.logical_or(cols <= rows, ki < qi)
        v = v_ref[...]
        for m in range(2):
            kb = k_ref[:, m * hd:(m + 1) * hd]
            for g in range(G):
                idx = g * 2 + m
                q = q_ref[:, idx * hd:(idx + 1) * hd]
                s = lax.dot_general(q, kb, _NT, preferred_element_type=F32)
                s = jnp.where(allowed, s, -jnp.inf)
                m_prev = m_sc[idx]
                m_new = jnp.maximum(m_prev, jnp.max(s, axis=-1, keepdims=True))
                p = jnp.exp(s - m_new)
                alpha = jnp.exp(m_prev - m_new)
                l_sc[idx] = alpha * l_sc[idx] + jnp.sum(p, axis=-1, keepdims=True)
                acc_sc[idx] = alpha * acc_sc[idx] + jnp.dot(p.astype(BF16), v, preferred_element_type=F32)
                m_sc[idx] = m_new

    @pl.when(ki == qi)
    def _():
        lam = _diff_lambda_in_kernel(lq1_ref, lk1_ref, lq2_ref, lk2_ref, lam_init)
        subw = subw_ref[...]
        for g in range(G):
            o = _attn_finalize(acc_sc[2 * g], l_sc[2 * g], acc_sc[2 * g + 1], l_sc[2 * g + 1],
                               lam, subw, lam_init)
            o_ref[:, g * vd:(g + 1) * vd] = o.astype(o_ref.dtype)


def _flash_diff_attn(q, kb, vb, lams, subln_w, *, HK, G, hd, vd, lam_init, out_dtype, tq=512):
    T = q.shape[0]
    tq = _pick(T, tq, SUBLANES)
    nq = T // tq
    vec = lambda n: pl.BlockSpec((1, n), lambda h, i, j: (0, 0))
    return pl.pallas_call(
        functools.partial(_flash_kernel, G=G, hd=hd, vd=vd, tq=tq, lam_init=lam_init),
        grid=(HK, nq, nq),
        in_specs=[pl.BlockSpec((tq, G * 2 * hd), lambda h, i, j: (i, h)),
                  pl.BlockSpec((tq, 2 * hd), lambda h, i, j: (jnp.minimum(i, j), h)),
                  pl.BlockSpec((tq, vd), lambda h, i, j: (jnp.minimum(i, j), h)),
                  vec(hd), vec(hd), vec(hd), vec(hd), vec(vd)],
        out_specs=pl.BlockSpec((tq, G * vd), lambda h, i, j: (i, h)),
        out_shape=jax.ShapeDtypeStruct((T, HK * G * vd), out_dtype),
        scratch_shapes=[pltpu.VMEM((2 * G, tq, 1), F32), pltpu.VMEM((2 * G, tq, 1), F32),
                        pltpu.VMEM((2 * G, tq, vd), F32)],
        compiler_params=_cparams("parallel", "parallel", "arbitrary"),
        name="flash_diff_attn",
    )(q, kb, vb, *[x.reshape(1, -1) for x in lams], subln_w.reshape(1, vd))


def _decode_kernel(pt_ref, q_ref, kn_ref, vn_ref, kc_ref, vc_ref, lq1_ref, lk1_ref, lq2_ref, lk2_ref,
                   subw_ref, o_ref, m_sc, l_sc, acc_sc, *, HK, G, hd, vd, lam_init):
    p = pl.program_id(1)
    q = q_ref[0]

    def qrow(hk, g, m):
        c = ((hk * G + g) * 2 + m) * hd
        return q[:, c:c + hd]

    @pl.when(p == 0)
    def _():
        kn = kn_ref[0]
        vn = vn_ref[0]
        for hk in range(HK):
            for g in range(G):
                for m in range(2):
                    idx = (hk * G + g) * 2 + m
                    kr = kn[:, (hk * 2 + m) * hd:(hk * 2 + m + 1) * hd]
                    s = jnp.sum(qrow(hk, g, m) * kr, axis=-1, keepdims=True)
                    m_sc[idx] = jnp.broadcast_to(s, (1, LANES))
                    l_sc[idx] = jnp.ones((1, LANES), F32)
                    acc_sc[idx] = vn[:, hk * vd:(hk + 1) * vd]

    kp = kc_ref[0]
    vp = vc_ref[0]
    for hk in range(HK):
        vs = vp[:, hk * vd:(hk + 1) * vd]
        for m in range(2):
            ks = kp[:, (hk * 2 + m) * hd:(hk * 2 + m + 1) * hd]
            for g in range(G):
                idx = (hk * G + g) * 2 + m
                s = jnp.sum(ks * qrow(hk, g, m), axis=-1, keepdims=True)
                m_prev = m_sc[idx][:, :1]
                m_new = jnp.maximum(m_prev, jnp.max(s, axis=0, keepdims=True))
                pe = jnp.exp(s - m_new)
                alpha = jnp.exp(m_prev - m_new)
                l_new = alpha * l_sc[idx][:, :1] + jnp.sum(pe, axis=0, keepdims=True)
                acc_sc[idx] = alpha * acc_sc[idx] + jnp.sum(pe * vs, axis=0, keepdims=True)
                l_sc[idx] = jnp.broadcast_to(l_new, (1, LANES))
                m_sc[idx] = jnp.broadcast_to(m_new, (1, LANES))

    @pl.when(p == pl.num_programs(1) - 1)
    def _():
        lam = _diff_lambda_in_kernel(lq1_ref, lk1_ref, lq2_ref, lk2_ref, lam_init)
        subw = subw_ref[...]
        for hk in range(HK):
            for g in range(G):
                i1 = (hk * G + g) * 2
                o = _attn_finalize(acc_sc[i1], l_sc[i1][:, :1], acc_sc[i1 + 1], l_sc[i1 + 1][:, :1],
                                   lam, subw, lam_init)
                c = (hk * G + g) * vd
                o_ref[0, :, c:c + vd] = o


def _decode_diff_attn(q, k_new, v_new, cache_k, cache_v, page_table, lams, subln_w, *, HK, G, hd, vd, lam_init):
    DB, n_pages = page_table.shape
    n_phys, page = cache_k.shape[0], cache_k.shape[1]
    kc = cache_k.reshape(n_phys, page, HK * 2 * hd)
    vc = cache_v.reshape(n_phys, page, HK * vd)
    NI = HK * G * 2
    qc, kcw, vcw = q.shape[1], k_new.shape[1], v_new.shape[1]
    per_b = lambda n: pl.BlockSpec((1, 1, n), lambda b, p, pt: (b, 0, 0))
    vec = lambda n: pl.BlockSpec((1, n), lambda b, p, pt: (0, 0))
    grid_spec = pltpu.PrefetchScalarGridSpec(
        num_scalar_prefetch=1,
        grid=(DB, n_pages),
        in_specs=[per_b(qc), per_b(kcw), per_b(vcw),
                  pl.BlockSpec((1, page, kcw), lambda b, p, pt: (pt[b, p], 0, 0)),
                  pl.BlockSpec((1, page, vcw), lambda b, p, pt: (pt[b, p], 0, 0)),
                  vec(hd), vec(hd), vec(hd), vec(hd), vec(vd)],
        out_specs=pl.BlockSpec((1, 1, HK * G * vd), lambda b, p, pt: (b, 0, 0)),
        scratch_shapes=[pltpu.VMEM((NI, 1, LANES), F32), pltpu.VMEM((NI, 1, LANES), F32),
                        pltpu.VMEM((NI, 1, vd), F32)],
    )
    out = pl.pallas_call(
        functools.partial(_decode_kernel, HK=HK, G=G, hd=hd, vd=vd, lam_init=lam_init),
        grid_spec=grid_spec,
        out_shape=jax.ShapeDtypeStruct((DB, 1, HK * G * vd), F32),
        compiler_params=_cparams("parallel", "arbitrary"),
        name="decode_diff_attn",
    )(page_table, q.reshape(DB, 1, qc), k_new.reshape(DB, 1, kcw), v_new.reshape(DB, 1, vcw), kc, vc,
      *[x.reshape(1, -1) for x in lams], subln_w.reshape(1, vd))
    return out.reshape(DB, HK * G * vd)


def _segsum_matrix(n, seg):
    r = lax.broadcasted_iota(jnp.int32, (n, n), 0) // seg
    c = lax.broadcasted_iota(jnp.int32, (n, n), 1) // seg
    return (r == c).astype(F32)


def _prep_kernel(*refs, halo, rh, precise, tc):
    if halo:
        (r_ref, k_ref, v_ref, l_ref, rp_ref, kp_ref, vp_ref, lp_ref, ri_ref, ki_ref, vi_ref, li_ref,
         mur_ref, muk_ref, muv_ref, mul_ref, w0_ref, w2_ref, a0_ref, a2_ref, g2_ref, kkw_ref, kaw_ref,
         ro_ref, lwo_ref, ko_ref, vo_ref, kko_ref, bo_ref, go_ref) = refs
    else:
        (r_ref, k_ref, v_ref, l_ref, rp_ref, kp_ref, vp_ref, lp_ref,
         mur_ref, muk_ref, muv_ref, mul_ref, w0_ref, w2_ref, a0_ref, a2_ref, g2_ref, kkw_ref, kaw_ref,
         ro_ref, lwo_ref, ko_ref, vo_ref, kko_ref, bo_ref, go_ref) = refs
    i = pl.program_id(0)

    def shifted(x_ref, prev_ref, init_ref):
        x = x_ref[...]
        if not halo:
            return x, prev_ref[...]
        last = prev_ref[SUBLANES - 1:SUBLANES, :]
        first = jnp.where(i == 0, init_ref[...], last)
        xs = pltpu.roll(x, 1, axis=0)
        rows = lax.broadcasted_iota(jnp.int32, x.shape, 0)
        return x, jnp.where(rows == 0, first, xs)

    def mix(x_ref, prev_ref, init_ref, mu_ref):
        x, xp = shifted(x_ref, prev_ref, init_ref)
        return x + (xp - x) * mu_ref[...]

    r = mix(r_ref, rp_ref, ri_ref if halo else None, mur_ref)
    k = mix(k_ref, kp_ref, ki_ref if halo else None, muk_ref)
    v = mix(v_ref, vp_ref, vi_ref if halo else None, muv_ref)
    lo = mix(l_ref, lp_ref, li_ref if halo else None, mul_ref)
    dl = w2_ref.shape[0]
    al = a2_ref.shape[0]
    xw = lo[:, :dl]
    xa = lo[:, dl:dl + al]
    xg = lo[:, dl + al:]
    wl = w0_ref[...] + _dg(jnp.tanh(xw), w2_ref[...], _NN, precise)
    lw = -jnp.exp(-_softplus(-wl) - 0.5)
    a = _sigmoid(a0_ref[...] + _dg(xa, a2_ref[...], _NN, precise))
    g = _dg(_sigmoid(xg), g2_ref[...], _NN, precise)
    seg = _segsum_matrix(tc, rh)
    kk = k * kkw_ref[...]
    ss = _dg(kk * kk, seg, _NN, precise)
    kk = kk / jnp.maximum(jnp.sqrt(ss), 1e-12)
    kf = k * (1.0 + (a - 1.0) * kaw_ref[...])
    b = kk * a
    for pi in range(tc // LANES):
        sl = slice(pi * LANES, (pi + 1) * LANES)
        ro_ref[pi] = r[:, sl]
        lwo_ref[pi] = lw[:, sl]
        ko_ref[pi] = kf[:, sl]
        vo_ref[pi] = v[:, sl]
        kko_ref[pi] = kk[:, sl]
        bo_ref[pi] = b[:, sl]
        go_ref[pi] = g[:, sl]


def _rwkv_prep(rkv, lora, prev, mu_rkv, mu_lora, w0, w2, a0, a2, g2p, k_k, k_a, *, rh, precise, tt=256, tc=512):
    T, W3 = rkv.shape
    W = W3 // 3
    LP = lora.shape[1]
    tc = _pick(W, tc)
    nj = W // tc
    halo = prev[0].shape[0] != T or T == 1
    tt = _pick(T, tt, SUBLANES)
    hb = tt // SUBLANES
    xblk = lambda off: pl.BlockSpec((tt, tc), lambda i, j: (i, off * nj + j))
    pblk = lambda off: pl.BlockSpec((SUBLANES, tc), lambda i, j: (jnp.maximum(i * hb - 1, 0), off * nj + j))
    cvec = lambda off: pl.BlockSpec((1, tc), lambda i, j: (0, off * nj + j))
    in_specs = [xblk(0), xblk(1), xblk(2), pl.BlockSpec((tt, LP), lambda i, j: (i, 0))]
    args = [rkv, rkv, rkv, lora]
    if halo:
        in_specs += [pblk(0), pblk(1), pblk(2),
                     pl.BlockSpec((SUBLANES, LP), lambda i, j: (jnp.maximum(i * hb - 1, 0), 0)),
                     cvec(0), cvec(1), cvec(2), pl.BlockSpec((1, LP), lambda i, j: (0, 0))]
        args += [rkv, rkv, rkv, lora, prev[0], prev[0], prev[0], prev[1]]
    else:
        in_specs += [xblk(0), xblk(1), xblk(2), pl.BlockSpec((tt, LP), lambda i, j: (i, 0))]
        args += [prev[0], prev[0], prev[0], prev[1]]
    in_specs += [cvec(0), cvec(1), cvec(2), pl.BlockSpec((1, LP), lambda i, j: (0, 0)),
                 cvec(0), pl.BlockSpec((w2.shape[0], tc), lambda i, j: (0, j)),
                 cvec(0), pl.BlockSpec((a2.shape[0], tc), lambda i, j: (0, j)),
                 pl.BlockSpec((g2p.shape[0], tc), lambda i, j: (0, j)),
                 cvec(0), cvec(0)]
    args += [mu_rkv, mu_rkv, mu_rkv, mu_lora, w0.reshape(1, W), w2, a0.reshape(1, W), a2, g2p,
             k_k.reshape(1, W), k_a.reshape(1, W)]
    npair = tc // LANES
    oshape = jax.ShapeDtypeStruct((W // LANES, T, LANES), F32)
    ospec = pl.BlockSpec((npair, tt, LANES), lambda i, j: (j, i, 0))
    return pl.pallas_call(
        functools.partial(_prep_kernel, halo=halo, rh=rh, precise=precise, tc=tc),
        grid=(T // tt, nj),
        in_specs=in_specs,
        out_specs=[ospec] * 7,
        out_shape=[oshape] * 7,
        compiler_params=_cparams("parallel", "arbitrary"),
        name="rwkv_prep",
    )(*args)


def _scan_kernel(r_ref, lw_ref, k_ref, v_ref, kk_ref, b_ref, g_ref, s0_ref, rk_ref, lnw_ref, lnb_ref,
                 y_ref, sout_ref, s_sc, *, C, rh, npair, precise):
    c = pl.program_id(1)
    R = 2 * C
    lane = lax.broadcasted_iota(jnp.int32, (C, LANES), 1)
    head0 = lane < rh
    ri = lax.broadcasted_iota(jnp.int32, (R, R), 0)
    ci = lax.broadcasted_iota(jnp.int32, (R, R), 1)
    same = (ri // C) == (ci // C)
    strict = jnp.logical_and(same, (ci % C) < (ri % C))
    incl = jnp.logical_and(same, (ci % C) <= (ri % C))
    tril_c = (lax.broadcasted_iota(jnp.int32, (C, C), 1)
              <= lax.broadcasted_iota(jnp.int32, (C, C), 0)).astype(F32)
    seg = _segsum_matrix(LANES, rh)
    zero = jnp.zeros((rh, rh), F32)

    def stack_masked(x):
        return jnp.concatenate([jnp.where(head0, x, 0.0), jnp.where(head0, 0.0, x)], axis=0)

    def stack_dup(x):
        return jnp.concatenate([x, x], axis=0)

    def mm(a, b, dn=_NN):
        return _dg(a, b, dn, precise)

    @pl.when(c == 0)
    def _():
        def init(p, carry):
            top = jnp.concatenate([s0_ref[0, 2 * p], zero], axis=1)
            bot = jnp.concatenate([zero, s0_ref[0, 2 * p + 1]], axis=1)
            s_sc[p] = jnp.concatenate([top, bot], axis=0)
            return carry
        lax.fori_loop(0, npair, init, 0)

    def body(p, carry):
        r = r_ref[p]
        lw = lw_ref[p]
        k = k_ref[p]
        v = v_ref[p]
        kk = kk_ref[p]
        b = b_ref[p]
        cum = lax.dot_general(tril_c, lw, _NN, precision=HIGHEST, preferred_element_type=F32)
        tot = cum[C - 1:C, :]
        e_incl = jnp.exp(cum)
        e_inv = jnp.exp(-cum)
        e_end = jnp.exp(tot - cum)
        a2 = stack_masked(-kk * jnp.exp(cum - lw))
        r2 = stack_masked(r * e_incl)
        v2 = stack_masked(v)
        bd2 = stack_masked(b * e_end)
        kd2 = stack_masked(k * e_end)
        b2 = stack_dup(b * e_inv)
        k2 = stack_dup(k * e_inv)
        S = s_sc[p]
        ar = jnp.concatenate([a2, r2], axis=0)
        arb = mm(ar, b2, _NT)
        ark = mm(ar, k2, _NT)
        L = jnp.where(strict, arb[:R], 0.0)
        lak = jnp.where(strict, ark[:R], 0.0)
        mrb = jnp.where(incl, arb[R:], 0.0)
        mrk = jnp.where(incl, ark[R:], 0.0)
        ars = mm(ar, S, _NT)
        x = ars[:R] + mm(lak, v2)
        pw = L
        n_it = max(1, (C - 1).bit_length())
        for it in range(n_it):
            x = x + mm(pw, x)
            if it + 1 < n_it:
                pw = mm(pw, pw)
        y2 = ars[R:] + mm(mrb, x) + mm(mrk, v2)
        y = y2[:C] + y2[C:]
        s_sc[p] = S * jnp.exp(tot) + mm(x, bd2, _TN) + mm(v2, kd2, _TN)
        mean = mm(y, seg) * (1.0 / rh)
        d = y - mean
        var = mm(d * d, seg) * (1.0 / rh)
        yn = d * lax.rsqrt(var + GN_EPS) * lnw_ref[p] + lnb_ref[p]
        bonus = mm(r * k * rk_ref[p], seg) * v
        y_ref[p] = ((yn + bonus) * g_ref[p]).astype(y_ref.dtype)
        return carry

    lax.fori_loop(0, npair, body, 0)

    @pl.when(c == pl.num_programs(1) - 1)
    def _():
        def fin(p, carry):
            S = s_sc[p]
            sout_ref[0, 2 * p] = S[:rh, :rh]
            sout_ref[0, 2 * p + 1] = S[rh:, rh:]
            return carry
        lax.fori_loop(0, npair, fin, 0)


def _rwkv_scan(prep, s0, r_k, ln_w, ln_b, *, B, C, rh, precise, out_dtype):
    npair, BT, _ = prep[0].shape
    Tb = BT // B
    nc = Tb // C
    H = s0.shape[1]
    xspec = pl.BlockSpec((npair, C, LANES), lambda b, c: (0, b * nc + c, 0))
    sspec = pl.BlockSpec((1, H, rh, rh), lambda b, c: (b, 0, 0, 0))
    pspec = pl.BlockSpec((npair, 1, LANES), lambda b, c: (0, 0, 0))
    y, s_out = pl.pallas_call(
        functools.partial(_scan_kernel, C=C, rh=rh, npair=npair, precise=precise),
        grid=(B, nc),
        in_specs=[xspec] * 7 + [sspec, pspec, pspec, pspec],
        out_specs=[xspec, sspec],
        out_shape=[jax.ShapeDtypeStruct((npair, BT, LANES), out_dtype),
                   jax.ShapeDtypeStruct((B, H, rh, rh), F32)],
        scratch_shapes=[pltpu.VMEM((npair, 2 * rh, 2 * rh), F32)],
        compiler_params=_cparams("parallel", "arbitrary"),
        name="rwkv_scan",
    )(*prep, s0, r_k.reshape(npair, 1, LANES), ln_w.reshape(npair, 1, LANES), ln_b.reshape(npair, 1, LANES))
    return y, s_out


def _router_kernel(x_ref, nw_ref, rw_ref, rb_ref, h_ref, lg_ref, *, precise):
    x = x_ref[...]
    ms = jnp.mean(x * x, axis=-1, keepdims=True)
    h = x * lax.rsqrt(ms + NORM_EPS) * nw_ref[...]
    h_ref[...] = h.astype(h_ref.dtype)
    lg_ref[...] = _dg(h, rw_ref[...], _NN, precise) + rb_ref[...]


def _router(x, norm_w, rw, rb, *, precise, h_dtype, tm=512):
    M, D = x.shape
    NR = rw.shape[1]
    tm = _pick(M, tm, SUBLANES)
    return pl.pallas_call(
        functools.partial(_router_kernel, precise=precise),
        grid=(M // tm,),
        in_specs=[pl.BlockSpec((tm, D), lambda i: (i, 0)), pl.BlockSpec((1, D), lambda i: (0, 0)),
                  pl.BlockSpec((D, NR), lambda i: (0, 0)), pl.BlockSpec((1, NR), lambda i: (0, 0))],
        out_specs=[pl.BlockSpec((tm, D), lambda i: (i, 0)), pl.BlockSpec((tm, NR), lambda i: (i, 0))],
        out_shape=[jax.ShapeDtypeStruct((M, D), h_dtype), jax.ShapeDtypeStruct((M, NR), F32)],
        compiler_params=_cparams("parallel"),
        name="moe_router",
    )(x, norm_w.reshape(1, D), rw, rb)


def _moe_kernel(be_ref, nu_ref, x_ref, wg_ref, wu_ref, wd_ref, sw_ref, o_ref):
    b = pl.program_id(0)
    j = pl.program_id(1)
    last = j == pl.num_programs(1) - 1
    used = b < nu_ref[0]

    @pl.when(jnp.logical_and(used, j == 0))
    def _():
        o_ref[...] = jnp.zeros(o_ref.shape, F32)

    @pl.when(used)
    def _():
        x = x_ref[...]
        hg = jnp.dot(x, wg_ref[0].astype(BF16), preferred_element_type=F32)
        hu = jnp.dot(x, wu_ref[0].astype(BF16), preferred_element_type=F32)
        h = hg * _sigmoid(hg) * hu
        o_ref[...] += jnp.dot(h.astype(BF16), wd_ref[0].astype(BF16), preferred_element_type=F32)

    @pl.when(jnp.logical_and(used, last))
    def _():
        o_ref[...] = o_ref[...] * sw_ref[...]

    @pl.when(jnp.logical_and(jnp.logical_not(used), last))
    def _():
        o_ref[...] = jnp.zeros(o_ref.shape, F32)


def _moe_experts(xs, blk_e, n_used, slot_w, wg, wu, wd, *, blk, tn):
    P, D = xs.shape
    DE = wg.shape[2]
    tn = _pick(DE, tn)
    nj = DE // tn
    nb = P // blk

    def eff(b, j, nu):
        live = b < nu[0]
        return jnp.where(live, b, nu[0] - 1), jnp.where(live, j, nj - 1)

    def wg_map(b, j, be, nu):
        bb, jj = eff(b, j, nu)
        return be[bb], 0, jj

    def wd_map(b, j, be, nu):
        bb, jj = eff(b, j, nu)
        return be[bb], jj, 0

    def x_map(b, j, be, nu):
        return eff(b, j, nu)[0], 0

    grid_spec = pltpu.PrefetchScalarGridSpec(
        num_scalar_prefetch=2,
        grid=(nb, nj),
        in_specs=[pl.BlockSpec((blk, D), x_map),
                  pl.BlockSpec((1, D, tn), wg_map), pl.BlockSpec((1, D, tn), wg_map),
                  pl.BlockSpec((1, tn, D), wd_map),
                  pl.BlockSpec((blk, 1), x_map)],
        out_specs=pl.BlockSpec((blk, D), lambda b, j, be, nu: (b, 0)),
    )
    return pl.pallas_call(
        _moe_kernel,
        grid_spec=grid_spec,
        out_shape=jax.ShapeDtypeStruct((P, D), F32),
        compiler_params=_cparams("arbitrary", "arbitrary"),
        name="moe_experts",
    )(blk_e, n_used, xs, wg, wu, wd, slot_w.reshape(P, 1))


def _hier_moe(x, norm_w, rg_w, rg_b, re_w, re_b, wg, wu, wd, *, precise, blk, tn):
    T, D = x.shape
    NG = rg_w.shape[1]
    E = re_w.shape[1]
    EPG = E // NG
    NR = -(-(NG + E) // LANES) * LANES
    rw = jnp.pad(jnp.concatenate([rg_w, re_w], axis=1), ((0, 0), (0, NR - NG - E)))
    rb = jnp.pad(jnp.concatenate([rg_b, re_b]), (0, NR - NG - E)).reshape(1, NR)
    if not precise:
        rw = rw.astype(BF16)
    h, logits = _router(x, norm_w, rw, rb, precise=precise, h_dtype=BF16)
    gprob = jax.nn.softmax(logits[:, :NG], axis=-1)
    gsel = jnp.argmax(gprob, axis=-1)
    gp = jnp.take_along_axis(gprob, gsel[:, None], axis=-1)[:, 0]
    elog = logits[:, NG:NG + E].reshape(T, NG, EPG)
    elog = jnp.take_along_axis(elog, gsel[:, None, None], axis=1)[:, 0]
    top_p, top_i = lax.top_k(jax.nn.softmax(elog, axis=-1), TOP_K)
    top_p = top_p / jnp.sum(top_p, axis=-1, keepdims=True)
    weights = gp[:, None] * top_p
    expert_idx = gsel[:, None].astype(jnp.int32) * EPG + top_i.astype(jnp.int32)

    TK = T * TOP_K
    nb = -(-TK // blk) + E
    P = nb * blk
    flat_e = expert_idx.reshape(-1)
    flat_w = weights.reshape(-1)
    flat_t = jnp.repeat(jnp.arange(T, dtype=jnp.int32), TOP_K)
    order = jnp.argsort(flat_e)
    se = flat_e[order]
    counts = jnp.bincount(flat_e, length=E)
    pcounts = (counts + blk - 1) // blk * blk
    pend = jnp.cumsum(pcounts)
    pstart = pend - pcounts
    start = jnp.cumsum(counts) - counts
    dest = (pstart[se] + jnp.arange(TK) - start[se]).astype(jnp.int32)
    slot_t = jnp.full((P,), T, jnp.int32).at[dest].set(flat_t[order])
    slot_w = jnp.zeros((P,), F32).at[dest].set(flat_w[order])
    blk_e = jnp.minimum(jnp.searchsorted(pend, jnp.arange(nb) * blk, side='right'), E - 1).astype(jnp.int32)
    n_used = (pend[-1] // blk).astype(jnp.int32).reshape(1)
    pos = jnp.zeros((TK,), jnp.int32).at[order].set(dest)

    xs = jnp.concatenate([h, jnp.zeros((1, D), h.dtype)], axis=0)[slot_t]
    ys = _moe_experts(xs, blk_e, n_used, slot_w, wg, wu, wd, blk=blk, tn=tn)
    pos = pos.reshape(T, TOP_K)
    return x + ys[pos[:, 0]] + ys[pos[:, 1]]


def _pairs_to_rows(y):
    npair, T, _ = y.shape
    return jnp.transpose(y, (1, 0, 2)).reshape(T, npair * LANES)


def _layer(x, positions, prev_rows, s0, B, cfg, w, *, precise, attn_fn):
    T, D = x.shape
    hd, HK, G, vd, rh = cfg['hd'], cfg['HK'], cfg['G'], cfg['vd'], cfg['rh']
    qcols, kcols, vcols = HK * G * 2 * hd, HK * 2 * hd, HK * vd
    W = cfg['W']
    act = F32 if precise else BF16
    mm = functools.partial(_matmul, precise=precise)

    h = _rmsnorm(x, w['norm1_w'], act)
    if precise:
        qkv = mm(h, w['w_in'], qcols + kcols + vcols, 0)
        rkv = mm(h, w['w_in'], 3 * W, cfg['r_off'])
        lora = mm(h, w['w_in'], cfg['lp'], cfg['r_off'] + 3 * W, tn=cfg['lp'])
    else:
        qkv = mm(h, w['w_qkv'], qcols + kcols + vcols)
        rkv = mm(h, w['w_rkv'], 3 * W)
        lora = mm(h, w['w_lora'], cfg['lp'], tn=cfg['lp'])
    gates = mm(h, w['w_gates'], 2 * D)

    cos_full, sin_signed = _rope_tables(positions, hd)
    outs = _qk_rope(qkv, cos_full, sin_signed, w['q_norm_w'], w['k_norm_w'], qcols, kcols, vcols,
                    q_dtype=act, emit_bf16=not precise)
    q, k_new, v_new = outs[:3]
    o_attn = attn_fn(q, outs)

    prep = _rwkv_prep(rkv, lora, prev_rows, w['mu_rkv'], w['mu_lora'], w['rwkv_w0'], w['rwkv_w2'],
                      w['rwkv_a0'], w['rwkv_a2'], w['g2p'], w['rwkv_k_k'], w['rwkv_k_a'],
                      rh=rh, precise=precise)
    Tb = T // B
    if Tb % 64 == 0:
        C = 64
    else:
        C = SUBLANES
        pad = -Tb % C
        prep = [jnp.pad(a.reshape(-1, B, Tb, LANES), ((0, 0), (0, 0), (0, pad), (0, 0)))
                .reshape(a.shape[0], B * (Tb + pad), LANES) for a in prep]
    y_r, s_fin = _rwkv_scan(prep, s0, w['rwkv_r_k'], w['rwkv_ln_w'], w['rwkv_ln_b'],
                            B=B, C=C, rh=rh, precise=precise, out_dtype=act)
    if Tb % 64:
        y_r = y_r.reshape(y_r.shape[0], B, -1, LANES)[:, :, :Tb].reshape(y_r.shape[0], T, LANES)
    o_rwkv = _pairs_to_rows(y_r)

    mixed = _merge(o_attn, w['w_branch_attn'], o_rwkv, w['w_branch_rwkv'], gates, w['gate_b'],
                   precise=precise, out_dtype=act)
    x1 = _matmul_residual(mixed, w['w_out'], x, precise=precise)
    y = _hier_moe(x1, w['norm2_w'], w['router_group_w'], w['router_group_b'], w['router_expert_w'],
                  w['router_expert_b'], w['expert_w_gate'], w['expert_w_up'], w['expert_w_down'],
                  precise=precise, blk=cfg['moe_blk_p'] if not precise else cfg['moe_blk_s'],
                  tn=cfg['moe_tn_p'] if not precise else cfg['moe_tn_s'])
    return y, k_new, v_new, s_fin, rkv, lora


def kernel(x_prompt, x_sample, cache_k, cache_v, state_rwkv, state_shift, page_table, norm1_w, w_in, gate_b, q_norm_w, k_norm_w, lambda_q1, lambda_k1, lambda_q2, lambda_k2, subln_w, rwkv_mu, rwkv_w0, rwkv_w2, rwkv_a0, rwkv_a2, rwkv_g2, rwkv_k_k, rwkv_k_a, rwkv_r_k, rwkv_ln_w, rwkv_ln_b, w_branch_attn, w_branch_rwkv, w_out, norm2_w, router_group_w, router_group_b, router_expert_w, router_expert_b, expert_w_gate, expert_w_up, expert_w_down):
    B, T, D = x_prompt.shape
    DB, S, _ = x_sample.shape
    L = norm1_w.shape[0]
    assert L == 1 and B == 1 and S == 1, "one trunk layer, one prompt sequence, one new token per sample"
    hd = q_norm_w.shape[1]
    HK = cache_k.shape[3]
    vd = cache_v.shape[4]
    G = w_branch_attn.shape[1] // vd // HK
    H, rh = rwkv_r_k.shape[1], rwkv_r_k.shape[2]
    W = H * rh
    dl, al, gl = rwkv_w2.shape[1], rwkv_a2.shape[1], rwkv_g2.shape[1]
    lcols = dl + al + gl
    lp = -(-lcols // LANES) * LANES
    qkv_cols = HK * G * 2 * hd + HK * 2 * hd + HK * vd
    r_off = qkv_cols
    g_off = r_off + 3 * W + lcols
    n_pages, page = page_table.shape[1], cache_k.shape[2]
    past = n_pages * page
    cfg = dict(hd=hd, HK=HK, G=G, vd=vd, rh=rh, W=W, lp=lp, r_off=r_off,
               moe_blk_p=min(512, max(SUBLANES * 2, T // 16)), moe_tn_p=128,
               moe_blk_s=2 * SUBLANES, moe_tn_s=256)
    l = 0
    lam_init = 0.8 - 0.6 * math.exp(-0.3 * l)
    lams = (lambda_q1[l], lambda_k1[l], lambda_q2[l], lambda_k2[l])

    w_in_l = w_in[l]
    w_gates = w_in_l[:, g_off:g_off + 2 * D]
    mu = rwkv_mu[l]
    common = dict(
        norm1_w=norm1_w[l], q_norm_w=q_norm_w[l], k_norm_w=k_norm_w[l], gate_b=gate_b[l],
        mu_rkv=mu[:3 * W].reshape(1, 3 * W), mu_lora=jnp.pad(mu[3 * W:], (0, lp - lcols)).reshape(1, lp),
        rwkv_w0=rwkv_w0[l], rwkv_a0=rwkv_a0[l], rwkv_k_k=rwkv_k_k[l], rwkv_k_a=rwkv_k_a[l],
        rwkv_r_k=rwkv_r_k[l], rwkv_ln_w=rwkv_ln_w[l], rwkv_ln_b=rwkv_ln_b[l],
        norm2_w=norm2_w[l], router_group_w=router_group_w[l], router_group_b=router_group_b[l],
        router_expert_w=router_expert_w[l], router_expert_b=router_expert_b[l],
        expert_w_gate=expert_w_gate[l], expert_w_up=expert_w_up[l], expert_w_down=expert_w_down[l])
    g2p = jnp.pad(rwkv_g2[l], ((0, lp - lcols), (0, 0)))
    w_lora_f32 = jnp.pad(w_in_l[:, r_off + 3 * W:r_off + 3 * W + lcols], ((0, 0), (0, lp - lcols)))
    wp = dict(common,
              w_qkv=w_in_l[:, :qkv_cols].astype(BF16), w_rkv=w_in_l[:, r_off:r_off + 3 * W].astype(BF16),
              w_lora=w_lora_f32.astype(BF16), w_gates=w_gates.astype(BF16),
              rwkv_w2=rwkv_w2[l].astype(BF16), rwkv_a2=rwkv_a2[l].astype(BF16), g2p=g2p.astype(BF16),
              w_branch_attn=w_branch_attn[l].astype(BF16), w_branch_rwkv=w_branch_rwkv[l].astype(BF16),
              w_out=w_out[l].astype(BF16))
    ws = dict(common, w_in=w_in_l, w_gates=w_gates, rwkv_w2=rwkv_w2[l], rwkv_a2=rwkv_a2[l], g2p=g2p,
              w_branch_attn=w_branch_attn[l], w_branch_rwkv=w_branch_rwkv[l], w_out=w_out[l])

    def prompt_attn(q, outs):
        return _flash_diff_attn(q, outs[3], outs[4], lams, subln_w[l], HK=HK, G=G, hd=hd, vd=vd,
                                lam_init=lam_init, out_dtype=BF16)

    zero_prev = (jnp.zeros((1, 3 * W), F32), jnp.zeros((1, lp), F32))
    yp, kp, vp, sp, rkv_p, lora_p = _layer(
        x_prompt.reshape(T, D), jnp.arange(T), zero_prev, jnp.zeros((1, H, rh, rh), F32), 1, cfg, wp,
        precise=False, attn_fn=prompt_attn)
    shift_p = jnp.concatenate([rkv_p[T - 1], lora_p[T - 1, :lcols]])

    def sample_attn(q, outs):
        return _decode_diff_attn(q, outs[1], outs[2], cache_k[l], cache_v[l], page_table, lams, subln_w[l],
                                 HK=HK, G=G, hd=hd, vd=vd, lam_init=lam_init)

    sh = state_shift[l]
    prev_s = (sh[:, :3 * W], jnp.pad(sh[:, 3 * W:], ((0, 0), (0, lp - lcols))))
    ys, ks, vs, ss, rkv_s, lora_s = _layer(
        x_sample.reshape(DB, D), jnp.full((DB,), past, jnp.int32), prev_s, state_rwkv[l], DB, cfg, ws,
        precise=True, attn_fn=sample_attn)
    shift_s = jnp.concatenate([rkv_s, lora_s[:, :lcols]], axis=1)

    return (yp.reshape(1, T, D), ys.reshape(DB, 1, D),
            kp.reshape(1, 1, T, HK, 2, hd), vp.reshape(1, 1, T, HK, vd),
            sp.reshape(1, 1, H, rh, rh), shift_p.reshape(1, 1, -1),
            ks.reshape(1, DB, 1, HK, 2, hd), vs.reshape(1, DB, 1, HK, vd),
            ss.reshape(1, DB, H, rh, rh), shift_s.reshape(1, DB, -1))
```

```python
import functools
import math

import jax
import jax.numpy as jnp
from jax import lax
from jax.experimental import pallas as pl
from jax.experimental.pallas import tpu as pltpu

F32 = jnp.float32
BF16 = jnp.bfloat16
HIGHEST = lax.Precision.HIGHEST

LANES = 128
SUBLANES = 8
VMEM_LIMIT = 56 * 1024 * 1024

ROPE_THETA = 10000.0
NORM_EPS = 1e-6
GN_EPS = 64e-5
TOP_K = 2
SLABS_PER_GROUP = 16
DECODE_PAGES_PER_STEP = 8

_NN = (((1,), (0,)), ((), ()))
_NT = (((1,), (1,)), ((), ()))
_TN = (((0,), (0,)), ((), ()))


def _cparams(*sem):
    return pltpu.CompilerParams(dimension_semantics=sem, vmem_limit_bytes=VMEM_LIMIT)


def _pick(n, pref, unit=LANES):
    if n <= pref:
        return n
    t = pref // unit * unit
    while t > unit and n % t:
        t -= unit
    assert n % t == 0, (n, pref)
    return t


def _dg(a, b, dn, precise):
    if precise:
        return lax.dot_general(a.astype(F32), b.astype(F32), dn, precision=HIGHEST,
                               preferred_element_type=F32)
    return lax.dot_general(a.astype(BF16), b.astype(BF16), dn, preferred_element_type=F32)


def _sigmoid(x):
    return 1.0 / (1.0 + jnp.exp(-x))


def _softplus(x):
    return jnp.maximum(x, 0.0) + jnp.log(1.0 + jnp.exp(-jnp.abs(x)))


def _rms_kernel(x_ref, w_ref, o_ref, *, eps):
    x = x_ref[...]
    ms = jnp.mean(x * x, axis=-1, keepdims=True)
    o_ref[...] = (x * lax.rsqrt(ms + eps) * w_ref[...]).astype(o_ref.dtype)


def _rmsnorm(x, w, out_dtype, tm=512):
    M, D = x.shape
    tm = _pick(M, tm, SUBLANES)
    return pl.pallas_call(
        functools.partial(_rms_kernel, eps=NORM_EPS),
        grid=(M // tm,),
        in_specs=[pl.BlockSpec((tm, D), lambda i: (i, 0)),
                  pl.BlockSpec((1, D), lambda i: (0, 0))],
        out_specs=pl.BlockSpec((tm, D), lambda i: (i, 0)),
        out_shape=jax.ShapeDtypeStruct((M, D), out_dtype),
        compiler_params=_cparams("parallel"),
        name="rmsnorm",
    )(x, w.reshape(1, D))


def _mm_kernel(a_ref, b_ref, o_ref, *, precise):
    o_ref[...] = _dg(a_ref[...], b_ref[...], _NN, precise).astype(o_ref.dtype)


def _matmul(a, b, n, col0=0, *, precise, out_dtype=F32, tm=1024, tn=512):
    M, K = a.shape
    tm = _pick(M, tm, SUBLANES)
    tn = _pick(n, tn)
    if col0 % tn:
        b, col0 = b[:, col0:col0 + n], 0
    c0 = col0 // tn
    return pl.pallas_call(
        functools.partial(_mm_kernel, precise=precise),
        grid=(M // tm, n // tn),
        in_specs=[pl.BlockSpec((tm, K), lambda i, j: (i, 0)),
                  pl.BlockSpec((K, tn), lambda i, j: (0, c0 + j))],
        out_specs=pl.BlockSpec((tm, tn), lambda i, j: (i, j)),
        out_shape=jax.ShapeDtypeStruct((M, n), out_dtype),
        compiler_params=_cparams("parallel", "arbitrary"),
        name="matmul",
    )(a, b)


def _merge_kernel(a1_ref, b1_ref, a2_ref, b2_ref, g1_ref, g2_ref, gb1_ref, gb2_ref, o_ref, *, precise):
    o1 = _dg(a1_ref[...], b1_ref[...], _NN, precise)
    o2 = _dg(a2_ref[...], b2_ref[...], _NN, precise)
    s1 = _sigmoid(g1_ref[...] + gb1_ref[...])
    s2 = _sigmoid(g2_ref[...] + gb2_ref[...])
    o_ref[...] = (s1 * o1 + s2 * o2).astype(o_ref.dtype)


def _merge(a1, b1, a2, b2, gates, gate_b, *, precise, out_dtype, tm=512, tn=512):
    M, K = a1.shape
    K2 = a2.shape[1]
    D = b1.shape[1]
    tm = _pick(M, tm, SUBLANES)
    tn = _pick(D, tn)
    nj = D // tn
    gb = gate_b.reshape(1, 2 * D)
    return pl.pallas_call(
        functools.partial(_merge_kernel, precise=precise),
        grid=(M // tm, nj),
        in_specs=[pl.BlockSpec((tm, K), lambda i, j: (i, 0)),
                  pl.BlockSpec((K, tn), lambda i, j: (0, j)),
                  pl.BlockSpec((tm, K2), lambda i, j: (i, 0)),
                  pl.BlockSpec((K2, tn), lambda i, j: (0, j)),
                  pl.BlockSpec((tm, tn), lambda i, j: (i, j)),
                  pl.BlockSpec((tm, tn), lambda i, j: (i, nj + j)),
                  pl.BlockSpec((1, tn), lambda i, j: (0, j)),
                  pl.BlockSpec((1, tn), lambda i, j: (0, nj + j))],
        out_specs=pl.BlockSpec((tm, tn), lambda i, j: (i, j)),
        out_shape=jax.ShapeDtypeStruct((M, D), out_dtype),
        compiler_params=_cparams("parallel", "arbitrary"),
        name="merge",
    )(a1, b1, a2, b2, gates, gates, gb, gb)


def _mm_res_kernel(a_ref, b_ref, r_ref, o_ref, *, precise):
    o_ref[...] = r_ref[...] + _dg(a_ref[...], b_ref[...], _NN, precise)


def _matmul_residual(a, b, res, *, precise, tm=1024, tn=512):
    M, K = a.shape
    N = b.shape[1]
    tm = _pick(M, tm, SUBLANES)
    tn = _pick(N, tn)
    return pl.pallas_call(
        functools.partial(_mm_res_kernel, precise=precise),
        grid=(M // tm, N // tn),
        in_specs=[pl.BlockSpec((tm, K), lambda i, j: (i, 0)),
                  pl.BlockSpec((K, tn), lambda i, j: (0, j)),
                  pl.BlockSpec((tm, tn), lambda i, j: (i, j))],
        out_specs=pl.BlockSpec((tm, tn), lambda i, j: (i, j)),
        out_shape=jax.ShapeDtypeStruct((M, N), F32),
        compiler_params=_cparams("parallel", "arbitrary"),
        name="matmul_residual",
    )(a, b, res)


def _qkrope_kernel(qkv_ref, cos_ref, sin_ref, qw_ref, kw_ref, *out_refs, nq, nk, hd, vcols, scale, emit_bf16):
    if emit_bf16:
        q_ref, k_ref, v_ref, kb_ref, vb_ref = out_refs
    else:
        q_ref, k_ref, v_ref = out_refs
    cos = cos_ref[...]
    sin = sin_ref[...]

    def norm_rope(x, w):
        ms = jnp.mean(x * x, axis=-1, keepdims=True)
        xn = x * lax.rsqrt(ms + NORM_EPS) * w
        return xn * cos + pltpu.roll(xn, hd // 2, axis=1) * sin

    qw = qw_ref[...]
    kw = kw_ref[...]
    for c in range(nq):
        o = norm_rope(qkv_ref[:, c * hd:(c + 1) * hd], qw) * scale
        q_ref[:, c * hd:(c + 1) * hd] = o.astype(q_ref.dtype)
    for c in range(nk):
        o = norm_rope(qkv_ref[:, (nq + c) * hd:(nq + c + 1) * hd], kw)
        k_ref[:, c * hd:(c + 1) * hd] = o
        if emit_bf16:
            kb_ref[:, c * hd:(c + 1) * hd] = o.astype(BF16)
    v = qkv_ref[:, (nq + nk) * hd:(nq + nk) * hd + vcols]
    v_ref[...] = v
    if emit_bf16:
        vb_ref[...] = v.astype(BF16)


def _qk_rope(qkv, cos_full, sin_signed, q_norm_w, k_norm_w, qcols, kcols, vcols, *, q_scale, q_dtype,
             emit_bf16, tq=256):
    T = qkv.shape[0]
    hd = q_norm_w.shape[0]
    tq = _pick(T, tq, SUBLANES)
    nq, nk = qcols // hd, kcols // hd
    row = lambda i: (i, 0)
    out_shape = [jax.ShapeDtypeStruct((T, qcols), q_dtype),
                 jax.ShapeDtypeStruct((T, kcols), F32),
                 jax.ShapeDtypeStruct((T, vcols), F32)]
    out_specs = [pl.BlockSpec((tq, qcols), row), pl.BlockSpec((tq, kcols), row), pl.BlockSpec((tq, vcols), row)]
    if emit_bf16:
        out_shape += [jax.ShapeDtypeStruct((T, kcols), BF16), jax.ShapeDtypeStruct((T, vcols), BF16)]
        out_specs += [pl.BlockSpec((tq, kcols), row), pl.BlockSpec((tq, vcols), row)]
    return pl.pallas_call(
        functools.partial(_qkrope_kernel, nq=nq, nk=nk, hd=hd, vcols=vcols, scale=q_scale,
                          emit_bf16=emit_bf16),
        grid=(T // tq,),
        in_specs=[pl.BlockSpec((tq, qkv.shape[1]), row),
                  pl.BlockSpec((tq, hd), row), pl.BlockSpec((tq, hd), row),
                  pl.BlockSpec((1, hd), lambda i: (0, 0)), pl.BlockSpec((1, hd), lambda i: (0, 0))],
        out_specs=out_specs,
        out_shape=out_shape,
        compiler_params=_cparams("parallel"),
        name="qk_norm_rope",
    )(qkv, cos_full, sin_signed, q_norm_w.reshape(1, hd), k_norm_w.reshape(1, hd))


def _rope_tables(positions, hd):
    inv = 1.0 / (ROPE_THETA ** (jnp.arange(0, hd, 2, dtype=F32) / hd))
    ang = positions.astype(F32)[:, None] * inv[None, :]
    cos, sin = jnp.cos(ang), jnp.sin(ang)
    return jnp.concatenate([cos, cos], axis=-1), jnp.concatenate([-sin, sin], axis=-1)


def _diff_lambda_in_kernel(lq1_ref, lk1_ref, lq2_ref, lk2_ref, lam_init):
    e1 = jnp.exp(jnp.sum(lq1_ref[...] * lk1_ref[...], axis=-1, keepdims=True))
    e2 = jnp.exp(jnp.sum(lq2_ref[...] * lk2_ref[...], axis=-1, keepdims=True))
    return e1 - e2 + lam_init


def _attn_finalize(acc1, l1, acc2, l2, lam, subw, lam_init):
    o = acc1 / l1 - lam * (acc2 / l2)
    ms = jnp.mean(o * o, axis=-1, keepdims=True)
    return o * lax.rsqrt(ms + NORM_EPS) * subw * (1.0 - lam_init)


def _flash_kernel(q_ref, k_ref, v_ref, lq1_ref, lk1_ref, lq2_ref, lk2_ref, subw_ref, o_ref,
                  m_sc, l_sc, acc_sc, s_sc, *, G, hd, vd, tq, lam_init):
    qi = pl.program_id(1)
    ki = pl.program_id(2)

    @pl.when(ki == 0)
    def _():
        m_sc[...] = jnp.full(m_sc.shape, -jnp.inf, F32)
        l_sc[...] = jnp.zeros(l_sc.shape, F32)
        acc_sc[...] = jnp.zeros(acc_sc.shape, F32)

    def tile(diagonal):
        v = v_ref[...]
        for m in range(2):
            kb = k_ref[:, m * hd:(m + 1) * hd]
            for g in range(G):
                idx = g * 2 + m
                q = q_ref[:, idx * hd:(idx + 1) * hd]
                s = lax.dot_general(q, kb, _NT, preferred_element_type=F32)
                if diagonal:
                    rows = lax.broadcasted_iota(jnp.int32, (tq, tq), 0)
                    cols = lax.broadcasted_iota(jnp.int32, (tq, tq), 1)
                    s = jnp.where(cols <= rows, s, -jnp.inf)
                s_sc[...] = s
                m_prev = m_sc[idx]
                m_new = jnp.maximum(m_prev, jnp.max(s_sc[...], axis=-1, keepdims=True))
                m_sc[idx] = m_new
                alpha = jnp.exp2(m_prev - m_new)
                p = jnp.exp2(s_sc[...] - jnp.tile(m_new, (1, tq // LANES)))
                l_sc[idx] = alpha * l_sc[idx] + jnp.sum(p, axis=-1, keepdims=True)
                pv = jnp.dot(p.astype(BF16), v, preferred_element_type=F32)
                acc_sc[idx] = jnp.tile(alpha, (1, vd // LANES)) * acc_sc[idx] + pv

    @pl.when(ki < qi)
    def _():
        tile(False)

    @pl.when(ki == qi)
    def _():
        tile(True)
        lam = _diff_lambda_in_kernel(lq1_ref, lk1_ref, lq2_ref, lk2_ref, lam_init)
        subw = subw_ref[...]
        for g in range(G):
            o = _attn_finalize(acc_sc[2 * g], l_sc[2 * g][:, :1], acc_sc[2 * g + 1], l_sc[2 * g + 1][:, :1],
                               lam, subw, lam_init)
            o_ref[:, g * vd:(g + 1) * vd] = o.astype(o_ref.dtype)


def _flash_diff_attn(q, kb, vb, lams, subln_w, *, HK, G, hd, vd, lam_init, out_dtype, tq=512):
    T = q.shape[0]
    tq = _pick(T, tq)
    nq = T // tq
    vec = lambda n: pl.BlockSpec((1, n), lambda h, i, j: (0, 0))
    return pl.pallas_call(
        functools.partial(_flash_kernel, G=G, hd=hd, vd=vd, tq=tq, lam_init=lam_init),
        grid=(HK, nq, nq),
        in_specs=[pl.BlockSpec((tq, G * 2 * hd), lambda h, i, j: (i, h)),
                  pl.BlockSpec((tq, 2 * hd), lambda h, i, j: (jnp.minimum(i, j), h)),
                  pl.BlockSpec((tq, vd), lambda h, i, j: (jnp.minimum(i, j), h)),
                  vec(hd), vec(hd), vec(hd), vec(hd), vec(vd)],
        out_specs=pl.BlockSpec((tq, G * vd), lambda h, i, j: (i, h)),
        out_shape=jax.ShapeDtypeStruct((T, HK * G * vd), out_dtype),
        scratch_shapes=[pltpu.VMEM((2 * G, tq, LANES), F32), pltpu.VMEM((2 * G, tq, LANES), F32),
                        pltpu.VMEM((2 * G, tq, vd), F32), pltpu.VMEM((tq, tq), F32)],
        compiler_params=_cparams("parallel", "parallel", "arbitrary"),
        name="flash_diff_attn",
    )(q, kb, vb, *[x.reshape(1, -1) for x in lams], subln_w.reshape(1, vd))


def _decode_kernel(pt_ref, q_ref, kn_ref, vn_ref, *refs, HK, G, vd, pps, scale, lam_init):
    kc_refs, vc_refs = refs[:pps], refs[pps:2 * pps]
    (lq1_ref, lk1_ref, lq2_ref, lk2_ref, subw_ref, o_ref, s_sc, a_sc, anew_sc, acc_sc) = refs[2 * pps:]
    ph = pl.program_id(1)
    pg = pl.program_id(2)
    last = pl.num_programs(2) - 1
    NI = q_ref.shape[1]
    half = NI // 2
    page = kc_refs[0].shape[1]
    q = q_ref[0]
    row_hk = lax.broadcasted_iota(jnp.int32, (half, vd), 0) // G

    def own_head(x):
        out = x[:, :vd]
        for hk in range(1, HK):
            out = jnp.where(row_hk == hk, x[:, hk * vd:(hk + 1) * vd], out)
        return out

    @pl.when(ph == 0)
    def _():
        s_sc[pg] = scale * jnp.concatenate(
            [lax.dot_general(q, kc[0].astype(BF16), _NT, preferred_element_type=F32) for kc in kc_refs],
            axis=1)

    @pl.when(jnp.logical_and(ph == 0, pg == last))
    def _():
        lam = _diff_lambda_in_kernel(lq1_ref, lk1_ref, lq2_ref, lk2_ref, lam_init)
        s_new = scale * jnp.sum(q.astype(F32) * kn_ref[0].astype(BF16).astype(F32), axis=-1, keepdims=True)
        s = s_sc[...]
        m = jnp.maximum(jnp.max(jnp.max(s, axis=0), axis=-1, keepdims=True), s_new)
        e = jnp.exp(s - m[None])
        e_new = jnp.exp(s_new - m)
        denom = jnp.sum(jnp.sum(e, axis=0), axis=-1, keepdims=True) + e_new
        p = e / denom[None]
        p_new = e_new / denom
        a_sc[...] = (p[:, :half] - lam * p[:, half:]).astype(BF16)
        anew_sc[...] = jnp.broadcast_to(p_new[:half] - lam * p_new[half:], (half, LANES))
        acc_sc[...] = jnp.zeros(acc_sc.shape, F32)

    @pl.when(ph == 1)
    def _():
        a = a_sc[pg]
        pv = sum(jnp.dot(a[:, j * page:(j + 1) * page], vc[0].astype(BF16), preferred_element_type=F32)
                 for j, vc in enumerate(vc_refs))
        acc_sc[...] += own_head(pv)

    @pl.when(jnp.logical_and(ph == 1, pg == last))
    def _():
        a_new = anew_sc[...][:, :1].astype(BF16).astype(F32)
        v_new = own_head(jnp.broadcast_to(vn_ref[0].astype(BF16).astype(F32), (half, HK * vd)))
        o = acc_sc[...] + a_new * v_new
        ms = jnp.mean(o * o, axis=-1, keepdims=True)
        o_ref[0] = o * lax.rsqrt(ms + NORM_EPS) * subw_ref[...] * (1.0 - lam_init)


def _decode_diff_attn(q, k_new, v_new, cache_k, cache_v, page_table, lams, subln_w, *, HK, G, hd, vd, lam_init):
    DB, n_pages = page_table.shape
    n_phys, page = cache_k.shape[0], cache_k.shape[1]
    assert page % LANES == 0 and vd % LANES == 0
    kcw, vcw = HK * 2 * hd, HK * vd
    kc = cache_k.reshape(n_phys, page, kcw)
    vc = cache_v.reshape(n_phys, page, vcw)
    NI = 2 * HK * G
    qt = jnp.transpose(q.reshape(DB, HK, G, 2, hd), (0, 3, 1, 2, 4))
    own = (jnp.eye(HK, dtype=q.dtype)[None, None, :, None, :, None, None]
           * jnp.eye(2, dtype=q.dtype)[None, :, None, None, None, :, None])
    qbd = (qt[:, :, :, :, None, None, :] * own).reshape(DB, NI, kcw)
    pps = math.gcd(n_pages, DECODE_PAGES_PER_STEP)
    steps = n_pages // pps
    per_b = lambda r, n: pl.BlockSpec((1, r, n), lambda b, ph, pg, pt: (b, 0, 0))
    vec = lambda n: pl.BlockSpec((1, n), lambda b, ph, pg, pt: (0, 0))

    def k_spec(j):
        return pl.BlockSpec((1, page, kcw),
                            lambda b, ph, pg, pt: (pt[b, jnp.where(ph == 0, pg, steps - 1) * pps + j], 0, 0))

    def v_spec(j):
        return pl.BlockSpec((1, page, vcw),
                            lambda b, ph, pg, pt: (pt[b, jnp.where(ph == 1, pg, 0) * pps + j], 0, 0))

    grid_spec = pltpu.PrefetchScalarGridSpec(
        num_scalar_prefetch=1,
        grid=(DB, 2, steps),
        in_specs=[per_b(NI, kcw), per_b(1, kcw), per_b(1, vcw)]
        + [k_spec(j) for j in range(pps)] + [v_spec(j) for j in range(pps)]
        + [vec(hd), vec(hd), vec(hd), vec(hd), vec(vd)],
        out_specs=pl.BlockSpec((1, NI // 2, vd), lambda b, ph, pg, pt: (b, 0, 0)),
        scratch_shapes=[pltpu.VMEM((steps, NI, pps * page), F32), pltpu.VMEM((steps, NI // 2, pps * page), BF16),
                        pltpu.VMEM((NI // 2, LANES), F32), pltpu.VMEM((NI // 2, vd), F32)],
    )
    out = pl.pallas_call(
        functools.partial(_decode_kernel, HK=HK, G=G, vd=vd, pps=pps, scale=hd ** -0.5, lam_init=lam_init),
        grid_spec=grid_spec,
        out_shape=jax.ShapeDtypeStruct((DB, NI // 2, vd), F32),
        compiler_params=_cparams("parallel", "arbitrary", "arbitrary"),
        name="decode_diff_attn",
    )(page_table, qbd, k_new.reshape(DB, 1, kcw), v_new.reshape(DB, 1, vcw), *([kc] * pps), *([vc] * pps),
      *[x.reshape(1, -1) for x in lams], subln_w.reshape(1, vd))
    return out.reshape(DB, HK * G * vd)


def _segsum_matrix(n, seg):
    r = lax.broadcasted_iota(jnp.int32, (n, n), 0) // seg
    c = lax.broadcasted_iota(jnp.int32, (n, n), 1) // seg
    return (r == c).astype(F32)


def _prep_kernel(*refs, halo, rh, exact, tc):
    if halo:
        (r_ref, k_ref, v_ref, l_ref, rp_ref, kp_ref, vp_ref, lp_ref, ri_ref, ki_ref, vi_ref, li_ref,
         mur_ref, muk_ref, muv_ref, mul_ref, w0_ref, w2_ref, a0_ref, a2_ref, g2_ref, kkw_ref, kaw_ref,
         ro_ref, lwo_ref, ko_ref, vo_ref, kko_ref, bo_ref, go_ref) = refs
    else:
        (r_ref, k_ref, v_ref, l_ref, rp_ref, kp_ref, vp_ref, lp_ref,
         mur_ref, muk_ref, muv_ref, mul_ref, w0_ref, w2_ref, a0_ref, a2_ref, g2_ref, kkw_ref, kaw_ref,
         ro_ref, lwo_ref, ko_ref, vo_ref, kko_ref, bo_ref, go_ref) = refs
    i = pl.program_id(0)

    def shifted(x_ref, prev_ref, init_ref):
        x = x_ref[...]
        if not halo:
            return x, prev_ref[...]
        last = prev_ref[SUBLANES - 1:SUBLANES, :]
        first = jnp.where(i == 0, init_ref[...], last)
        xs = pltpu.roll(x, 1, axis=0)
        rows = lax.broadcasted_iota(jnp.int32, x.shape, 0)
        return x, jnp.where(rows == 0, first, xs)

    def mix(x_ref, prev_ref, init_ref, mu_ref):
        x, xp = shifted(x_ref, prev_ref, init_ref)
        return x + (xp - x) * mu_ref[...]

    r = mix(r_ref, rp_ref, ri_ref if halo else None, mur_ref)
    k = mix(k_ref, kp_ref, ki_ref if halo else None, muk_ref)
    v = mix(v_ref, vp_ref, vi_ref if halo else None, muv_ref)
    lo = mix(l_ref, lp_ref, li_ref if halo else None, mul_ref)
    dl = w2_ref.shape[0]
    al = a2_ref.shape[0]
    xw = lo[:, :dl]
    xa = lo[:, dl:dl + al]
    xg = lo[:, dl + al:]
    wl = w0_ref[...] + _dg(jnp.tanh(xw), w2_ref[...], _NN, False)
    lw = -jnp.exp(-_softplus(-wl) - 0.5)
    a = _sigmoid(a0_ref[...] + _dg(xa, a2_ref[...], _NN, False))
    g = _dg(_sigmoid(xg), g2_ref[...], _NN, False)
    seg = _segsum_matrix(tc, rh)
    kk = k * kkw_ref[...]
    ss = _dg(kk * kk, seg, _NN, exact)
    kk = kk / jnp.maximum(jnp.sqrt(ss), 1e-12)
    kf = k * (1.0 + (a - 1.0) * kaw_ref[...])
    b = kk * a
    for pi in range(tc // LANES):
        sl = slice(pi * LANES, (pi + 1) * LANES)
        ro_ref[pi] = r[:, sl]
        lwo_ref[pi] = lw[:, sl]
        ko_ref[pi] = kf[:, sl]
        vo_ref[pi] = v[:, sl]
        kko_ref[pi] = kk[:, sl]
        bo_ref[pi] = b[:, sl]
        go_ref[pi] = g[:, sl]


def _rwkv_prep(rkv, lora, prev, mu_rkv, mu_lora, w0, w2, a0, a2, g2p, k_k, k_a, *, rh, exact, tt=256, tc=512):
    T, W3 = rkv.shape
    W = W3 // 3
    LP = lora.shape[1]
    tc = _pick(W, tc)
    nj = W // tc
    halo = prev[0].shape[0] != T or T == 1
    tt = _pick(T, tt, SUBLANES)
    hb = tt // SUBLANES
    xblk = lambda off: pl.BlockSpec((tt, tc), lambda i, j: (i, off * nj + j))
    pblk = lambda off: pl.BlockSpec((SUBLANES, tc), lambda i, j: (jnp.maximum(i * hb - 1, 0), off * nj + j))
    cvec = lambda off: pl.BlockSpec((1, tc), lambda i, j: (0, off * nj + j))
    in_specs = [xblk(0), xblk(1), xblk(2), pl.BlockSpec((tt, LP), lambda i, j: (i, 0))]
    args = [rkv, rkv, rkv, lora]
    if halo:
        in_specs += [pblk(0), pblk(1), pblk(2),
                     pl.BlockSpec((SUBLANES, LP), lambda i, j: (jnp.maximum(i * hb - 1, 0), 0)),
                     cvec(0), cvec(1), cvec(2), pl.BlockSpec((1, LP), lambda i, j: (0, 0))]
        args += [rkv, rkv, rkv, lora, prev[0], prev[0], prev[0], prev[1]]
    else:
        in_specs += [xblk(0), xblk(1), xblk(2), pl.BlockSpec((tt, LP), lambda i, j: (i, 0))]
        args += [prev[0], prev[0], prev[0], prev[1]]
    in_specs += [cvec(0), cvec(1), cvec(2), pl.BlockSpec((1, LP), lambda i, j: (0, 0)),
                 cvec(0), pl.BlockSpec((w2.shape[0], tc), lambda i, j: (0, j)),
                 cvec(0), pl.BlockSpec((a2.shape[0], tc), lambda i, j: (0, j)),
                 pl.BlockSpec((g2p.shape[0], tc), lambda i, j: (0, j)),
                 cvec(0), cvec(0)]
    args += [mu_rkv, mu_rkv, mu_rkv, mu_lora, w0.reshape(1, W), w2, a0.reshape(1, W), a2, g2p,
             k_k.reshape(1, W), k_a.reshape(1, W)]
    npair = tc // LANES
    oshape = jax.ShapeDtypeStruct((W // LANES, T, LANES), F32)
    ospec = pl.BlockSpec((npair, tt, LANES), lambda i, j: (j, i, 0))
    return pl.pallas_call(
        functools.partial(_prep_kernel, halo=halo, rh=rh, exact=exact, tc=tc),
        grid=(T // tt, nj),
        in_specs=in_specs,
        out_specs=[ospec] * 7,
        out_shape=[oshape] * 7,
        compiler_params=_cparams("parallel", "arbitrary"),
        name="rwkv_prep",
    )(*args)


def _scan_kernel(r_ref, lw_ref, k_ref, v_ref, kk_ref, b_ref, g_ref, s0_ref, rk_ref, lnw_ref, lnb_ref,
                 y_ref, sout_ref, s_sc, *, C, rh, npair, precise):
    c = pl.program_id(1)
    R = 2 * C
    lane = lax.broadcasted_iota(jnp.int32, (C, LANES), 1)
    head0 = lane < rh
    ri = lax.broadcasted_iota(jnp.int32, (R, R), 0)
    ci = lax.broadcasted_iota(jnp.int32, (R, R), 1)
    same = (ri // C) == (ci // C)
    strict = jnp.logical_and(same, (ci % C) < (ri % C))
    incl = jnp.logical_and(same, (ci % C) <= (ri % C))
    tril_c = (lax.broadcasted_iota(jnp.int32, (C, C), 1)
              <= lax.broadcasted_iota(jnp.int32, (C, C), 0)).astype(F32)
    seg = _segsum_matrix(LANES, rh)
    zero = jnp.zeros((rh, rh), F32)

    def stack_masked(x):
        return jnp.concatenate([jnp.where(head0, x, 0.0), jnp.where(head0, 0.0, x)], axis=0)

    def stack_dup(x):
        return jnp.concatenate([x, x], axis=0)

    @pl.when(c == 0)
    def _():
        def init(p, carry):
            top = jnp.concatenate([s0_ref[0, 2 * p], zero], axis=1)
            bot = jnp.concatenate([zero, s0_ref[0, 2 * p + 1]], axis=1)
            s_sc[p] = jnp.concatenate([top, bot], axis=0)
            return carry
        lax.fori_loop(0, npair, init, 0)

    def slab(r, lw, k, v, kk, b, g, S, rk, lnw, lnb):
        cum = yield tril_c, lw, _NN, True
        tot = cum[C - 1:C, :]
        e_incl = jnp.exp(cum)
        e_inv = jnp.exp(-cum)
        e_end = jnp.exp(tot - cum)
        a2 = stack_masked(-kk * jnp.exp(cum - lw))
        r2 = stack_masked(r * e_incl)
        v2 = stack_masked(v)
        bd2 = stack_masked(b * e_end)
        kd2 = stack_masked(k * e_end)
        b2 = stack_dup(b * e_inv)
        k2 = stack_dup(k * e_inv)
        ar = jnp.concatenate([a2, r2], axis=0)
        arb = yield ar, b2, _NT, precise
        ark = yield ar, k2, _NT, precise
        ars = yield ar, S, _NT, precise
        L = jnp.where(strict, arb[:R], 0.0)
        lak = jnp.where(strict, ark[:R], 0.0)
        mrb = jnp.where(incl, arb[R:], 0.0)
        mrk = jnp.where(incl, ark[R:], 0.0)
        x = ars[:R] + (yield lak, v2, _NN, precise)
        pw = L
        n_it = max(1, (C - 1).bit_length())
        for it in range(n_it):
            x = x + (yield pw, x, _NN, precise)
            if it + 1 < n_it:
                pw = yield pw, pw, _NN, precise
        y2 = ars[R:] + (yield mrb, x, _NN, precise)
        y2 = y2 + (yield mrk, v2, _NN, precise)
        y = y2[:C] + y2[C:]
        s_new = S * jnp.exp(tot) + (yield x, bd2, _TN, precise)
        s_new = s_new + (yield v2, kd2, _TN, precise)
        mean = (yield y, seg, _NN, precise) * (1.0 / rh)
        d = y - mean
        var = (yield d * d, seg, _NN, precise) * (1.0 / rh)
        yn = d * lax.rsqrt(var + GN_EPS) * lnw + lnb
        bonus = (yield r * k * rk, seg, _NN, precise) * v
        return (yn + bonus) * g, s_new

    def run_lockstep(gens):
        reqs = [next(gen) for gen in gens]
        results = [None] * len(gens)
        live = list(range(len(gens)))
        while live:
            vals = [_dg(*reqs[i]) for i in live]
            nxt = []
            for i, val in zip(live, vals):
                try:
                    reqs[i] = gens[i].send(val)
                    nxt.append(i)
                except StopIteration as done:
                    results[i] = done.value
            live = nxt
        return results

    U = math.gcd(npair, SLABS_PER_GROUP)

    def body(i, carry):
        grp = pl.ds(pl.multiple_of(i * U, U), U)
        ins = [ref[grp] for ref in (r_ref, lw_ref, k_ref, v_ref, kk_ref, b_ref, g_ref, s_sc,
                                    rk_ref, lnw_ref, lnb_ref)]
        outs = run_lockstep([slab(*[a[u] for a in ins]) for u in range(U)])
        y_ref[grp] = jnp.stack([o[0] for o in outs]).astype(y_ref.dtype)
        s_sc[grp] = jnp.stack([o[1] for o in outs])
        return carry

    lax.fori_loop(0, npair // U, body, 0)

    @pl.when(c == pl.num_programs(1) - 1)
    def _():
        def fin(p, carry):
            S = s_sc[p]
            sout_ref[0, 2 * p] = S[:rh, :rh]
            sout_ref[0, 2 * p + 1] = S[rh:, rh:]
            return carry
        lax.fori_loop(0, npair, fin, 0)


def _rwkv_scan(prep, s0, r_k, ln_w, ln_b, *, B, C, rh, precise, out_dtype):
    npair, BT, _ = prep[0].shape
    Tb = BT // B
    nc = Tb // C
    H = s0.shape[1]
    xspec = pl.BlockSpec((npair, C, LANES), lambda b, c: (0, b * nc + c, 0))
    sspec = pl.BlockSpec((1, H, rh, rh), lambda b, c: (b, 0, 0, 0))
    pspec = pl.BlockSpec((npair, 1, LANES), lambda b, c: (0, 0, 0))
    y, s_out = pl.pallas_call(
        functools.partial(_scan_kernel, C=C, rh=rh, npair=npair, precise=precise),
        grid=(B, nc),
        in_specs=[xspec] * 7 + [sspec, pspec, pspec, pspec],
        out_specs=[xspec, sspec],
        out_shape=[jax.ShapeDtypeStruct((npair, BT, LANES), out_dtype),
                   jax.ShapeDtypeStruct((B, H, rh, rh), F32)],
        scratch_shapes=[pltpu.VMEM((npair, 2 * rh, 2 * rh), F32)],
        compiler_params=_cparams("parallel", "arbitrary"),
        name="rwkv_scan",
    )(*prep, s0, r_k.reshape(npair, 1, LANES), ln_w.reshape(npair, 1, LANES), ln_b.reshape(npair, 1, LANES))
    return y, s_out


def _router_kernel(x_ref, nw_ref, rw_ref, rb_ref, h_ref, lg_ref, *, precise):
    x = x_ref[...]
    ms = jnp.mean(x * x, axis=-1, keepdims=True)
    h = x * lax.rsqrt(ms + NORM_EPS) * nw_ref[...]
    h_ref[...] = h.astype(h_ref.dtype)
    lg_ref[...] = _dg(h, rw_ref[...], _NN, precise) + rb_ref[...]


def _router(x, norm_w, rw, rb, *, precise, h_dtype, tm=512):
    M, D = x.shape
    NR = rw.shape[1]
    tm = _pick(M, tm, SUBLANES)
    return pl.pallas_call(
        functools.partial(_router_kernel, precise=precise),
        grid=(M // tm,),
        in_specs=[pl.BlockSpec((tm, D), lambda i: (i, 0)), pl.BlockSpec((1, D), lambda i: (0, 0)),
                  pl.BlockSpec((D, NR), lambda i: (0, 0)), pl.BlockSpec((1, NR), lambda i: (0, 0))],
        out_specs=[pl.BlockSpec((tm, D), lambda i: (i, 0)), pl.BlockSpec((tm, NR), lambda i: (i, 0))],
        out_shape=[jax.ShapeDtypeStruct((M, D), h_dtype), jax.ShapeDtypeStruct((M, NR), F32)],
        compiler_params=_cparams("parallel"),
        name="moe_router",
    )(x, norm_w.reshape(1, D), rw, rb)


def _moe_kernel(be_ref, nu_ref, x_ref, wg_ref, wu_ref, wd_ref, sw_ref, o_ref):
    b = pl.program_id(0)
    j = pl.program_id(1)
    last = j == pl.num_programs(1) - 1
    used = b < nu_ref[0]

    @pl.when(jnp.logical_and(used, j == 0))
    def _():
        o_ref[...] = jnp.zeros(o_ref.shape, F32)

    @pl.when(used)
    def _():
        x = x_ref[...]
        tn = wg_ref.shape[2]
        wgu = jnp.concatenate([wg_ref[0].astype(BF16), wu_ref[0].astype(BF16)], axis=1)
        hgu = jnp.dot(x, wgu, preferred_element_type=F32)
        hg, hu = hgu[:, :tn], hgu[:, tn:]
        h = hg * _sigmoid(hg) * hu
        o_ref[...] += jnp.dot(h.astype(BF16), wd_ref[0].astype(BF16), preferred_element_type=F32)

    @pl.when(jnp.logical_and(used, last))
    def _():
        o_ref[...] = o_ref[...] * sw_ref[...]

    @pl.when(jnp.logical_and(jnp.logical_not(used), last))
    def _():
        o_ref[...] = jnp.zeros(o_ref.shape, F32)


def _moe_experts(xs, blk_e, n_used, slot_w, wg, wu, wd, *, blk, tn):
    P, D = xs.shape
    DE = wg.shape[2]
    tn = _pick(DE, tn)
    nj = DE // tn
    nb = P // blk

    def eff(b, j, nu):
        live = b < nu[0]
        return jnp.where(live, b, nu[0] - 1), jnp.where(live, j, nj - 1)

    def wg_map(b, j, be, nu):
        bb, jj = eff(b, j, nu)
        return be[bb], 0, jj

    def wd_map(b, j, be, nu):
        bb, jj = eff(b, j, nu)
        return be[bb], jj, 0

    def x_map(b, j, be, nu):
        return eff(b, j, nu)[0], 0

    grid_spec = pltpu.PrefetchScalarGridSpec(
        num_scalar_prefetch=2,
        grid=(nb, nj),
        in_specs=[pl.BlockSpec((blk, D), x_map),
                  pl.BlockSpec((1, D, tn), wg_map), pl.BlockSpec((1, D, tn), wg_map),
                  pl.BlockSpec((1, tn, D), wd_map),
                  pl.BlockSpec((blk, 1), x_map)],
        out_specs=pl.BlockSpec((blk, D), lambda b, j, be, nu: (b, 0)),
    )
    return pl.pallas_call(
        _moe_kernel,
        grid_spec=grid_spec,
        out_shape=jax.ShapeDtypeStruct((P, D), F32),
        compiler_params=_cparams("arbitrary", "arbitrary"),
        name="moe_experts",
    )(blk_e, n_used, xs, wg, wu, wd, slot_w.reshape(P, 1))


def _hier_moe(x, norm_w, rg_w, rg_b, re_w, re_b, wg, wu, wd, *, blk, tn):
    T, D = x.shape
    NG = rg_w.shape[1]
    E = re_w.shape[1]
    EPG = E // NG
    NR = -(-(NG + E) // LANES) * LANES
    rw = jnp.pad(jnp.concatenate([rg_w, re_w], axis=1), ((0, 0), (0, NR - NG - E)))
    rb = jnp.pad(jnp.concatenate([rg_b, re_b]), (0, NR - NG - E)).reshape(1, NR)
    h, logits = _router(x, norm_w, rw.astype(BF16), rb, precise=False, h_dtype=BF16)
    gprob = jax.nn.softmax(logits[:, :NG], axis=-1)
    gsel = jnp.argmax(gprob, axis=-1)
    gp = jnp.take_along_axis(gprob, gsel[:, None], axis=-1)[:, 0]
    elog = logits[:, NG:NG + E].reshape(T, NG, EPG)
    elog = jnp.take_along_axis(elog, gsel[:, None, None], axis=1)[:, 0]
    top_p, top_i = lax.top_k(jax.nn.softmax(elog, axis=-1), TOP_K)
    top_p = top_p / jnp.sum(top_p, axis=-1, keepdims=True)
    weights = gp[:, None] * top_p
    expert_idx = gsel[:, None].astype(jnp.int32) * EPG + top_i.astype(jnp.int32)

    TK = T * TOP_K
    nb = -(-TK // blk) + E
    P = nb * blk
    flat_e = expert_idx.reshape(-1)
    flat_w = weights.reshape(-1)
    flat_t = jnp.repeat(jnp.arange(T, dtype=jnp.int32), TOP_K)
    order = jnp.argsort(flat_e)
    se = flat_e[order]
    counts = jnp.bincount(flat_e, length=E)
    pcounts = (counts + blk - 1) // blk * blk
    pend = jnp.cumsum(pcounts)
    pstart = pend - pcounts
    start = jnp.cumsum(counts) - counts
    dest = (pstart[se] + jnp.arange(TK) - start[se]).astype(jnp.int32)
    slot_t = jnp.full((P,), T, jnp.int32).at[dest].set(flat_t[order])
    slot_w = jnp.zeros((P,), F32).at[dest].set(flat_w[order])
    blk_e = jnp.minimum(jnp.searchsorted(pend, jnp.arange(nb) * blk, side='right'), E - 1).astype(jnp.int32)
    n_used = (pend[-1] // blk).astype(jnp.int32).reshape(1)
    pos = jnp.zeros((TK,), jnp.int32).at[order].set(dest)

    xs = jnp.concatenate([h, jnp.zeros((1, D), h.dtype)], axis=0)[slot_t]
    ys = _moe_experts(xs, blk_e, n_used, slot_w, wg, wu, wd, blk=blk, tn=tn)
    pos = pos.reshape(T, TOP_K)
    return x + ys[pos[:, 0]] + ys[pos[:, 1]]


def _pairs_to_rows(y):
    npair, T, _ = y.shape
    return jnp.transpose(y, (1, 0, 2)).reshape(T, npair * LANES)


def _layer(x, positions, prev_rows, s0, B, cfg, w, *, exact, attn_fn, q_scale, emit_bf16, moe_blk, moe_tn):
    T, D = x.shape
    hd, HK, G, vd, rh = cfg['hd'], cfg['HK'], cfg['G'], cfg['vd'], cfg['rh']
    qcols, kcols, vcols = HK * G * 2 * hd, HK * 2 * hd, HK * vd
    W = cfg['W']
    mm = functools.partial(_matmul, precise=False)

    h = _rmsnorm(x, w['norm1_w'], BF16)
    qkv = mm(h, w['w_qkv'], qcols + kcols + vcols)
    rkv = mm(h, w['w_rkv'], 3 * W)
    lora = mm(h, w['w_lora'], cfg['lp'], tn=cfg['lp'])
    gates = mm(h, w['w_gates'], 2 * D)

    cos_full, sin_signed = _rope_tables(positions, hd)
    outs = _qk_rope(qkv, cos_full, sin_signed, w['q_norm_w'], w['k_norm_w'], qcols, kcols, vcols,
                    q_scale=q_scale, q_dtype=BF16, emit_bf16=emit_bf16)
    q, k_new, v_new = outs[:3]
    o_attn = attn_fn(q, outs)

    prep = _rwkv_prep(rkv, lora, prev_rows, w['mu_rkv'], w['mu_lora'], w['rwkv_w0'], w['rwkv_w2'],
                      w['rwkv_a0'], w['rwkv_a2'], w['g2p'], w['rwkv_k_k'], w['rwkv_k_a'],
                      rh=rh, exact=exact)
    Tb = T // B
    if Tb % 64 == 0:
        C = 64
    else:
        C = SUBLANES
        pad = -Tb % C
        prep = [jnp.pad(a.reshape(-1, B, Tb, LANES), ((0, 0), (0, 0), (0, pad), (0, 0)))
                .reshape(a.shape[0], B * (Tb + pad), LANES) for a in prep]
    y_r, s_fin = _rwkv_scan(prep, s0, w['rwkv_r_k'], w['rwkv_ln_w'], w['rwkv_ln_b'],
                            B=B, C=C, rh=rh, precise=exact, out_dtype=BF16)
    if Tb % 64:
        y_r = y_r.reshape(y_r.shape[0], B, -1, LANES)[:, :, :Tb].reshape(y_r.shape[0], T, LANES)
    o_rwkv = _pairs_to_rows(y_r)

    mixed = _merge(o_attn, w['w_branch_attn'], o_rwkv, w['w_branch_rwkv'], gates, w['gate_b'],
                   precise=False, out_dtype=BF16)
    x1 = _matmul_residual(mixed, w['w_out'], x, precise=False)
    y = _hier_moe(x1, w['norm2_w'], w['router_group_w'], w['router_group_b'], w['router_expert_w'],
                  w['router_expert_b'], w['expert_w_gate'], w['expert_w_up'], w['expert_w_down'],
                  blk=moe_blk, tn=moe_tn)
    return y, k_new, v_new, s_fin, rkv, lora


def kernel(x_prompt, x_sample, cache_k, cache_v, state_rwkv, state_shift, page_table, norm1_w, w_in, gate_b, q_norm_w, k_norm_w, lambda_q1, lambda_k1, lambda_q2, lambda_k2, subln_w, rwkv_mu, rwkv_w0, rwkv_w2, rwkv_a0, rwkv_a2, rwkv_g2, rwkv_k_k, rwkv_k_a, rwkv_r_k, rwkv_ln_w, rwkv_ln_b, w_branch_attn, w_branch_rwkv, w_out, norm2_w, router_group_w, router_group_b, router_expert_w, router_expert_b, expert_w_gate, expert_w_up, expert_w_down):
    B, T, D = x_prompt.shape
    DB, S, _ = x_sample.shape
    L = norm1_w.shape[0]
    assert L == 1 and B == 1 and S == 1, "one trunk layer, one prompt sequence, one new token per sample"
    hd = q_norm_w.shape[1]
    HK = cache_k.shape[3]
    vd = cache_v.shape[4]
    G = w_branch_attn.shape[1] // vd // HK
    H, rh = rwkv_r_k.shape[1], rwkv_r_k.shape[2]
    W = H * rh
    dl, al, gl = rwkv_w2.shape[1], rwkv_a2.shape[1], rwkv_g2.shape[1]
    lcols = dl + al + gl
    lp = -(-lcols // LANES) * LANES
    qkv_cols = HK * G * 2 * hd + HK * 2 * hd + HK * vd
    r_off = qkv_cols
    g_off = r_off + 3 * W + lcols
    n_pages, page = page_table.shape[1], cache_k.shape[2]
    past = n_pages * page
    cfg = dict(hd=hd, HK=HK, G=G, vd=vd, rh=rh, W=W, lp=lp, r_off=r_off,
               moe_blk_p=min(512, max(SUBLANES * 2, T // 16)), moe_tn_p=256,
               moe_blk_s=2 * SUBLANES, moe_tn_s=256)
    l = 0
    (cache_k, cache_v, state_rwkv, state_shift, norm1_w, w_in, gate_b, q_norm_w, k_norm_w, lambda_q1,
     lambda_k1, lambda_q2, lambda_k2, subln_w, rwkv_mu, rwkv_w0, rwkv_w2, rwkv_a0, rwkv_a2, rwkv_g2,
     rwkv_k_k, rwkv_k_a, rwkv_r_k, rwkv_ln_w, rwkv_ln_b, w_branch_attn, w_branch_rwkv, w_out, norm2_w,
     router_group_w, router_group_b, router_expert_w, router_expert_b, expert_w_gate, expert_w_up,
     expert_w_down) = [a.reshape(a.shape[1:]) for a in (
         cache_k, cache_v, state_rwkv, state_shift, norm1_w, w_in, gate_b, q_norm_w, k_norm_w, lambda_q1,
         lambda_k1, lambda_q2, lambda_k2, subln_w, rwkv_mu, rwkv_w0, rwkv_w2, rwkv_a0, rwkv_a2, rwkv_g2,
         rwkv_k_k, rwkv_k_a, rwkv_r_k, rwkv_ln_w, rwkv_ln_b, w_branch_attn, w_branch_rwkv, w_out, norm2_w,
         router_group_w, router_group_b, router_expert_w, router_expert_b, expert_w_gate, expert_w_up,
         expert_w_down)]
    lam_init = 0.8 - 0.6 * math.exp(-0.3 * l)
    lams = (lambda_q1, lambda_k1, lambda_q2, lambda_k2)

    mu = rwkv_mu
    g2p = jnp.pad(rwkv_g2, ((0, lp - lcols), (0, 0)))
    w_lora = jnp.pad(w_in[:, r_off + 3 * W:r_off + 3 * W + lcols], ((0, 0), (0, lp - lcols)))
    w = dict(
        norm1_w=norm1_w, q_norm_w=q_norm_w, k_norm_w=k_norm_w, gate_b=gate_b,
        mu_rkv=mu[:3 * W].reshape(1, 3 * W), mu_lora=jnp.pad(mu[3 * W:], (0, lp - lcols)).reshape(1, lp),
        rwkv_w0=rwkv_w0, rwkv_a0=rwkv_a0, rwkv_k_k=rwkv_k_k, rwkv_k_a=rwkv_k_a,
        rwkv_r_k=rwkv_r_k, rwkv_ln_w=rwkv_ln_w, rwkv_ln_b=rwkv_ln_b,
        norm2_w=norm2_w, router_group_w=router_group_w, router_group_b=router_group_b,
        router_expert_w=router_expert_w, router_expert_b=router_expert_b,
        expert_w_gate=expert_w_gate, expert_w_up=expert_w_up, expert_w_down=expert_w_down,
        w_qkv=w_in[:, :qkv_cols].astype(BF16), w_rkv=w_in[:, r_off:r_off + 3 * W].astype(BF16),
        w_lora=w_lora.astype(BF16), w_gates=w_in[:, g_off:g_off + 2 * D].astype(BF16),
        rwkv_w2=rwkv_w2.astype(BF16), rwkv_a2=rwkv_a2.astype(BF16), g2p=g2p.astype(BF16),
        w_branch_attn=w_branch_attn.astype(BF16), w_branch_rwkv=w_branch_rwkv.astype(BF16),
        w_out=w_out.astype(BF16))

    def prompt_attn(q, outs):
        return _flash_diff_attn(q, outs[3], outs[4], lams, subln_w, HK=HK, G=G, hd=hd, vd=vd,
                                lam_init=lam_init, out_dtype=BF16)

    zero_prev = (jnp.zeros((1, 3 * W), F32), jnp.zeros((1, lp), F32))
    yp, kp, vp, sp, rkv_p, lora_p = _layer(
        x_prompt.reshape(T, D), jnp.arange(T), zero_prev, jnp.zeros((1, H, rh, rh), F32), 1, cfg, w,
        exact=False, attn_fn=prompt_attn, q_scale=hd ** -0.5 * math.log2(math.e), emit_bf16=True,
        moe_blk=cfg['moe_blk_p'], moe_tn=cfg['moe_tn_p'])
    shift_p = jnp.concatenate([rkv_p[T - 1], lora_p[T - 1, :lcols]])

    def sample_attn(q, outs):
        return _decode_diff_attn(q, outs[1], outs[2], cache_k, cache_v, page_table, lams, subln_w,
                                 HK=HK, G=G, hd=hd, vd=vd, lam_init=lam_init)

    sh = state_shift
    prev_s = (sh[:, :3 * W], jnp.pad(sh[:, 3 * W:], ((0, 0), (0, lp - lcols))))
    ys, ks, vs, ss, rkv_s, lora_s = _layer(
        x_sample.reshape(DB, D), jnp.full((DB,), past, jnp.int32), prev_s, state_rwkv, DB, cfg, w,
        exact=True, attn_fn=sample_attn, q_scale=1.0, emit_bf16=False,
        moe_blk=cfg['moe_blk_s'], moe_tn=cfg['moe_tn_s'])
    shift_s = jnp.concatenate([rkv_s, lora_s[:, :lcols]], axis=1)

    return (yp.reshape(1, T, D), ys.reshape(DB, 1, D),
            kp.reshape(1, 1, T, HK, 2, hd), vp.reshape(1, 1, T, HK, vd),
            sp.reshape(1, 1, H, rh, rh), shift_p.reshape(1, 1, -1),
            ks.reshape(1, DB, 1, HK, 2, hd), vs.reshape(1, DB, 1, HK, vd),
            ss.reshape(1, DB, H, rh, rh), shift_s.reshape(1, DB, -1))
```

```python
import functools
import math

import jax
import jax.numpy as jnp
from jax import lax
from jax.experimental import pallas as pl
from jax.experimental.pallas import tpu as pltpu

F32 = jnp.float32
BF16 = jnp.bfloat16
HIGHEST = lax.Precision.HIGHEST

LANES = 128
SUBLANES = 8
VMEM_LIMIT = 56 * 1024 * 1024

ROPE_THETA = 10000.0
NORM_EPS = 1e-6
GN_EPS = 64e-5
TOP_K = 2
SLABS_PER_GROUP = 16
DECODE_PAGES_PER_STEP = 8

_NN = (((1,), (0,)), ((), ()))
_NT = (((1,), (1,)), ((), ()))
_TN = (((0,), (0,)), ((), ()))


def _cparams(*sem):
    return pltpu.CompilerParams(dimension_semantics=sem, vmem_limit_bytes=VMEM_LIMIT)


def _pick(n, pref, unit=LANES):
    if n <= pref:
        return n
    t = pref // unit * unit
    while t > unit and n % t:
        t -= unit
    assert n % t == 0, (n, pref)
    return t


def _dg(a, b, dn, precise):
    if precise:
        return lax.dot_general(a.astype(F32), b.astype(F32), dn, precision=HIGHEST,
                               preferred_element_type=F32)
    return lax.dot_general(a.astype(BF16), b.astype(BF16), dn, preferred_element_type=F32)


def _sigmoid(x):
    return 1.0 / (1.0 + jnp.exp(-x))


def _softplus(x):
    return jnp.maximum(x, 0.0) + jnp.log(1.0 + jnp.exp(-jnp.abs(x)))


def _rms_kernel(x_ref, w_ref, o_ref, *, eps):
    x = x_ref[...]
    ms = jnp.mean(x * x, axis=-1, keepdims=True)
    o_ref[...] = (x * lax.rsqrt(ms + eps) * w_ref[...]).astype(o_ref.dtype)


def _rmsnorm(x, w, out_dtype, tm=512):
    M, D = x.shape
    tm = _pick(M, tm, SUBLANES)
    return pl.pallas_call(
        functools.partial(_rms_kernel, eps=NORM_EPS),
        grid=(M // tm,),
        in_specs=[pl.BlockSpec((tm, D), lambda i: (i, 0)),
                  pl.BlockSpec((1, D), lambda i: (0, 0))],
        out_specs=pl.BlockSpec((tm, D), lambda i: (i, 0)),
        out_shape=jax.ShapeDtypeStruct((M, D), out_dtype),
        compiler_params=_cparams("parallel"),
        name="rmsnorm",
    )(x, w.reshape(1, D))


def _mm_kernel(a_ref, b_ref, o_ref, *, precise):
    o_ref[...] = _dg(a_ref[...], b_ref[...], _NN, precise).astype(o_ref.dtype)


def _matmul(a, b, n, col0=0, *, precise, out_dtype=F32, tm=1024, tn=512):
    M, K = a.shape
    tm = _pick(M, tm, SUBLANES)
    tn = _pick(n, tn)
    if col0 % tn:
        b, col0 = b[:, col0:col0 + n], 0
    c0 = col0 // tn
    return pl.pallas_call(
        functools.partial(_mm_kernel, precise=precise),
        grid=(M // tm, n // tn),
        in_specs=[pl.BlockSpec((tm, K), lambda i, j: (i, 0)),
                  pl.BlockSpec((K, tn), lambda i, j: (0, c0 + j))],
        out_specs=pl.BlockSpec((tm, tn), lambda i, j: (i, j)),
        out_shape=jax.ShapeDtypeStruct((M, n), out_dtype),
        compiler_params=_cparams("parallel", "arbitrary"),
        name="matmul",
    )(a, b)


def _merge_kernel(a1_ref, b1_ref, a2_ref, b2_ref, g1_ref, g2_ref, gb1_ref, gb2_ref, o_ref, *, precise):
    o1 = _dg(a1_ref[...], b1_ref[...], _NN, precise)
    o2 = _dg(a2_ref[...], b2_ref[...], _NN, precise)
    s1 = _sigmoid(g1_ref[...] + gb1_ref[...])
    s2 = _sigmoid(g2_ref[...] + gb2_ref[...])
    o_ref[...] = (s1 * o1 + s2 * o2).astype(o_ref.dtype)


def _merge(a1, b1, a2, b2, gates, gate_b, *, precise, out_dtype, tm=512, tn=512):
    M, K = a1.shape
    K2 = a2.shape[1]
    D = b1.shape[1]
    tm = _pick(M, tm, SUBLANES)
    tn = _pick(D, tn)
    nj = D // tn
    gb = gate_b.reshape(1, 2 * D)
    return pl.pallas_call(
        functools.partial(_merge_kernel, precise=precise),
        grid=(M // tm, nj),
        in_specs=[pl.BlockSpec((tm, K), lambda i, j: (i, 0)),
                  pl.BlockSpec((K, tn), lambda i, j: (0, j)),
                  pl.BlockSpec((tm, K2), lambda i, j: (i, 0)),
                  pl.BlockSpec((K2, tn), lambda i, j: (0, j)),
                  pl.BlockSpec((tm, tn), lambda i, j: (i, j)),
                  pl.BlockSpec((tm, tn), lambda i, j: (i, nj + j)),
                  pl.BlockSpec((1, tn), lambda i, j: (0, j)),
                  pl.BlockSpec((1, tn), lambda i, j: (0, nj + j))],
        out_specs=pl.BlockSpec((tm, tn), lambda i, j: (i, j)),
        out_shape=jax.ShapeDtypeStruct((M, D), out_dtype),
        compiler_params=_cparams("parallel", "arbitrary"),
        name="merge",
    )(a1, b1, a2, b2, gates, gates, gb, gb)


def _mm_res_kernel(a_ref, b_ref, r_ref, o_ref, *, precise):
    o_ref[...] = r_ref[...] + _dg(a_ref[...], b_ref[...], _NN, precise)


def _matmul_residual(a, b, res, *, precise, tm=1024, tn=512):
    M, K = a.shape
    N = b.shape[1]
    tm = _pick(M, tm, SUBLANES)
    tn = _pick(N, tn)
    return pl.pallas_call(
        functools.partial(_mm_res_kernel, precise=precise),
        grid=(M // tm, N // tn),
        in_specs=[pl.BlockSpec((tm, K), lambda i, j: (i, 0)),
                  pl.BlockSpec((K, tn), lambda i, j: (0, j)),
                  pl.BlockSpec((tm, tn), lambda i, j: (i, j))],
        out_specs=pl.BlockSpec((tm, tn), lambda i, j: (i, j)),
        out_shape=jax.ShapeDtypeStruct((M, N), F32),
        compiler_params=_cparams("parallel", "arbitrary"),
        name="matmul_residual",
    )(a, b, res)


def _qkrope_kernel(qkv_ref, cos_ref, sin_ref, qw_ref, kw_ref, *out_refs, nq, nk, hd, vcols, scale, emit_bf16):
    if emit_bf16:
        q_ref, k_ref, v_ref, kb_ref, vb_ref = out_refs
    else:
        q_ref, k_ref, v_ref = out_refs
    cos = cos_ref[...]
    sin = sin_ref[...]

    def norm_rope(x, w):
        ms = jnp.mean(x * x, axis=-1, keepdims=True)
        xn = x * lax.rsqrt(ms + NORM_EPS) * w
        return xn * cos + pltpu.roll(xn, hd // 2, axis=1) * sin

    qw = qw_ref[...]
    kw = kw_ref[...]
    for c in range(nq):
        o = norm_rope(qkv_ref[:, c * hd:(c + 1) * hd], qw) * scale
        q_ref[:, c * hd:(c + 1) * hd] = o.astype(q_ref.dtype)
    for c in range(nk):
        o = norm_rope(qkv_ref[:, (nq + c) * hd:(nq + c + 1) * hd], kw)
        k_ref[:, c * hd:(c + 1) * hd] = o
        if emit_bf16:
            kb_ref[:, c * hd:(c + 1) * hd] = o.astype(BF16)
    v = qkv_ref[:, (nq + nk) * hd:(nq + nk) * hd + vcols]
    v_ref[...] = v
    if emit_bf16:
        vb_ref[...] = v.astype(BF16)


def _qk_rope(qkv, cos_full, sin_signed, q_norm_w, k_norm_w, qcols, kcols, vcols, *, q_scale, q_dtype,
             emit_bf16, tq=256):
    T = qkv.shape[0]
    hd = q_norm_w.shape[0]
    tq = _pick(T, tq, SUBLANES)
    nq, nk = qcols // hd, kcols // hd
    row = lambda i: (i, 0)
    out_shape = [jax.ShapeDtypeStruct((T, qcols), q_dtype),
                 jax.ShapeDtypeStruct((T, kcols), F32),
                 jax.ShapeDtypeStruct((T, vcols), F32)]
    out_specs = [pl.BlockSpec((tq, qcols), row), pl.BlockSpec((tq, kcols), row), pl.BlockSpec((tq, vcols), row)]
    if emit_bf16:
        out_shape += [jax.ShapeDtypeStruct((T, kcols), BF16), jax.ShapeDtypeStruct((T, vcols), BF16)]
        out_specs += [pl.BlockSpec((tq, kcols), row), pl.BlockSpec((tq, vcols), row)]
    return pl.pallas_call(
        functools.partial(_qkrope_kernel, nq=nq, nk=nk, hd=hd, vcols=vcols, scale=q_scale,
                          emit_bf16=emit_bf16),
        grid=(T // tq,),
        in_specs=[pl.BlockSpec((tq, qkv.shape[1]), row),
                  pl.BlockSpec((tq, hd), row), pl.BlockSpec((tq, hd), row),
                  pl.BlockSpec((1, hd), lambda i: (0, 0)), pl.BlockSpec((1, hd), lambda i: (0, 0))],
        out_specs=out_specs,
        out_shape=out_shape,
        compiler_params=_cparams("parallel"),
        name="qk_norm_rope",
    )(qkv, cos_full, sin_signed, q_norm_w.reshape(1, hd), k_norm_w.reshape(1, hd))


def _rope_tables(positions, hd):
    inv = 1.0 / (ROPE_THETA ** (jnp.arange(0, hd, 2, dtype=F32) / hd))
    ang = positions.astype(F32)[:, None] * inv[None, :]
    cos, sin = jnp.cos(ang), jnp.sin(ang)
    return jnp.concatenate([cos, cos], axis=-1), jnp.concatenate([-sin, sin], axis=-1)


def _diff_lambda_in_kernel(lq1_ref, lk1_ref, lq2_ref, lk2_ref, lam_init):
    e1 = jnp.exp(jnp.sum(lq1_ref[...] * lk1_ref[...], axis=-1, keepdims=True))
    e2 = jnp.exp(jnp.sum(lq2_ref[...] * lk2_ref[...], axis=-1, keepdims=True))
    return e1 - e2 + lam_init


def _attn_finalize(acc1, l1, acc2, l2, lam, subw, lam_init):
    o = acc1 / l1 - lam * (acc2 / l2)
    ms = jnp.mean(o * o, axis=-1, keepdims=True)
    return o * lax.rsqrt(ms + NORM_EPS) * subw * (1.0 - lam_init)


def _flash_kernel(q_ref, k_ref, v_ref, lq1_ref, lk1_ref, lq2_ref, lk2_ref, subw_ref, o_ref,
                  m_sc, l_sc, acc_sc, s_sc, *, G, hd, vd, tq, lam_init):
    qi = pl.program_id(1)
    ki = pl.program_id(2)

    @pl.when(ki == 0)
    def _():
        m_sc[...] = jnp.full(m_sc.shape, -jnp.inf, F32)
        l_sc[...] = jnp.zeros(l_sc.shape, F32)
        acc_sc[...] = jnp.zeros(acc_sc.shape, F32)

    def tile(diagonal):
        v = v_ref[...]
        for m in range(2):
            kb = k_ref[:, m * hd:(m + 1) * hd]
            for g in range(G):
                idx = g * 2 + m
                q = q_ref[:, idx * hd:(idx + 1) * hd]
                s = lax.dot_general(q, kb, _NT, preferred_element_type=F32)
                if diagonal:
                    rows = lax.broadcasted_iota(jnp.int32, (tq, tq), 0)
                    cols = lax.broadcasted_iota(jnp.int32, (tq, tq), 1)
                    s = jnp.where(cols <= rows, s, -jnp.inf)
                s_sc[...] = s
                m_prev = m_sc[idx]
                m_new = jnp.maximum(m_prev, jnp.max(s_sc[...], axis=-1, keepdims=True))
                m_sc[idx] = m_new
                alpha = jnp.exp2(m_prev - m_new)
                p = jnp.exp2(s_sc[...] - jnp.tile(m_new, (1, tq // LANES)))
                l_sc[idx] = alpha * l_sc[idx] + jnp.sum(p, axis=-1, keepdims=True)
                pv = jnp.dot(p.astype(BF16), v, preferred_element_type=F32)
                acc_sc[idx] = jnp.tile(alpha, (1, vd // LANES)) * acc_sc[idx] + pv

    @pl.when(ki < qi)
    def _():
        tile(False)

    @pl.when(ki == qi)
    def _():
        tile(True)
        lam = _diff_lambda_in_kernel(lq1_ref, lk1_ref, lq2_ref, lk2_ref, lam_init)
        subw = subw_ref[...]
        for g in range(G):
            o = _attn_finalize(acc_sc[2 * g], l_sc[2 * g][:, :1], acc_sc[2 * g + 1], l_sc[2 * g + 1][:, :1],
                               lam, subw, lam_init)
            o_ref[:, g * vd:(g + 1) * vd] = o.astype(o_ref.dtype)


def _flash_diff_attn(q, kb, vb, lams, subln_w, *, HK, G, hd, vd, lam_init, out_dtype, tq=512):
    T = q.shape[0]
    tq = _pick(T, tq)
    nq = T // tq
    vec = lambda n: pl.BlockSpec((1, n), lambda h, i, j: (0, 0))
    return pl.pallas_call(
        functools.partial(_flash_kernel, G=G, hd=hd, vd=vd, tq=tq, lam_init=lam_init),
        grid=(HK, nq, nq),
        in_specs=[pl.BlockSpec((tq, G * 2 * hd), lambda h, i, j: (i, h)),
                  pl.BlockSpec((tq, 2 * hd), lambda h, i, j: (jnp.minimum(i, j), h)),
                  pl.BlockSpec((tq, vd), lambda h, i, j: (jnp.minimum(i, j), h)),
                  vec(hd), vec(hd), vec(hd), vec(hd), vec(vd)],
        out_specs=pl.BlockSpec((tq, G * vd), lambda h, i, j: (i, h)),
        out_shape=jax.ShapeDtypeStruct((T, HK * G * vd), out_dtype),
        scratch_shapes=[pltpu.VMEM((2 * G, tq, LANES), F32), pltpu.VMEM((2 * G, tq, LANES), F32),
                        pltpu.VMEM((2 * G, tq, vd), F32), pltpu.VMEM((tq, tq), F32)],
        compiler_params=_cparams("parallel", "parallel", "arbitrary"),
        name="flash_diff_attn",
    )(q, kb, vb, *[x.reshape(1, -1) for x in lams], subln_w.reshape(1, vd))


def _decode_kernel(pt_ref, q_ref, kn_ref, vn_ref, *refs, HK, G, vd, pps, scale, lam_init):
    kc_refs, vc_refs = refs[:pps], refs[pps:2 * pps]
    (lq1_ref, lk1_ref, lq2_ref, lk2_ref, subw_ref, o_ref, s_sc, a_sc, anew_sc, acc_sc) = refs[2 * pps:]
    ph = pl.program_id(1)
    pg = pl.program_id(2)
    last = pl.num_programs(2) - 1
    NI = q_ref.shape[1]
    half = NI // 2
    q = q_ref[0]
    nk = q.shape[1] // LANES
    page = kc_refs[0].shape[0] // nk
    nv = vc_refs[0].shape[0] // page
    v_order = [et * HK + hk for hk in range(HK) for et in range(nv // HK)]

    def token_rows(ref, order):
        n = len(order)
        return jnp.concatenate([ref[pl.ds(s, page, stride=n), :] for s in order], axis=1).astype(BF16)

    row_hk = lax.broadcasted_iota(jnp.int32, (half, vd), 0) // G

    def own_head(x):
        out = x[:, :vd]
        for hk in range(1, HK):
            out = jnp.where(row_hk == hk, x[:, hk * vd:(hk + 1) * vd], out)
        return out

    @pl.when(ph == 0)
    def _():
        s_sc[pg] = scale * jnp.concatenate(
            [lax.dot_general(q, token_rows(kc, range(nk)), _NT, preferred_element_type=F32) for kc in kc_refs],
            axis=1)

    @pl.when(jnp.logical_and(ph == 0, pg == last))
    def _():
        lam = _diff_lambda_in_kernel(lq1_ref, lk1_ref, lq2_ref, lk2_ref, lam_init)
        s_new = scale * jnp.sum(q.astype(F32) * kn_ref[0].astype(BF16).astype(F32), axis=-1, keepdims=True)
        s = s_sc[...]
        m = jnp.maximum(jnp.max(jnp.max(s, axis=0), axis=-1, keepdims=True), s_new)
        e = jnp.exp(s - m[None])
        e_new = jnp.exp(s_new - m)
        denom = jnp.sum(jnp.sum(e, axis=0), axis=-1, keepdims=True) + e_new
        p = e / denom[None]
        p_new = e_new / denom
        a_sc[...] = (p[:, :half] - lam * p[:, half:]).astype(BF16)
        anew_sc[...] = jnp.broadcast_to(p_new[:half] - lam * p_new[half:], (half, LANES))
        acc_sc[...] = jnp.zeros(acc_sc.shape, F32)

    @pl.when(ph == 1)
    def _():
        a = a_sc[pg]
        pv = sum(jnp.dot(a[:, j * page:(j + 1) * page], token_rows(vc, v_order), preferred_element_type=F32)
                 for j, vc in enumerate(vc_refs))
        acc_sc[...] += own_head(pv)

    @pl.when(jnp.logical_and(ph == 1, pg == last))
    def _():
        a_new = anew_sc[...][:, :1].astype(BF16).astype(F32)
        v_new = own_head(jnp.broadcast_to(vn_ref[0].astype(BF16).astype(F32), (half, HK * vd)))
        o = acc_sc[...] + a_new * v_new
        ms = jnp.mean(o * o, axis=-1, keepdims=True)
        o_ref[0] = o * lax.rsqrt(ms + NORM_EPS) * subw_ref[...] * (1.0 - lam_init)


def _decode_diff_attn(q, k_new, v_new, cache_k, cache_v, page_table, lams, subln_w, *, HK, G, hd, vd, lam_init):
    DB, n_pages = page_table.shape
    n_phys, page = cache_k.shape[0], cache_k.shape[1]
    assert page % LANES == 0 and vd % LANES == 0 and hd == LANES
    kcw, vcw = HK * 2 * hd, HK * vd
    nk, nv = kcw // LANES, vcw // LANES
    kc = cache_k.reshape(n_phys * page * nk, LANES)
    vc = jnp.transpose(cache_v.reshape(n_phys, page, HK, vd // LANES, LANES),
                       (0, 1, 3, 2, 4)).reshape(n_phys * page * nv, LANES)
    NI = 2 * HK * G
    qt = jnp.transpose(q.reshape(DB, HK, G, 2, hd), (0, 3, 1, 2, 4))
    own = (jnp.eye(HK, dtype=q.dtype)[None, None, :, None, :, None, None]
           * jnp.eye(2, dtype=q.dtype)[None, :, None, None, None, :, None])
    qbd = (qt[:, :, :, :, None, None, :] * own).reshape(DB, NI, kcw)
    pps = math.gcd(n_pages, DECODE_PAGES_PER_STEP)
    steps = n_pages // pps
    per_b = lambda r, n: pl.BlockSpec((1, r, n), lambda b, ph, pg, pt: (b, 0, 0))
    vec = lambda n: pl.BlockSpec((1, n), lambda b, ph, pg, pt: (0, 0))

    def k_spec(j):
        return pl.BlockSpec((page * nk, LANES),
                            lambda b, ph, pg, pt: (pt[b, jnp.where(ph == 0, pg, steps - 1) * pps + j], 0))

    def v_spec(j):
        return pl.BlockSpec((page * nv, LANES),
                            lambda b, ph, pg, pt: (pt[b, jnp.where(ph == 1, pg, 0) * pps + j], 0))

    grid_spec = pltpu.PrefetchScalarGridSpec(
        num_scalar_prefetch=1,
        grid=(DB, 2, steps),
        in_specs=[per_b(NI, kcw), per_b(1, kcw), per_b(1, vcw)]
        + [k_spec(j) for j in range(pps)] + [v_spec(j) for j in range(pps)]
        + [vec(hd), vec(hd), vec(hd), vec(hd), vec(vd)],
        out_specs=pl.BlockSpec((1, NI // 2, vd), lambda b, ph, pg, pt: (b, 0, 0)),
        scratch_shapes=[pltpu.VMEM((steps, NI, pps * page), F32), pltpu.VMEM((steps, NI // 2, pps * page), BF16),
                        pltpu.VMEM((NI // 2, LANES), F32), pltpu.VMEM((NI // 2, vd), F32)],
    )
    out = pl.pallas_call(
        functools.partial(_decode_kernel, HK=HK, G=G, vd=vd, pps=pps, scale=hd ** -0.5, lam_init=lam_init),
        grid_spec=grid_spec,
        out_shape=jax.ShapeDtypeStruct((DB, NI // 2, vd), F32),
        compiler_params=_cparams("parallel", "arbitrary", "arbitrary"),
        name="decode_diff_attn",
    )(page_table, qbd, k_new.reshape(DB, 1, kcw), v_new.reshape(DB, 1, vcw), *([kc] * pps), *([vc] * pps),
      *[x.reshape(1, -1) for x in lams], subln_w.reshape(1, vd))
    return out.reshape(DB, HK * G * vd)


def _segsum_matrix(n, seg):
    r = lax.broadcasted_iota(jnp.int32, (n, n), 0) // seg
    c = lax.broadcasted_iota(jnp.int32, (n, n), 1) // seg
    return (r == c).astype(F32)


def _prep_kernel(*refs, halo, rh, exact, tc):
    if halo:
        (r_ref, k_ref, v_ref, l_ref, rp_ref, kp_ref, vp_ref, lp_ref, ri_ref, ki_ref, vi_ref, li_ref,
         mur_ref, muk_ref, muv_ref, mul_ref, w0_ref, w2_ref, a0_ref, a2_ref, g2_ref, kkw_ref, kaw_ref,
         ro_ref, lwo_ref, ko_ref, vo_ref, kko_ref, bo_ref, go_ref) = refs
    else:
        (r_ref, k_ref, v_ref, l_ref, rp_ref, kp_ref, vp_ref, lp_ref,
         mur_ref, muk_ref, muv_ref, mul_ref, w0_ref, w2_ref, a0_ref, a2_ref, g2_ref, kkw_ref, kaw_ref,
         ro_ref, lwo_ref, ko_ref, vo_ref, kko_ref, bo_ref, go_ref) = refs
    i = pl.program_id(0)

    def shifted(x_ref, prev_ref, init_ref):
        x = x_ref[...]
        if not halo:
            return x, prev_ref[...]
        last = prev_ref[SUBLANES - 1:SUBLANES, :]
        first = jnp.where(i == 0, init_ref[...], last)
        xs = pltpu.roll(x, 1, axis=0)
        rows = lax.broadcasted_iota(jnp.int32, x.shape, 0)
        return x, jnp.where(rows == 0, first, xs)

    def mix(x_ref, prev_ref, init_ref, mu_ref):
        x, xp = shifted(x_ref, prev_ref, init_ref)
        return x + (xp - x) * mu_ref[...]

    r = mix(r_ref, rp_ref, ri_ref if halo else None, mur_ref)
    k = mix(k_ref, kp_ref, ki_ref if halo else None, muk_ref)
    v = mix(v_ref, vp_ref, vi_ref if halo else None, muv_ref)
    lo = mix(l_ref, lp_ref, li_ref if halo else None, mul_ref)
    dl = w2_ref.shape[0]
    al = a2_ref.shape[0]
    xw = lo[:, :dl]
    xa = lo[:, dl:dl + al]
    xg = lo[:, dl + al:]
    wl = w0_ref[...] + _dg(jnp.tanh(xw), w2_ref[...], _NN, False)
    lw = -jnp.exp(-_softplus(-wl) - 0.5)
    a = _sigmoid(a0_ref[...] + _dg(xa, a2_ref[...], _NN, False))
    g = _dg(_sigmoid(xg), g2_ref[...], _NN, False)
    seg = _segsum_matrix(tc, rh)
    kk = k * kkw_ref[...]
    ss = _dg(kk * kk, seg, _NN, exact)
    kk = kk / jnp.maximum(jnp.sqrt(ss), 1e-12)
    kf = k * (1.0 + (a - 1.0) * kaw_ref[...])
    b = kk * a
    for pi in range(tc // LANES):
        sl = slice(pi * LANES, (pi + 1) * LANES)
        ro_ref[pi] = r[:, sl]
        lwo_ref[pi] = lw[:, sl]
        ko_ref[pi] = kf[:, sl]
        vo_ref[pi] = v[:, sl]
        kko_ref[pi] = kk[:, sl]
        bo_ref[pi] = b[:, sl]
        go_ref[pi] = g[:, sl]


def _rwkv_prep(rkv, lora, prev, mu_rkv, mu_lora, w0, w2, a0, a2, g2p, k_k, k_a, *, rh, exact, tt=256, tc=512):
    T, W3 = rkv.shape
    W = W3 // 3
    LP = lora.shape[1]
    tc = _pick(W, tc)
    nj = W // tc
    halo = prev[0].shape[0] != T or T == 1
    tt = _pick(T, tt, SUBLANES)
    hb = tt // SUBLANES
    xblk = lambda off: pl.BlockSpec((tt, tc), lambda i, j: (i, off * nj + j))
    pblk = lambda off: pl.BlockSpec((SUBLANES, tc), lambda i, j: (jnp.maximum(i * hb - 1, 0), off * nj + j))
    cvec = lambda off: pl.BlockSpec((1, tc), lambda i, j: (0, off * nj + j))
    in_specs = [xblk(0), xblk(1), xblk(2), pl.BlockSpec((tt, LP), lambda i, j: (i, 0))]
    args = [rkv, rkv, rkv, lora]
    if halo:
        in_specs += [pblk(0), pblk(1), pblk(2),
                     pl.BlockSpec((SUBLANES, LP), lambda i, j: (jnp.maximum(i * hb - 1, 0), 0)),
                     cvec(0), cvec(1), cvec(2), pl.BlockSpec((1, LP), lambda i, j: (0, 0))]
        args += [rkv, rkv, rkv, lora, prev[0], prev[0], prev[0], prev[1]]
    else:
        in_specs += [xblk(0), xblk(1), xblk(2), pl.BlockSpec((tt, LP), lambda i, j: (i, 0))]
        args += [prev[0], prev[0], prev[0], prev[1]]
    in_specs += [cvec(0), cvec(1), cvec(2), pl.BlockSpec((1, LP), lambda i, j: (0, 0)),
                 cvec(0), pl.BlockSpec((w2.shape[0], tc), lambda i, j: (0, j)),
                 cvec(0), pl.BlockSpec((a2.shape[0], tc), lambda i, j: (0, j)),
                 pl.BlockSpec((g2p.shape[0], tc), lambda i, j: (0, j)),
                 cvec(0), cvec(0)]
    args += [mu_rkv, mu_rkv, mu_rkv, mu_lora, w0.reshape(1, W), w2, a0.reshape(1, W), a2, g2p,
             k_k.reshape(1, W), k_a.reshape(1, W)]
    npair = tc // LANES
    oshape = jax.ShapeDtypeStruct((W // LANES, T, LANES), F32)
    ospec = pl.BlockSpec((npair, tt, LANES), lambda i, j: (j, i, 0))
    return pl.pallas_call(
        functools.partial(_prep_kernel, halo=halo, rh=rh, exact=exact, tc=tc),
        grid=(T // tt, nj),
        in_specs=in_specs,
        out_specs=[ospec] * 7,
        out_shape=[oshape] * 7,
        compiler_params=_cparams("parallel", "arbitrary"),
        name="rwkv_prep",
    )(*args)


def _scan_kernel(r_ref, lw_ref, k_ref, v_ref, kk_ref, b_ref, g_ref, s0_ref, rk_ref, lnw_ref, lnb_ref,
                 y_ref, sout_ref, s_sc, *, C, rh, npair, precise, t_real):
    c = pl.program_id(1)
    R = 2 * C
    lane = lax.broadcasted_iota(jnp.int32, (C, LANES), 1)
    head0 = lane < rh
    ri = lax.broadcasted_iota(jnp.int32, (R, R), 0)
    ci = lax.broadcasted_iota(jnp.int32, (R, R), 1)
    same = (ri // C) == (ci // C)
    strict = jnp.logical_and(same, (ci % C) < (ri % C))
    incl = jnp.logical_and(same, (ci % C) <= (ri % C))
    tril_c = (lax.broadcasted_iota(jnp.int32, (C, C), 1)
              <= lax.broadcasted_iota(jnp.int32, (C, C), 0)).astype(F32)
    seg = _segsum_matrix(LANES, rh)
    zero = jnp.zeros((rh, rh), F32)

    def stack_masked(x):
        return jnp.concatenate([jnp.where(head0, x, 0.0), jnp.where(head0, 0.0, x)], axis=0)

    def stack_dup(x):
        return jnp.concatenate([x, x], axis=0)

    @pl.when(c == 0)
    def _():
        def init(p, carry):
            top = jnp.concatenate([s0_ref[0, 2 * p], zero], axis=1)
            bot = jnp.concatenate([zero, s0_ref[0, 2 * p + 1]], axis=1)
            s_sc[p] = jnp.concatenate([top, bot], axis=0)
            return carry
        lax.fori_loop(0, npair, init, 0)

    def slab(r, lw, k, v, kk, b, g, S, rk, lnw, lnb):
        cum = yield tril_c, lw, _NN, True
        tot = cum[C - 1:C, :]
        e_incl = jnp.exp(cum)
        e_inv = jnp.exp(-cum)
        e_end = jnp.exp(tot - cum)
        a2 = stack_masked(-kk * jnp.exp(cum - lw))
        r2 = stack_masked(r * e_incl)
        v2 = stack_masked(v)
        bd2 = stack_masked(b * e_end)
        kd2 = stack_masked(k * e_end)
        b2 = stack_dup(b * e_inv)
        k2 = stack_dup(k * e_inv)
        ar = jnp.concatenate([a2, r2], axis=0)
        arb = yield ar, b2, _NT, precise
        ark = yield ar, k2, _NT, precise
        ars = yield ar, S, _NT, precise
        mrb = jnp.where(incl, arb[R:], 0.0)
        mrk = jnp.where(incl, ark[R:], 0.0)
        x = ars[:R]
        n_it = (t_real - 1).bit_length()
        if n_it:
            x = x + (yield jnp.where(strict, ark[:R], 0.0), v2, _NN, precise)
            pw = jnp.where(strict, arb[:R], 0.0)
        for it in range(n_it):
            x = x + (yield pw, x, _NN, precise)
            if it + 1 < n_it:
                pw = yield pw, pw, _NN, precise
        y2 = ars[R:] + (yield mrb, x, _NN, precise)
        y2 = y2 + (yield mrk, v2, _NN, precise)
        y = y2[:C] + y2[C:]
        s_new = S * jnp.exp(tot) + (yield x, bd2, _TN, precise)
        s_new = s_new + (yield v2, kd2, _TN, precise)
        mean = (yield y, seg, _NN, precise) * (1.0 / rh)
        d = y - mean
        var = (yield d * d, seg, _NN, precise) * (1.0 / rh)
        yn = d * lax.rsqrt(var + GN_EPS) * lnw + lnb
        bonus = (yield r * k * rk, seg, _NN, precise) * v
        return (yn + bonus) * g, s_new

    def run_lockstep(gens):
        reqs = [next(gen) for gen in gens]
        results = [None] * len(gens)
        live = list(range(len(gens)))
        while live:
            vals = [_dg(*reqs[i]) for i in live]
            nxt = []
            for i, val in zip(live, vals):
                try:
                    reqs[i] = gens[i].send(val)
                    nxt.append(i)
                except StopIteration as done:
                    results[i] = done.value
            live = nxt
        return results

    U = math.gcd(npair, SLABS_PER_GROUP)

    def body(i, carry):
        grp = pl.ds(pl.multiple_of(i * U, U), U)
        ins = [ref[grp] for ref in (r_ref, lw_ref, k_ref, v_ref, kk_ref, b_ref, g_ref, s_sc,
                                    rk_ref, lnw_ref, lnb_ref)]
        outs = run_lockstep([slab(*[a[u] for a in ins]) for u in range(U)])
        y_ref[grp] = jnp.stack([o[0] for o in outs]).astype(y_ref.dtype)
        s_sc[grp] = jnp.stack([o[1] for o in outs])
        return carry

    lax.fori_loop(0, npair // U, body, 0)

    @pl.when(c == pl.num_programs(1) - 1)
    def _():
        def fin(p, carry):
            S = s_sc[p]
            sout_ref[0, 2 * p] = S[:rh, :rh]
            sout_ref[0, 2 * p + 1] = S[rh:, rh:]
            return carry
        lax.fori_loop(0, npair, fin, 0)


def _rwkv_scan(prep, s0, r_k, ln_w, ln_b, *, B, C, rh, precise, out_dtype, t_real):
    npair, BT, _ = prep[0].shape
    Tb = BT // B
    nc = Tb // C
    H = s0.shape[1]
    xspec = pl.BlockSpec((npair, C, LANES), lambda b, c: (0, b * nc + c, 0))
    sspec = pl.BlockSpec((1, H, rh, rh), lambda b, c: (b, 0, 0, 0))
    pspec = pl.BlockSpec((npair, 1, LANES), lambda b, c: (0, 0, 0))
    y, s_out = pl.pallas_call(
        functools.partial(_scan_kernel, C=C, rh=rh, npair=npair, precise=precise, t_real=t_real),
        grid=(B, nc),
        in_specs=[xspec] * 7 + [sspec, pspec, pspec, pspec],
        out_specs=[xspec, sspec],
        out_shape=[jax.ShapeDtypeStruct((npair, BT, LANES), out_dtype),
                   jax.ShapeDtypeStruct((B, H, rh, rh), F32)],
        scratch_shapes=[pltpu.VMEM((npair, 2 * rh, 2 * rh), F32)],
        compiler_params=_cparams("parallel", "arbitrary"),
        name="rwkv_scan",
    )(*prep, s0, r_k.reshape(npair, 1, LANES), ln_w.reshape(npair, 1, LANES), ln_b.reshape(npair, 1, LANES))
    return y, s_out


def _router_kernel(x_ref, nw_ref, rw_ref, rb_ref, h_ref, lg_ref, *, precise):
    x = x_ref[...]
    ms = jnp.mean(x * x, axis=-1, keepdims=True)
    h = x * lax.rsqrt(ms + NORM_EPS) * nw_ref[...]
    h_ref[...] = h.astype(h_ref.dtype)
    lg_ref[...] = _dg(h, rw_ref[...], _NN, precise) + rb_ref[...]


def _router(x, norm_w, rw, rb, *, precise, h_dtype, tm=512):
    M, D = x.shape
    NR = rw.shape[1]
    tm = _pick(M, tm, SUBLANES)
    return pl.pallas_call(
        functools.partial(_router_kernel, precise=precise),
        grid=(M // tm,),
        in_specs=[pl.BlockSpec((tm, D), lambda i: (i, 0)), pl.BlockSpec((1, D), lambda i: (0, 0)),
                  pl.BlockSpec((D, NR), lambda i: (0, 0)), pl.BlockSpec((1, NR), lambda i: (0, 0))],
        out_specs=[pl.BlockSpec((tm, D), lambda i: (i, 0)), pl.BlockSpec((tm, NR), lambda i: (i, 0))],
        out_shape=[jax.ShapeDtypeStruct((M, D), h_dtype), jax.ShapeDtypeStruct((M, NR), F32)],
        compiler_params=_cparams("parallel"),
        name="moe_router",
    )(x, norm_w.reshape(1, D), rw, rb)


def _moe_kernel(be_ref, nu_ref, x_ref, wg_ref, wu_ref, wd_ref, sw_ref, o_ref):
    b = pl.program_id(0)
    j = pl.program_id(1)
    last = j == pl.num_programs(1) - 1
    used = b < nu_ref[0]

    @pl.when(jnp.logical_and(used, j == 0))
    def _():
        o_ref[...] = jnp.zeros(o_ref.shape, F32)

    @pl.when(used)
    def _():
        x = x_ref[...]
        tn = wg_ref.shape[2]
        wgu = jnp.concatenate([wg_ref[0].astype(BF16), wu_ref[0].astype(BF16)], axis=1)
        hgu = jnp.dot(x, wgu, preferred_element_type=F32)
        hg, hu = hgu[:, :tn], hgu[:, tn:]
        h = hg * _sigmoid(hg) * hu
        o_ref[...] += jnp.dot(h.astype(BF16), wd_ref[0].astype(BF16), preferred_element_type=F32)

    @pl.when(jnp.logical_and(used, last))
    def _():
        o_ref[...] = o_ref[...] * sw_ref[...]

    @pl.when(jnp.logical_and(jnp.logical_not(used), last))
    def _():
        o_ref[...] = jnp.zeros(o_ref.shape, F32)


def _moe_experts(xs, blk_e, n_used, slot_w, wg, wu, wd, *, blk, tn):
    P, D = xs.shape
    DE = wg.shape[2]
    tn = _pick(DE, tn)
    nj = DE // tn
    nb = P // blk

    def eff(b, j, nu):
        live = b < nu[0]
        return jnp.where(live, b, nu[0] - 1), jnp.where(live, j, nj - 1)

    def wg_map(b, j, be, nu):
        bb, jj = eff(b, j, nu)
        return be[bb], 0, jj

    def wd_map(b, j, be, nu):
        bb, jj = eff(b, j, nu)
        return be[bb], jj, 0

    def x_map(b, j, be, nu):
        return eff(b, j, nu)[0], 0

    grid_spec = pltpu.PrefetchScalarGridSpec(
        num_scalar_prefetch=2,
        grid=(nb, nj),
        in_specs=[pl.BlockSpec((blk, D), x_map),
                  pl.BlockSpec((1, D, tn), wg_map), pl.BlockSpec((1, D, tn), wg_map),
                  pl.BlockSpec((1, tn, D), wd_map),
                  pl.BlockSpec((blk, 1), x_map)],
        out_specs=pl.BlockSpec((blk, D), lambda b, j, be, nu: (b, 0)),
    )
    return pl.pallas_call(
        _moe_kernel,
        grid_spec=grid_spec,
        out_shape=jax.ShapeDtypeStruct((P, D), F32),
        compiler_params=_cparams("arbitrary", "arbitrary"),
        name="moe_experts",
    )(blk_e, n_used, xs, wg, wu, wd, slot_w.reshape(P, 1))


def _hier_moe(x, norm_w, rg_w, rg_b, re_w, re_b, wg, wu, wd, *, blk, tn):
    T, D = x.shape
    NG = rg_w.shape[1]
    E = re_w.shape[1]
    EPG = E // NG
    NR = -(-(NG + E) // LANES) * LANES
    rw = jnp.pad(jnp.concatenate([rg_w, re_w], axis=1), ((0, 0), (0, NR - NG - E)))
    rb = jnp.pad(jnp.concatenate([rg_b, re_b]), (0, NR - NG - E)).reshape(1, NR)
    h, logits = _router(x, norm_w, rw.astype(BF16), rb, precise=False, h_dtype=BF16)
    gprob = jax.nn.softmax(logits[:, :NG], axis=-1)
    gsel = jnp.argmax(gprob, axis=-1)
    gp = jnp.take_along_axis(gprob, gsel[:, None], axis=-1)[:, 0]
    elog = logits[:, NG:NG + E].reshape(T, NG, EPG)
    elog = jnp.take_along_axis(elog, gsel[:, None, None], axis=1)[:, 0]
    top_p, top_i = lax.top_k(jax.nn.softmax(elog, axis=-1), TOP_K)
    top_p = top_p / jnp.sum(top_p, axis=-1, keepdims=True)
    weights = gp[:, None] * top_p
    expert_idx = gsel[:, None].astype(jnp.int32) * EPG + top_i.astype(jnp.int32)

    TK = T * TOP_K
    nb = -(-TK // blk) + E
    P = nb * blk
    flat_e = expert_idx.reshape(-1)
    flat_w = weights.reshape(-1)
    flat_t = jnp.repeat(jnp.arange(T, dtype=jnp.int32), TOP_K)
    order = jnp.argsort(flat_e)
    se = flat_e[order]
    counts = jnp.bincount(flat_e, length=E)
    pcounts = (counts + blk - 1) // blk * blk
    pend = jnp.cumsum(pcounts)
    pstart = pend - pcounts
    start = jnp.cumsum(counts) - counts
    dest = (pstart[se] + jnp.arange(TK) - start[se]).astype(jnp.int32)
    slot_t = jnp.full((P,), T, jnp.int32).at[dest].set(flat_t[order])
    slot_w = jnp.zeros((P,), F32).at[dest].set(flat_w[order])
    blk_e = jnp.minimum(jnp.searchsorted(pend, jnp.arange(nb) * blk, side='right'), E - 1).astype(jnp.int32)
    n_used = (pend[-1] // blk).astype(jnp.int32).reshape(1)
    pos = jnp.zeros((TK,), jnp.int32).at[order].set(dest)

    xs = jnp.concatenate([h, jnp.zeros((1, D), h.dtype)], axis=0)[slot_t]
    ys = _moe_experts(xs, blk_e, n_used, slot_w, wg, wu, wd, blk=blk, tn=tn)
    pos = pos.reshape(T, TOP_K)
    return x + ys[pos[:, 0]] + ys[pos[:, 1]]


def _pairs_to_rows(y):
    npair, T, _ = y.shape
    return jnp.transpose(y, (1, 0, 2)).reshape(T, npair * LANES)


def _layer(x, positions, prev_rows, s0, B, cfg, w, *, exact, attn_fn, q_scale, emit_bf16, moe_blk, moe_tn):
    T, D = x.shape
    hd, HK, G, vd, rh = cfg['hd'], cfg['HK'], cfg['G'], cfg['vd'], cfg['rh']
    qcols, kcols, vcols = HK * G * 2 * hd, HK * 2 * hd, HK * vd
    W = cfg['W']
    mm = functools.partial(_matmul, precise=False)

    h = _rmsnorm(x, w['norm1_w'], BF16)
    qkv = mm(h, w['w_qkv'], qcols + kcols + vcols)
    rkv = mm(h, w['w_rkv'], 3 * W)
    lora = mm(h, w['w_lora'], cfg['lp'], tn=cfg['lp'])
    gates = mm(h, w['w_gates'], 2 * D)

    cos_full, sin_signed = _rope_tables(positions, hd)
    outs = _qk_rope(qkv, cos_full, sin_signed, w['q_norm_w'], w['k_norm_w'], qcols, kcols, vcols,
                    q_scale=q_scale, q_dtype=BF16, emit_bf16=emit_bf16)
    q, k_new, v_new = outs[:3]
    o_attn = attn_fn(q, outs)

    prep = _rwkv_prep(rkv, lora, prev_rows, w['mu_rkv'], w['mu_lora'], w['rwkv_w0'], w['rwkv_w2'],
                      w['rwkv_a0'], w['rwkv_a2'], w['g2p'], w['rwkv_k_k'], w['rwkv_k_a'],
                      rh=rh, exact=exact)
    Tb = T // B
    if Tb % 64 == 0:
        C = 64
    else:
        C = SUBLANES
        pad = -Tb % C
        prep = [jnp.pad(a.reshape(-1, B, Tb, LANES), ((0, 0), (0, 0), (0, pad), (0, 0)))
                .reshape(a.shape[0], B * (Tb + pad), LANES) for a in prep]
    y_r, s_fin = _rwkv_scan(prep, s0, w['rwkv_r_k'], w['rwkv_ln_w'], w['rwkv_ln_b'],
                            B=B, C=C, rh=rh, precise=exact, out_dtype=BF16, t_real=min(Tb, C))
    if Tb % 64:
        y_r = y_r.reshape(y_r.shape[0], B, -1, LANES)[:, :, :Tb].reshape(y_r.shape[0], T, LANES)
    o_rwkv = _pairs_to_rows(y_r)

    mixed = _merge(o_attn, w['w_branch_attn'], o_rwkv, w['w_branch_rwkv'], gates, w['gate_b'],
                   precise=False, out_dtype=BF16)
    x1 = _matmul_residual(mixed, w['w_out'], x, precise=False)
    y = _hier_moe(x1, w['norm2_w'], w['router_group_w'], w['router_group_b'], w['router_expert_w'],
                  w['router_expert_b'], w['expert_w_gate'], w['expert_w_up'], w['expert_w_down'],
                  blk=moe_blk, tn=moe_tn)
    return y, k_new, v_new, s_fin, rkv, lora


def kernel(x_prompt, x_sample, cache_k, cache_v, state_rwkv, state_shift, page_table, norm1_w, w_in, gate_b, q_norm_w, k_norm_w, lambda_q1, lambda_k1, lambda_q2, lambda_k2, subln_w, rwkv_mu, rwkv_w0, rwkv_w2, rwkv_a0, rwkv_a2, rwkv_g2, rwkv_k_k, rwkv_k_a, rwkv_r_k, rwkv_ln_w, rwkv_ln_b, w_branch_attn, w_branch_rwkv, w_out, norm2_w, router_group_w, router_group_b, router_expert_w, router_expert_b, expert_w_gate, expert_w_up, expert_w_down):
    B, T, D = x_prompt.shape
    DB, S, _ = x_sample.shape
    L = norm1_w.shape[0]
    assert L == 1 and B == 1 and S == 1, "one trunk layer, one prompt sequence, one new token per sample"
    hd = q_norm_w.shape[1]
    HK = cache_k.shape[3]
    vd = cache_v.shape[4]
    G = w_branch_attn.shape[1] // vd // HK
    H, rh = rwkv_r_k.shape[1], rwkv_r_k.shape[2]
    W = H * rh
    dl, al, gl = rwkv_w2.shape[1], rwkv_a2.shape[1], rwkv_g2.shape[1]
    lcols = dl + al + gl
    lp = -(-lcols // LANES) * LANES
    qkv_cols = HK * G * 2 * hd + HK * 2 * hd + HK * vd
    r_off = qkv_cols
    g_off = r_off + 3 * W + lcols
    n_pages, page = page_table.shape[1], cache_k.shape[2]
    past = n_pages * page
    cfg = dict(hd=hd, HK=HK, G=G, vd=vd, rh=rh, W=W, lp=lp, r_off=r_off,
               moe_blk_p=min(512, max(SUBLANES * 2, T // 16)), moe_tn_p=256,
               moe_blk_s=2 * SUBLANES, moe_tn_s=256)
    l = 0
    (cache_k, cache_v, state_rwkv, state_shift, norm1_w, w_in, gate_b, q_norm_w, k_norm_w, lambda_q1,
     lambda_k1, lambda_q2, lambda_k2, subln_w, rwkv_mu, rwkv_w0, rwkv_w2, rwkv_a0, rwkv_a2, rwkv_g2,
     rwkv_k_k, rwkv_k_a, rwkv_r_k, rwkv_ln_w, rwkv_ln_b, w_branch_attn, w_branch_rwkv, w_out, norm2_w,
     router_group_w, router_group_b, router_expert_w, router_expert_b, expert_w_gate, expert_w_up,
     expert_w_down) = [a.reshape(a.shape[1:]) for a in (
         cache_k, cache_v, state_rwkv, state_shift, norm1_w, w_in, gate_b, q_norm_w, k_norm_w, lambda_q1,
         lambda_k1, lambda_q2, lambda_k2, subln_w, rwkv_mu, rwkv_w0, rwkv_w2, rwkv_a0, rwkv_a2, rwkv_g2,
         rwkv_k_k, rwkv_k_a, rwkv_r_k, rwkv_ln_w, rwkv_ln_b, w_branch_attn, w_branch_rwkv, w_out, norm2_w,
         router_group_w, router_group_b, router_expert_w, router_expert_b, expert_w_gate, expert_w_up,
         expert_w_down)]
    lam_init = 0.8 - 0.6 * math.exp(-0.3 * l)
    lams = (lambda_q1, lambda_k1, lambda_q2, lambda_k2)

    mu = rwkv_mu
    g2p = jnp.pad(rwkv_g2, ((0, lp - lcols), (0, 0)))
    w_lora = jnp.pad(w_in[:, r_off + 3 * W:r_off + 3 * W + lcols], ((0, 0), (0, lp - lcols)))
    w = dict(
        norm1_w=norm1_w, q_norm_w=q_norm_w, k_norm_w=k_norm_w, gate_b=gate_b,
        mu_rkv=mu[:3 * W].reshape(1, 3 * W), mu_lora=jnp.pad(mu[3 * W:], (0, lp - lcols)).reshape(1, lp),
        rwkv_w0=rwkv_w0, rwkv_a0=rwkv_a0, rwkv_k_k=rwkv_k_k, rwkv_k_a=rwkv_k_a,
        rwkv_r_k=rwkv_r_k, rwkv_ln_w=rwkv_ln_w, rwkv_ln_b=rwkv_ln_b,
        norm2_w=norm2_w, router_group_w=router_group_w, router_group_b=router_group_b,
        router_expert_w=router_expert_w, router_expert_b=router_expert_b,
        expert_w_gate=expert_w_gate, expert_w_up=expert_w_up, expert_w_down=expert_w_down,
        w_qkv=w_in[:, :qkv_cols].astype(BF16), w_rkv=w_in[:, r_off:r_off + 3 * W].astype(BF16),
        w_lora=w_lora.astype(BF16), w_gates=w_in[:, g_off:g_off + 2 * D].astype(BF16),
        rwkv_w2=rwkv_w2.astype(BF16), rwkv_a2=rwkv_a2.astype(BF16), g2p=g2p.astype(BF16),
        w_branch_attn=w_branch_attn.astype(BF16), w_branch_rwkv=w_branch_rwkv.astype(BF16),
        w_out=w_out.astype(BF16))

    def prompt_attn(q, outs):
        return _flash_diff_attn(q, outs[3], outs[4], lams, subln_w, HK=HK, G=G, hd=hd, vd=vd,
                                lam_init=lam_init, out_dtype=BF16)

    zero_prev = (jnp.zeros((1, 3 * W), F32), jnp.zeros((1, lp), F32))
    yp, kp, vp, sp, rkv_p, lora_p = _layer(
        x_prompt.reshape(T, D), jnp.arange(T), zero_prev, jnp.zeros((1, H, rh, rh), F32), 1, cfg, w,
        exact=False, attn_fn=prompt_attn, q_scale=hd ** -0.5 * math.log2(math.e), emit_bf16=True,
        moe_blk=cfg['moe_blk_p'], moe_tn=cfg['moe_tn_p'])
    shift_p = jnp.concatenate([rkv_p[T - 1], lora_p[T - 1, :lcols]])

    def sample_attn(q, outs):
        return _decode_diff_attn(q, outs[1], outs[2], cache_k, cache_v, page_table, lams, subln_w,
                                 HK=HK, G=G, hd=hd, vd=vd, lam_init=lam_init)

    sh = state_shift
    prev_s = (sh[:, :3 * W], jnp.pad(sh[:, 3 * W:], ((0, 0), (0, lp - lcols))))
    ys, ks, vs, ss, rkv_s, lora_s = _layer(
        x_sample.reshape(DB, D), jnp.full((DB,), past, jnp.int32), prev_s, state_rwkv, DB, cfg, w,
        exact=True, attn_fn=sample_attn, q_scale=1.0, emit_bf16=False,
        moe_blk=cfg['moe_blk_s'], moe_tn=cfg['moe_tn_s'])
    shift_s = jnp.concatenate([rkv_s, lora_s[:, :lcols]], axis=1)

    return (yp.reshape(1, T, D), ys.reshape(DB, 1, D),
            kp.reshape(1, 1, T, HK, 2, hd), vp.reshape(1, 1, T, HK, vd),
            sp.reshape(1, 1, H, rh, rh), shift_p.reshape(1, 1, -1),
            ks.reshape(1, DB, 1, HK, 2, hd), vs.reshape(1, DB, 1, HK, vd),
            ss.reshape(1, DB, H, rh, rh), shift_s.reshape(1, DB, -1))
```

```python
import functools
import math

import jax
import jax.numpy as jnp
from jax import lax
from jax.experimental import pallas as pl
from jax.experimental.pallas import tpu as pltpu

F32 = jnp.float32
BF16 = jnp.bfloat16
HIGHEST = lax.Precision.HIGHEST

LANES = 128
SUBLANES = 8
VMEM_LIMIT = 56 * 1024 * 1024

ROPE_THETA = 10000.0
NORM_EPS = 1e-6
GN_EPS = 64e-5
TOP_K = 2
SLABS_PER_GROUP = 16
DECODE_PAGES_PER_STEP = 8
MOE_ROW_BLOCK = 512
MOE_HIDDEN_TILE = 256

_NN = (((1,), (0,)), ((), ()))
_NT = (((1,), (1,)), ((), ()))
_TN = (((0,), (0,)), ((), ()))


def _cparams(*sem):
    return pltpu.CompilerParams(dimension_semantics=sem, vmem_limit_bytes=VMEM_LIMIT)


def _pick(n, pref, unit=LANES):
    if n <= pref:
        return n
    t = pref // unit * unit
    while t > unit and n % t:
        t -= unit
    assert n % t == 0, (n, pref)
    return t


def _dg(a, b, dn, precise):
    if precise:
        return lax.dot_general(a.astype(F32), b.astype(F32), dn, precision=HIGHEST,
                               preferred_element_type=F32)
    return lax.dot_general(a.astype(BF16), b.astype(BF16), dn, preferred_element_type=F32)


def _sigmoid(x):
    return 1.0 / (1.0 + jnp.exp(-x))


def _softplus(x):
    return jnp.maximum(x, 0.0) + jnp.log(1.0 + jnp.exp(-jnp.abs(x)))


def _rms_kernel(x_ref, w_ref, o_ref, *, eps):
    x = x_ref[...]
    ms = jnp.mean(x * x, axis=-1, keepdims=True)
    o_ref[...] = (x * lax.rsqrt(ms + eps) * w_ref[...]).astype(o_ref.dtype)


def _rmsnorm(x, w, out_dtype, tm=512):
    M, D = x.shape
    tm = _pick(M, tm, SUBLANES)
    return pl.pallas_call(
        functools.partial(_rms_kernel, eps=NORM_EPS),
        grid=(M // tm,),
        in_specs=[pl.BlockSpec((tm, D), lambda i: (i, 0)),
                  pl.BlockSpec((1, D), lambda i: (0, 0))],
        out_specs=pl.BlockSpec((tm, D), lambda i: (i, 0)),
        out_shape=jax.ShapeDtypeStruct((M, D), out_dtype),
        compiler_params=_cparams("parallel"),
        name="rmsnorm",
    )(x, w.reshape(1, D))


def _mm_kernel(a_ref, b_ref, o_ref, *, precise):
    o_ref[...] = _dg(a_ref[...], b_ref[...], _NN, precise).astype(o_ref.dtype)


def _matmul(a, b, n, col0=0, *, precise, out_dtype=F32, tm=1024, tn=512):
    M, K = a.shape
    tm = _pick(M, tm, SUBLANES)
    tn = _pick(n, tn)
    if col0 % tn:
        b, col0 = b[:, col0:col0 + n], 0
    c0 = col0 // tn
    return pl.pallas_call(
        functools.partial(_mm_kernel, precise=precise),
        grid=(M // tm, n // tn),
        in_specs=[pl.BlockSpec((tm, K), lambda i, j: (i, 0)),
                  pl.BlockSpec((K, tn), lambda i, j: (0, c0 + j))],
        out_specs=pl.BlockSpec((tm, tn), lambda i, j: (i, j)),
        out_shape=jax.ShapeDtypeStruct((M, n), out_dtype),
        compiler_params=_cparams("parallel", "arbitrary"),
        name="matmul",
    )(a, b)


def _merge_kernel(a1_ref, b1_ref, a2_ref, b2_ref, g1_ref, g2_ref, gb1_ref, gb2_ref, o_ref, *, precise):
    o1 = _dg(a1_ref[...], b1_ref[...], _NN, precise)
    o2 = _dg(a2_ref[...], b2_ref[...], _NN, precise)
    s1 = _sigmoid(g1_ref[...] + gb1_ref[...])
    s2 = _sigmoid(g2_ref[...] + gb2_ref[...])
    o_ref[...] = (s1 * o1 + s2 * o2).astype(o_ref.dtype)


def _merge(a1, b1, a2, b2, gates, gate_b, *, precise, out_dtype, tm=512, tn=512):
    M, K = a1.shape
    K2 = a2.shape[1]
    D = b1.shape[1]
    tm = _pick(M, tm, SUBLANES)
    tn = _pick(D, tn)
    nj = D // tn
    gb = gate_b.reshape(1, 2 * D)
    return pl.pallas_call(
        functools.partial(_merge_kernel, precise=precise),
        grid=(M // tm, nj),
        in_specs=[pl.BlockSpec((tm, K), lambda i, j: (i, 0)),
                  pl.BlockSpec((K, tn), lambda i, j: (0, j)),
                  pl.BlockSpec((tm, K2), lambda i, j: (i, 0)),
                  pl.BlockSpec((K2, tn), lambda i, j: (0, j)),
                  pl.BlockSpec((tm, tn), lambda i, j: (i, j)),
                  pl.BlockSpec((tm, tn), lambda i, j: (i, nj + j)),
                  pl.BlockSpec((1, tn), lambda i, j: (0, j)),
                  pl.BlockSpec((1, tn), lambda i, j: (0, nj + j))],
        out_specs=pl.BlockSpec((tm, tn), lambda i, j: (i, j)),
        out_shape=jax.ShapeDtypeStruct((M, D), out_dtype),
        compiler_params=_cparams("parallel", "arbitrary"),
        name="merge",
    )(a1, b1, a2, b2, gates, gates, gb, gb)


def _mm_res_kernel(a_ref, b_ref, r_ref, o_ref, *, precise):
    o_ref[...] = r_ref[...] + _dg(a_ref[...], b_ref[...], _NN, precise)


def _matmul_residual(a, b, res, *, precise, tm=1024, tn=512):
    M, K = a.shape
    N = b.shape[1]
    tm = _pick(M, tm, SUBLANES)
    tn = _pick(N, tn)
    return pl.pallas_call(
        functools.partial(_mm_res_kernel, precise=precise),
        grid=(M // tm, N // tn),
        in_specs=[pl.BlockSpec((tm, K), lambda i, j: (i, 0)),
                  pl.BlockSpec((K, tn), lambda i, j: (0, j)),
                  pl.BlockSpec((tm, tn), lambda i, j: (i, j))],
        out_specs=pl.BlockSpec((tm, tn), lambda i, j: (i, j)),
        out_shape=jax.ShapeDtypeStruct((M, N), F32),
        compiler_params=_cparams("parallel", "arbitrary"),
        name="matmul_residual",
    )(a, b, res)


def _qkrope_kernel(qkv_ref, cos_ref, sin_ref, qw_ref, kw_ref, *out_refs, nq, nk, hd, vcols, scale, emit_bf16):
    if emit_bf16:
        q_ref, k_ref, v_ref, kb_ref, vb_ref = out_refs
    else:
        q_ref, k_ref, v_ref = out_refs
    cos = cos_ref[...]
    sin = sin_ref[...]

    def norm_rope(x, w):
        ms = jnp.mean(x * x, axis=-1, keepdims=True)
        xn = x * lax.rsqrt(ms + NORM_EPS) * w
        return xn * cos + pltpu.roll(xn, hd // 2, axis=1) * sin

    qw = qw_ref[...]
    kw = kw_ref[...]
    for c in range(nq):
        o = norm_rope(qkv_ref[:, c * hd:(c + 1) * hd], qw) * scale
        q_ref[:, c * hd:(c + 1) * hd] = o.astype(q_ref.dtype)
    for c in range(nk):
        o = norm_rope(qkv_ref[:, (nq + c) * hd:(nq + c + 1) * hd], kw)
        k_ref[:, c * hd:(c + 1) * hd] = o
        if emit_bf16:
            kb_ref[:, c * hd:(c + 1) * hd] = o.astype(BF16)
    v = qkv_ref[:, (nq + nk) * hd:(nq + nk) * hd + vcols]
    v_ref[...] = v
    if emit_bf16:
        vb_ref[...] = v.astype(BF16)


def _qk_rope(qkv, cos_full, sin_signed, q_norm_w, k_norm_w, qcols, kcols, vcols, *, q_scale, q_dtype,
             emit_bf16, tq=256):
    T = qkv.shape[0]
    hd = q_norm_w.shape[0]
    tq = _pick(T, tq, SUBLANES)
    nq, nk = qcols // hd, kcols // hd
    row = lambda i: (i, 0)
    out_shape = [jax.ShapeDtypeStruct((T, qcols), q_dtype),
                 jax.ShapeDtypeStruct((T, kcols), F32),
                 jax.ShapeDtypeStruct((T, vcols), F32)]
    out_specs = [pl.BlockSpec((tq, qcols), row), pl.BlockSpec((tq, kcols), row), pl.BlockSpec((tq, vcols), row)]
    if emit_bf16:
        out_shape += [jax.ShapeDtypeStruct((T, kcols), BF16), jax.ShapeDtypeStruct((T, vcols), BF16)]
        out_specs += [pl.BlockSpec((tq, kcols), row), pl.BlockSpec((tq, vcols), row)]
    return pl.pallas_call(
        functools.partial(_qkrope_kernel, nq=nq, nk=nk, hd=hd, vcols=vcols, scale=q_scale,
                          emit_bf16=emit_bf16),
        grid=(T // tq,),
        in_specs=[pl.BlockSpec((tq, qkv.shape[1]), row),
                  pl.BlockSpec((tq, hd), row), pl.BlockSpec((tq, hd), row),
                  pl.BlockSpec((1, hd), lambda i: (0, 0)), pl.BlockSpec((1, hd), lambda i: (0, 0))],
        out_specs=out_specs,
        out_shape=out_shape,
        compiler_params=_cparams("parallel"),
        name="qk_norm_rope",
    )(qkv, cos_full, sin_signed, q_norm_w.reshape(1, hd), k_norm_w.reshape(1, hd))


def _rope_tables(positions, hd):
    inv = 1.0 / (ROPE_THETA ** (jnp.arange(0, hd, 2, dtype=F32) / hd))
    ang = positions.astype(F32)[:, None] * inv[None, :]
    cos, sin = jnp.cos(ang), jnp.sin(ang)
    return jnp.concatenate([cos, cos], axis=-1), jnp.concatenate([-sin, sin], axis=-1)


def _diff_lambda_in_kernel(lq1_ref, lk1_ref, lq2_ref, lk2_ref, lam_init):
    e1 = jnp.exp(jnp.sum(lq1_ref[...] * lk1_ref[...], axis=-1, keepdims=True))
    e2 = jnp.exp(jnp.sum(lq2_ref[...] * lk2_ref[...], axis=-1, keepdims=True))
    return e1 - e2 + lam_init


def _attn_finalize(acc1, l1, acc2, l2, lam, subw, lam_init):
    o = acc1 / l1 - lam * (acc2 / l2)
    ms = jnp.mean(o * o, axis=-1, keepdims=True)
    return o * lax.rsqrt(ms + NORM_EPS) * subw * (1.0 - lam_init)


def _flash_kernel(q_ref, k_ref, v_ref, lq1_ref, lk1_ref, lq2_ref, lk2_ref, subw_ref, o_ref,
                  m_sc, l_sc, acc_sc, s_sc, *, G, hd, vd, tq, lam_init):
    qi = pl.program_id(1)
    ki = pl.program_id(2)

    @pl.when(ki == 0)
    def _():
        m_sc[...] = jnp.full(m_sc.shape, -jnp.inf, F32)
        l_sc[...] = jnp.zeros(l_sc.shape, F32)
        acc_sc[...] = jnp.zeros(acc_sc.shape, F32)

    def tile(diagonal):
        v = v_ref[...]
        for m in range(2):
            kb = k_ref[:, m * hd:(m + 1) * hd]
            for g in range(G):
                idx = g * 2 + m
                q = q_ref[:, idx * hd:(idx + 1) * hd]
                s = lax.dot_general(q, kb, _NT, preferred_element_type=F32)
                if diagonal:
                    rows = lax.broadcasted_iota(jnp.int32, (tq, tq), 0)
                    cols = lax.broadcasted_iota(jnp.int32, (tq, tq), 1)
                    s = jnp.where(cols <= rows, s, -jnp.inf)
                s_sc[...] = s
                m_prev = m_sc[idx]
                m_new = jnp.maximum(m_prev, jnp.max(s_sc[...], axis=-1, keepdims=True))
                m_sc[idx] = m_new
                alpha = jnp.exp2(m_prev - m_new)
                p = jnp.exp2(s_sc[...] - jnp.tile(m_new, (1, tq // LANES)))
                l_sc[idx] = alpha * l_sc[idx] + jnp.sum(p, axis=-1, keepdims=True)
                pv = jnp.dot(p.astype(BF16), v, preferred_element_type=F32)
                acc_sc[idx] = jnp.tile(alpha, (1, vd // LANES)) * acc_sc[idx] + pv

    @pl.when(ki < qi)
    def _():
        tile(False)

    @pl.when(ki == qi)
    def _():
        tile(True)
        lam = _diff_lambda_in_kernel(lq1_ref, lk1_ref, lq2_ref, lk2_ref, lam_init)
        subw = subw_ref[...]
        for g in range(G):
            o = _attn_finalize(acc_sc[2 * g], l_sc[2 * g][:, :1], acc_sc[2 * g + 1], l_sc[2 * g + 1][:, :1],
                               lam, subw, lam_init)
            o_ref[:, g * vd:(g + 1) * vd] = o.astype(o_ref.dtype)


def _flash_diff_attn(q, kb, vb, lams, subln_w, *, HK, G, hd, vd, lam_init, out_dtype, tq=512):
    T = q.shape[0]
    tq = _pick(T, tq)
    nq = T // tq
    vec = lambda n: pl.BlockSpec((1, n), lambda h, i, j: (0, 0))
    return pl.pallas_call(
        functools.partial(_flash_kernel, G=G, hd=hd, vd=vd, tq=tq, lam_init=lam_init),
        grid=(HK, nq, nq),
        in_specs=[pl.BlockSpec((tq, G * 2 * hd), lambda h, i, j: (i, h)),
                  pl.BlockSpec((tq, 2 * hd), lambda h, i, j: (jnp.minimum(i, j), h)),
                  pl.BlockSpec((tq, vd), lambda h, i, j: (jnp.minimum(i, j), h)),
                  vec(hd), vec(hd), vec(hd), vec(hd), vec(vd)],
        out_specs=pl.BlockSpec((tq, G * vd), lambda h, i, j: (i, h)),
        out_shape=jax.ShapeDtypeStruct((T, HK * G * vd), out_dtype),
        scratch_shapes=[pltpu.VMEM((2 * G, tq, LANES), F32), pltpu.VMEM((2 * G, tq, LANES), F32),
                        pltpu.VMEM((2 * G, tq, vd), F32), pltpu.VMEM((tq, tq), F32)],
        compiler_params=_cparams("parallel", "parallel", "arbitrary"),
        name="flash_diff_attn",
    )(q, kb, vb, *[x.reshape(1, -1) for x in lams], subln_w.reshape(1, vd))


def _decode_kernel(pt_ref, q_ref, kn_ref, vn_ref, *refs, HK, G, vd, pps, scale, lam_init):
    kc_refs, vc_refs = refs[:pps], refs[pps:2 * pps]
    (lq1_ref, lk1_ref, lq2_ref, lk2_ref, subw_ref, o_ref, s_sc, a_sc, anew_sc, acc_sc) = refs[2 * pps:]
    ph = pl.program_id(1)
    pg = pl.program_id(2)
    last = pl.num_programs(2) - 1
    NI = q_ref.shape[1]
    half = NI // 2
    q = q_ref[0]
    nk = q.shape[1] // LANES
    page = kc_refs[0].shape[0] // nk
    nv = vc_refs[0].shape[0] // page
    v_order = [et * HK + hk for hk in range(HK) for et in range(nv // HK)]

    def token_rows(ref, order):
        x = ref[...].reshape(page, len(order) * LANES)
        if list(order) != list(range(len(order))):
            x = jnp.concatenate([x[:, s * LANES:(s + 1) * LANES] for s in order], axis=1)
        return x.astype(BF16)

    row_hk = lax.broadcasted_iota(jnp.int32, (half, vd), 0) // G

    def own_head(x):
        out = x[:, :vd]
        for hk in range(1, HK):
            out = jnp.where(row_hk == hk, x[:, hk * vd:(hk + 1) * vd], out)
        return out

    @pl.when(ph == 0)
    def _():
        s_sc[pg] = scale * jnp.concatenate(
            [lax.dot_general(q, token_rows(kc, range(nk)), _NT, preferred_element_type=F32) for kc in kc_refs],
            axis=1)

    @pl.when(jnp.logical_and(ph == 0, pg == last))
    def _():
        lam = _diff_lambda_in_kernel(lq1_ref, lk1_ref, lq2_ref, lk2_ref, lam_init)
        s_new = scale * jnp.sum(q.astype(F32) * kn_ref[0].astype(BF16).astype(F32), axis=-1, keepdims=True)
        s = s_sc[...]
        m = jnp.maximum(jnp.max(jnp.max(s, axis=0), axis=-1, keepdims=True), s_new)
        e = jnp.exp(s - m[None])
        e_new = jnp.exp(s_new - m)
        denom = jnp.sum(jnp.sum(e, axis=0), axis=-1, keepdims=True) + e_new
        p = e / denom[None]
        p_new = e_new / denom
        a_sc[...] = (p[:, :half] - lam * p[:, half:]).astype(BF16)
        anew_sc[...] = jnp.broadcast_to(p_new[:half] - lam * p_new[half:], (half, LANES))
        acc_sc[...] = jnp.zeros(acc_sc.shape, F32)

    @pl.when(ph == 1)
    def _():
        a = a_sc[pg]
        pv = sum(jnp.dot(a[:, j * page:(j + 1) * page], token_rows(vc, v_order), preferred_element_type=F32)
                 for j, vc in enumerate(vc_refs))
        acc_sc[...] += own_head(pv)

    @pl.when(jnp.logical_and(ph == 1, pg == last))
    def _():
        a_new = anew_sc[...][:, :1]
        v_new = own_head(jnp.broadcast_to(vn_ref[0], (half, HK * vd)))
        o = acc_sc[...] + a_new * v_new
        ms = jnp.mean(o * o, axis=-1, keepdims=True)
        o_ref[0] = o * lax.rsqrt(ms + NORM_EPS) * subw_ref[...] * (1.0 - lam_init)


def _decode_diff_attn(q, k_new, v_new, cache_k, cache_v, page_table, lams, subln_w, *, HK, G, hd, vd, lam_init):
    DB, n_pages = page_table.shape
    n_phys, page = cache_k.shape[0], cache_k.shape[1]
    assert page % LANES == 0 and vd % LANES == 0 and hd == LANES
    kcw, vcw = HK * 2 * hd, HK * vd
    nk, nv = kcw // LANES, vcw // LANES
    kc = cache_k.reshape(n_phys * page * nk, LANES)
    vc = jnp.transpose(cache_v.reshape(n_phys, page, HK, vd // LANES, LANES),
                       (0, 1, 3, 2, 4)).reshape(n_phys * page * nv, LANES)
    NI = 2 * HK * G
    qt = jnp.transpose(q.reshape(DB, HK, G, 2, hd), (0, 3, 1, 2, 4))
    own = (jnp.eye(HK, dtype=q.dtype)[None, None, :, None, :, None, None]
           * jnp.eye(2, dtype=q.dtype)[None, :, None, None, None, :, None])
    qbd = (qt[:, :, :, :, None, None, :] * own).reshape(DB, NI, kcw)
    pps = math.gcd(n_pages, DECODE_PAGES_PER_STEP)
    steps = n_pages // pps
    per_b = lambda r, n: pl.BlockSpec((1, r, n), lambda b, ph, pg, pt: (b, 0, 0))
    vec = lambda n: pl.BlockSpec((1, n), lambda b, ph, pg, pt: (0, 0))

    def k_spec(j):
        return pl.BlockSpec((page * nk, LANES),
                            lambda b, ph, pg, pt: (pt[b, jnp.where(ph == 0, pg, steps - 1) * pps + j], 0))

    def v_spec(j):
        return pl.BlockSpec((page * nv, LANES),
                            lambda b, ph, pg, pt: (pt[b, jnp.where(ph == 1, pg, 0) * pps + j], 0))

    grid_spec = pltpu.PrefetchScalarGridSpec(
        num_scalar_prefetch=1,
        grid=(DB, 2, steps),
        in_specs=[per_b(NI, kcw), per_b(1, kcw), per_b(1, vcw)]
        + [k_spec(j) for j in range(pps)] + [v_spec(j) for j in range(pps)]
        + [vec(hd), vec(hd), vec(hd), vec(hd), vec(vd)],
        out_specs=pl.BlockSpec((1, NI // 2, vd), lambda b, ph, pg, pt: (b, 0, 0)),
        scratch_shapes=[pltpu.VMEM((steps, NI, pps * page), F32), pltpu.VMEM((steps, NI // 2, pps * page), BF16),
                        pltpu.VMEM((NI // 2, LANES), F32), pltpu.VMEM((NI // 2, vd), F32)],
    )
    out = pl.pallas_call(
        functools.partial(_decode_kernel, HK=HK, G=G, vd=vd, pps=pps, scale=hd ** -0.5, lam_init=lam_init),
        grid_spec=grid_spec,
        out_shape=jax.ShapeDtypeStruct((DB, NI // 2, vd), F32),
        compiler_params=_cparams("parallel", "arbitrary", "arbitrary"),
        name="decode_diff_attn",
    )(page_table, qbd, k_new.reshape(DB, 1, kcw), v_new.reshape(DB, 1, vcw), *([kc] * pps), *([vc] * pps),
      *[x.reshape(1, -1) for x in lams], subln_w.reshape(1, vd))
    return out.reshape(DB, HK * G * vd)


def _segsum_matrix(n, seg):
    r = lax.broadcasted_iota(jnp.int32, (n, n), 0) // seg
    c = lax.broadcasted_iota(jnp.int32, (n, n), 1) // seg
    return (r == c).astype(F32)


def _prep_kernel(*refs, halo, rh, exact, tc):
    if halo:
        (r_ref, k_ref, v_ref, l_ref, rp_ref, kp_ref, vp_ref, lp_ref, ri_ref, ki_ref, vi_ref, li_ref,
         mur_ref, muk_ref, muv_ref, mul_ref, w0_ref, w2_ref, a0_ref, a2_ref, g2_ref, kkw_ref, kaw_ref,
         ro_ref, lwo_ref, ko_ref, vo_ref, kko_ref, bo_ref, go_ref) = refs
    else:
        (r_ref, k_ref, v_ref, l_ref, rp_ref, kp_ref, vp_ref, lp_ref,
         mur_ref, muk_ref, muv_ref, mul_ref, w0_ref, w2_ref, a0_ref, a2_ref, g2_ref, kkw_ref, kaw_ref,
         ro_ref, lwo_ref, ko_ref, vo_ref, kko_ref, bo_ref, go_ref) = refs
    i = pl.program_id(0)

    def shifted(x_ref, prev_ref, init_ref):
        x = x_ref[...]
        if not halo:
            return x, prev_ref[...]
        last = prev_ref[SUBLANES - 1:SUBLANES, :]
        first = jnp.where(i == 0, init_ref[...], last)
        xs = pltpu.roll(x, 1, axis=0)
        rows = lax.broadcasted_iota(jnp.int32, x.shape, 0)
        return x, jnp.where(rows == 0, first, xs)

    def mix(x_ref, prev_ref, init_ref, mu_ref):
        x, xp = shifted(x_ref, prev_ref, init_ref)
        return x + (xp - x) * mu_ref[...]

    r = mix(r_ref, rp_ref, ri_ref if halo else None, mur_ref)
    k = mix(k_ref, kp_ref, ki_ref if halo else None, muk_ref)
    v = mix(v_ref, vp_ref, vi_ref if halo else None, muv_ref)
    lo = mix(l_ref, lp_ref, li_ref if halo else None, mul_ref)
    dl = w2_ref.shape[0]
    al = a2_ref.shape[0]
    xw = lo[:, :dl]
    xa = lo[:, dl:dl + al]
    xg = lo[:, dl + al:]
    wl = w0_ref[...] + _dg(jnp.tanh(xw), w2_ref[...], _NN, False)
    lw = -jnp.exp(-_softplus(-wl) - 0.5)
    a = _sigmoid(a0_ref[...] + _dg(xa, a2_ref[...], _NN, False))
    g = _dg(_sigmoid(xg), g2_ref[...], _NN, False)
    seg = _segsum_matrix(tc, rh)
    kk = k * kkw_ref[...]
    ss = _dg(kk * kk, seg, _NN, exact)
    kk = kk / jnp.maximum(jnp.sqrt(ss), 1e-12)
    kf = k * (1.0 + (a - 1.0) * kaw_ref[...])
    b = kk * a
    for pi in range(tc // LANES):
        sl = slice(pi * LANES, (pi + 1) * LANES)
        ro_ref[pi] = r[:, sl]
        lwo_ref[pi] = lw[:, sl]
        ko_ref[pi] = kf[:, sl]
        vo_ref[pi] = v[:, sl]
        kko_ref[pi] = kk[:, sl]
        bo_ref[pi] = b[:, sl]
        go_ref[pi] = g[:, sl]


def _rwkv_prep(rkv, lora, prev, mu_rkv, mu_lora, w0, w2, a0, a2, g2p, k_k, k_a, *, rh, exact, tt=256, tc=512):
    T, W3 = rkv.shape
    W = W3 // 3
    LP = lora.shape[1]
    tc = _pick(W, tc)
    nj = W // tc
    halo = prev[0].shape[0] != T or T == 1
    tt = _pick(T, tt, SUBLANES)
    hb = tt // SUBLANES
    xblk = lambda off: pl.BlockSpec((tt, tc), lambda i, j: (i, off * nj + j))
    pblk = lambda off: pl.BlockSpec((SUBLANES, tc), lambda i, j: (jnp.maximum(i * hb - 1, 0), off * nj + j))
    cvec = lambda off: pl.BlockSpec((1, tc), lambda i, j: (0, off * nj + j))
    in_specs = [xblk(0), xblk(1), xblk(2), pl.BlockSpec((tt, LP), lambda i, j: (i, 0))]
    args = [rkv, rkv, rkv, lora]
    if halo:
        in_specs += [pblk(0), pblk(1), pblk(2),
                     pl.BlockSpec((SUBLANES, LP), lambda i, j: (jnp.maximum(i * hb - 1, 0), 0)),
                     cvec(0), cvec(1), cvec(2), pl.BlockSpec((1, LP), lambda i, j: (0, 0))]
        args += [rkv, rkv, rkv, lora, prev[0], prev[0], prev[0], prev[1]]
    else:
        in_specs += [xblk(0), xblk(1), xblk(2), pl.BlockSpec((tt, LP), lambda i, j: (i, 0))]
        args += [prev[0], prev[0], prev[0], prev[1]]
    in_specs += [cvec(0), cvec(1), cvec(2), pl.BlockSpec((1, LP), lambda i, j: (0, 0)),
                 cvec(0), pl.BlockSpec((w2.shape[0], tc), lambda i, j: (0, j)),
                 cvec(0), pl.BlockSpec((a2.shape[0], tc), lambda i, j: (0, j)),
                 pl.BlockSpec((g2p.shape[0], tc), lambda i, j: (0, j)),
                 cvec(0), cvec(0)]
    args += [mu_rkv, mu_rkv, mu_rkv, mu_lora, w0.reshape(1, W), w2, a0.reshape(1, W), a2, g2p,
             k_k.reshape(1, W), k_a.reshape(1, W)]
    npair = tc // LANES
    oshape = jax.ShapeDtypeStruct((W // LANES, T, LANES), F32)
    ospec = pl.BlockSpec((npair, tt, LANES), lambda i, j: (j, i, 0))
    return pl.pallas_call(
        functools.partial(_prep_kernel, halo=halo, rh=rh, exact=exact, tc=tc),
        grid=(T // tt, nj),
        in_specs=in_specs,
        out_specs=[ospec] * 7,
        out_shape=[oshape] * 7,
        compiler_params=_cparams("parallel", "arbitrary"),
        name="rwkv_prep",
    )(*args)


def _scan_kernel(r_ref, lw_ref, k_ref, v_ref, kk_ref, b_ref, g_ref, s0_ref, rk_ref, lnw_ref, lnb_ref,
                 y_ref, sout_ref, s_sc, *, C, rh, npair, precise, t_real):
    c = pl.program_id(1)
    R = 2 * C
    lane = lax.broadcasted_iota(jnp.int32, (C, LANES), 1)
    head0 = lane < rh
    ri = lax.broadcasted_iota(jnp.int32, (R, R), 0)
    ci = lax.broadcasted_iota(jnp.int32, (R, R), 1)
    same = (ri // C) == (ci // C)
    strict = jnp.logical_and(same, (ci % C) < (ri % C))
    incl = jnp.logical_and(same, (ci % C) <= (ri % C))
    tril_c = (lax.broadcasted_iota(jnp.int32, (C, C), 1)
              <= lax.broadcasted_iota(jnp.int32, (C, C), 0)).astype(F32)
    seg = _segsum_matrix(LANES, rh)
    same_head = seg > 0.5
    zero = jnp.zeros((rh, rh), F32)

    def stack_masked(x):
        return jnp.concatenate([jnp.where(head0, x, 0.0), jnp.where(head0, 0.0, x)], axis=0)

    def stack_dup(x):
        return jnp.concatenate([x, x], axis=0)

    @pl.when(c == 0)
    def _():
        def init(p, carry):
            top = jnp.concatenate([s0_ref[0, 2 * p], zero], axis=1)
            bot = jnp.concatenate([zero, s0_ref[0, 2 * p + 1]], axis=1)
            s_sc[p] = jnp.concatenate([top, bot], axis=0)
            return carry
        lax.fori_loop(0, npair, init, 0)

    def single_step(r, lw, k, v, kk, b, S):
        sa = yield -kk, S, _NT, False
        o1 = yield sa, b, _TN, True
        o2 = yield v, k, _TN, True
        s_new = S * jnp.exp(lw[0:1, :]) + jnp.where(same_head, o1 + o2, 0.0)
        y = yield r, s_new, _NT, False
        return y, s_new

    def chunk_steps(r, lw, k, v, kk, b, S):
        cum = yield tril_c, lw, _NN, True
        tot = cum[C - 1:C, :]
        e_incl = jnp.exp(cum)
        e_inv = jnp.exp(-cum)
        e_end = jnp.exp(tot - cum)
        a2 = stack_masked(-kk * jnp.exp(cum - lw))
        r2 = stack_masked(r * e_incl)
        v2 = stack_masked(v)
        bd2 = stack_masked(b * e_end)
        kd2 = stack_masked(k * e_end)
        b2 = stack_dup(b * e_inv)
        k2 = stack_dup(k * e_inv)
        ar = jnp.concatenate([a2, r2], axis=0)
        arb = yield ar, b2, _NT, precise
        ark = yield ar, k2, _NT, precise
        ars = yield ar, S, _NT, precise
        mrb = jnp.where(incl, arb[R:], 0.0)
        mrk = jnp.where(incl, ark[R:], 0.0)
        x = ars[:R]
        n_it = (t_real - 1).bit_length()
        if n_it:
            x = x + (yield jnp.where(strict, ark[:R], 0.0), v2, _NN, precise)
            pw = jnp.where(strict, arb[:R], 0.0)
        for it in range(n_it):
            x = x + (yield pw, x, _NN, precise)
            if it + 1 < n_it:
                pw = yield pw, pw, _NN, precise
        y2 = ars[R:] + (yield mrb, x, _NN, precise)
        y2 = y2 + (yield mrk, v2, _NN, precise)
        y = y2[:C] + y2[C:]
        s_new = S * jnp.exp(tot) + (yield x, bd2, _TN, precise)
        s_new = s_new + (yield v2, kd2, _TN, precise)
        return y, s_new

    def slab(r, lw, k, v, kk, b, g, S, rk, lnw, lnb):
        steps = single_step if t_real == 1 else chunk_steps
        y, s_new = yield from steps(r, lw, k, v, kk, b, S)
        mean = (yield y, seg, _NN, precise) * (1.0 / rh)
        d = y - mean
        var = (yield d * d, seg, _NN, precise) * (1.0 / rh)
        yn = d * lax.rsqrt(var + GN_EPS) * lnw + lnb
        bonus = (yield r * k * rk, seg, _NN, precise) * v
        return (yn + bonus) * g, s_new

    def run_lockstep(gens):
        reqs = [next(gen) for gen in gens]
        results = [None] * len(gens)
        live = list(range(len(gens)))
        while live:
            vals = [_dg(*reqs[i]) for i in live]
            nxt = []
            for i, val in zip(live, vals):
                try:
                    reqs[i] = gens[i].send(val)
                    nxt.append(i)
                except StopIteration as done:
                    results[i] = done.value
            live = nxt
        return results

    U = math.gcd(npair, SLABS_PER_GROUP)

    def body(i, carry):
        grp = pl.ds(pl.multiple_of(i * U, U), U)
        ins = [ref[grp] for ref in (r_ref, lw_ref, k_ref, v_ref, kk_ref, b_ref, g_ref, s_sc,
                                    rk_ref, lnw_ref, lnb_ref)]
        outs = run_lockstep([slab(*[a[u] for a in ins]) for u in range(U)])
        y_ref[grp] = jnp.stack([o[0] for o in outs]).astype(y_ref.dtype)
        s_sc[grp] = jnp.stack([o[1] for o in outs])
        return carry

    lax.fori_loop(0, npair // U, body, 0)

    @pl.when(c == pl.num_programs(1) - 1)
    def _():
        def fin(p, carry):
            S = s_sc[p]
            sout_ref[0, 2 * p] = S[:rh, :rh]
            sout_ref[0, 2 * p + 1] = S[rh:, rh:]
            return carry
        lax.fori_loop(0, npair, fin, 0)


def _rwkv_scan(prep, s0, r_k, ln_w, ln_b, *, B, C, rh, precise, out_dtype, t_real):
    npair, BT, _ = prep[0].shape
    Tb = BT // B
    nc = Tb // C
    H = s0.shape[1]
    xspec = pl.BlockSpec((npair, C, LANES), lambda b, c: (0, b * nc + c, 0))
    sspec = pl.BlockSpec((1, H, rh, rh), lambda b, c: (b, 0, 0, 0))
    pspec = pl.BlockSpec((npair, 1, LANES), lambda b, c: (0, 0, 0))
    y, s_out = pl.pallas_call(
        functools.partial(_scan_kernel, C=C, rh=rh, npair=npair, precise=precise, t_real=t_real),
        grid=(B, nc),
        in_specs=[xspec] * 7 + [sspec, pspec, pspec, pspec],
        out_specs=[xspec, sspec],
        out_shape=[jax.ShapeDtypeStruct((npair, BT, LANES), out_dtype),
                   jax.ShapeDtypeStruct((B, H, rh, rh), F32)],
        scratch_shapes=[pltpu.VMEM((npair, 2 * rh, 2 * rh), F32)],
        compiler_params=_cparams("parallel", "arbitrary"),
        name="rwkv_scan",
    )(*prep, s0, r_k.reshape(npair, 1, LANES), ln_w.reshape(npair, 1, LANES), ln_b.reshape(npair, 1, LANES))
    return y, s_out


def _router_kernel(x_ref, nw_ref, rw_ref, rb_ref, h_ref, lg_ref, *, precise):
    x = x_ref[...]
    ms = jnp.mean(x * x, axis=-1, keepdims=True)
    h = x * lax.rsqrt(ms + NORM_EPS) * nw_ref[...]
    h_ref[...] = h.astype(h_ref.dtype)
    lg_ref[...] = _dg(h, rw_ref[...], _NN, precise) + rb_ref[...]


def _router(x, norm_w, rw, rb, *, precise, h_dtype, tm=512):
    M, D = x.shape
    NR = rw.shape[1]
    tm = _pick(M, tm, SUBLANES)
    return pl.pallas_call(
        functools.partial(_router_kernel, precise=precise),
        grid=(M // tm,),
        in_specs=[pl.BlockSpec((tm, D), lambda i: (i, 0)), pl.BlockSpec((1, D), lambda i: (0, 0)),
                  pl.BlockSpec((D, NR), lambda i: (0, 0)), pl.BlockSpec((1, NR), lambda i: (0, 0))],
        out_specs=[pl.BlockSpec((tm, D), lambda i: (i, 0)), pl.BlockSpec((tm, NR), lambda i: (i, 0))],
        out_shape=[jax.ShapeDtypeStruct((M, D), h_dtype), jax.ShapeDtypeStruct((M, NR), F32)],
        compiler_params=_cparams("parallel"),
        name="moe_router",
    )(x, norm_w.reshape(1, D), rw, rb)


def _moe_kernel(be_ref, nu_ref, x_ref, wg_ref, wu_ref, wd_ref, sw_ref, o_ref):
    b = pl.program_id(0)
    j = pl.program_id(1)
    last = j == pl.num_programs(1) - 1
    used = b < nu_ref[0]

    @pl.when(jnp.logical_and(used, j == 0))
    def _():
        o_ref[...] = jnp.zeros(o_ref.shape, F32)

    @pl.when(used)
    def _():
        x = x_ref[...]
        tn = wg_ref.shape[2]
        wgu = jnp.concatenate([wg_ref[0].astype(BF16), wu_ref[0].astype(BF16)], axis=1)
        hgu = jnp.dot(x, wgu, preferred_element_type=F32)
        hg, hu = hgu[:, :tn], hgu[:, tn:]
        h = hg * _sigmoid(hg) * hu
        o_ref[...] += jnp.dot(h.astype(BF16), wd_ref[0].astype(BF16), preferred_element_type=F32)

    @pl.when(jnp.logical_and(used, last))
    def _():
        o_ref[...] = o_ref[...] * sw_ref[...]

    @pl.when(jnp.logical_and(jnp.logical_not(used), last))
    def _():
        o_ref[...] = jnp.zeros(o_ref.shape, F32)


def _moe_experts(xs, blk_e, n_used, slot_w, wg, wu, wd, *, blk, tn):
    P, D = xs.shape
    DE = wg.shape[2]
    tn = _pick(DE, tn)
    nj = DE // tn
    nb = P // blk

    def eff(b, j, nu):
        live = b < nu[0]
        return jnp.where(live, b, nu[0] - 1), jnp.where(live, j, nj - 1)

    def wg_map(b, j, be, nu):
        bb, jj = eff(b, j, nu)
        return be[bb], 0, jj

    def wd_map(b, j, be, nu):
        bb, jj = eff(b, j, nu)
        return be[bb], jj, 0

    def x_map(b, j, be, nu):
        return eff(b, j, nu)[0], 0

    grid_spec = pltpu.PrefetchScalarGridSpec(
        num_scalar_prefetch=2,
        grid=(nb, nj),
        in_specs=[pl.BlockSpec((blk, D), x_map),
                  pl.BlockSpec((1, D, tn), wg_map), pl.BlockSpec((1, D, tn), wg_map),
                  pl.BlockSpec((1, tn, D), wd_map),
                  pl.BlockSpec((blk, 1), x_map)],
        out_specs=pl.BlockSpec((blk, D), lambda b, j, be, nu: (b, 0)),
    )
    return pl.pallas_call(
        _moe_kernel,
        grid_spec=grid_spec,
        out_shape=jax.ShapeDtypeStruct((P, D), F32),
        compiler_params=_cparams("arbitrary", "arbitrary"),
        name="moe_experts",
    )(blk_e, n_used, xs, wg, wu, wd, slot_w.reshape(P, 1))


def _hier_moe(x_groups, norm_w, rg_w, rg_b, re_w, re_b, wg, wu, wd, *, blk, tn):
    D = x_groups[0].shape[1]
    NG = rg_w.shape[1]
    E = re_w.shape[1]
    EPG = E // NG
    NR = -(-(NG + E) // LANES) * LANES
    rw = jnp.pad(jnp.concatenate([rg_w, re_w], axis=1), ((0, 0), (0, NR - NG - E))).astype(BF16)
    rb = jnp.pad(jnp.concatenate([rg_b, re_b]), (0, NR - NG - E)).reshape(1, NR)
    routed = [_router(x, norm_w, rw, rb, precise=False, h_dtype=BF16) for x in x_groups]
    logits = jnp.concatenate([r[1] for r in routed], axis=0)
    T = logits.shape[0]
    gprob = jax.nn.softmax(logits[:, :NG], axis=-1)
    gsel = jnp.argmax(gprob, axis=-1)
    gp = jnp.take_along_axis(gprob, gsel[:, None], axis=-1)[:, 0]
    elog = logits[:, NG:NG + E].reshape(T, NG, EPG)
    elog = jnp.take_along_axis(elog, gsel[:, None, None], axis=1)[:, 0]
    top_p, top_i = lax.top_k(jax.nn.softmax(elog, axis=-1), TOP_K)
    top_p = top_p / jnp.sum(top_p, axis=-1, keepdims=True)
    weights = gp[:, None] * top_p
    expert_idx = gsel[:, None].astype(jnp.int32) * EPG + top_i.astype(jnp.int32)

    TK = T * TOP_K
    nb = -(-TK // blk) + E
    P = nb * blk
    flat_e = expert_idx.reshape(-1)
    flat_w = weights.reshape(-1)
    flat_t = jnp.repeat(jnp.arange(T, dtype=jnp.int32), TOP_K)
    order = jnp.argsort(flat_e)
    se = flat_e[order]
    counts = jnp.bincount(flat_e, length=E)
    pcounts = (counts + blk - 1) // blk * blk
    pend = jnp.cumsum(pcounts)
    pstart = pend - pcounts
    start = jnp.cumsum(counts) - counts
    dest = (pstart[se] + jnp.arange(TK) - start[se]).astype(jnp.int32)
    slot_t = jnp.full((P,), T, jnp.int32).at[dest].set(flat_t[order])
    slot_w = jnp.zeros((P,), F32).at[dest].set(flat_w[order])
    blk_e = jnp.minimum(jnp.searchsorted(pend, jnp.arange(nb) * blk, side='right'), E - 1).astype(jnp.int32)
    n_used = (pend[-1] // blk).astype(jnp.int32).reshape(1)
    pos = jnp.zeros((TK,), jnp.int32).at[order].set(dest)

    xs = jnp.concatenate([r[0] for r in routed] + [jnp.zeros((1, D), BF16)], axis=0)[slot_t]
    ys = _moe_experts(xs, blk_e, n_used, slot_w, wg, wu, wd, blk=blk, tn=tn)
    pos = pos.reshape(T, TOP_K)
    outs, t0 = [], 0
    for x in x_groups:
        pg = pos[t0:t0 + x.shape[0]]
        outs.append(x + ys[pg[:, 0]] + ys[pg[:, 1]])
        t0 += x.shape[0]
    return outs


def _pairs_to_rows(y):
    npair, T, _ = y.shape
    return jnp.transpose(y, (1, 0, 2)).reshape(T, npair * LANES)


def _mixer(x, positions, prev_rows, s0, B, cfg, w, *, exact, attn_fn, q_scale, emit_bf16):
    T, D = x.shape
    hd, HK, G, vd, rh = cfg['hd'], cfg['HK'], cfg['G'], cfg['vd'], cfg['rh']
    qcols, kcols, vcols = HK * G * 2 * hd, HK * 2 * hd, HK * vd
    W = cfg['W']
    mm = functools.partial(_matmul, precise=False)

    h = _rmsnorm(x, w['norm1_w'], BF16)
    qkv = mm(h, w['w_qkv'], qcols + kcols + vcols)
    rkv = mm(h, w['w_rkv'], 3 * W)
    lora = mm(h, w['w_lora'], cfg['lp'], tn=cfg['lp'])
    gates = mm(h, w['w_gates'], 2 * D)

    cos_full, sin_signed = _rope_tables(positions, hd)
    outs = _qk_rope(qkv, cos_full, sin_signed, w['q_norm_w'], w['k_norm_w'], qcols, kcols, vcols,
                    q_scale=q_scale, q_dtype=BF16, emit_bf16=emit_bf16)
    q, k_new, v_new = outs[:3]
    o_attn = attn_fn(q, outs)

    prep = _rwkv_prep(rkv, lora, prev_rows, w['mu_rkv'], w['mu_lora'], w['rwkv_w0'], w['rwkv_w2'],
                      w['rwkv_a0'], w['rwkv_a2'], w['g2p'], w['rwkv_k_k'], w['rwkv_k_a'],
                      rh=rh, exact=exact)
    Tb = T // B
    if Tb % 64 == 0:
        C = 64
    else:
        C = SUBLANES
        pad = -Tb % C
        prep = [jnp.pad(a.reshape(-1, B, Tb, LANES), ((0, 0), (0, 0), (0, pad), (0, 0)))
                .reshape(a.shape[0], B * (Tb + pad), LANES) for a in prep]
    y_r, s_fin = _rwkv_scan(prep, s0, w['rwkv_r_k'], w['rwkv_ln_w'], w['rwkv_ln_b'],
                            B=B, C=C, rh=rh, precise=exact, out_dtype=BF16, t_real=min(Tb, C))
    if Tb % 64:
        y_r = y_r.reshape(y_r.shape[0], B, -1, LANES)[:, :, :Tb].reshape(y_r.shape[0], T, LANES)
    o_rwkv = _pairs_to_rows(y_r)

    mixed = _merge(o_attn, w['w_branch_attn'], o_rwkv, w['w_branch_rwkv'], gates, w['gate_b'],
                   precise=False, out_dtype=BF16)
    x1 = _matmul_residual(mixed, w['w_out'], x, precise=False)
    return x1, k_new, v_new, s_fin, rkv, lora


def kernel(x_prompt, x_sample, cache_k, cache_v, state_rwkv, state_shift, page_table, norm1_w, w_in, gate_b, q_norm_w, k_norm_w, lambda_q1, lambda_k1, lambda_q2, lambda_k2, subln_w, rwkv_mu, rwkv_w0, rwkv_w2, rwkv_a0, rwkv_a2, rwkv_g2, rwkv_k_k, rwkv_k_a, rwkv_r_k, rwkv_ln_w, rwkv_ln_b, w_branch_attn, w_branch_rwkv, w_out, norm2_w, router_group_w, router_group_b, router_expert_w, router_expert_b, expert_w_gate, expert_w_up, expert_w_down):
    B, T, D = x_prompt.shape
    DB, S, _ = x_sample.shape
    L = norm1_w.shape[0]
    assert L == 1 and B == 1 and S == 1, "one trunk layer, one prompt sequence, one new token per sample"
    hd = q_norm_w.shape[1]
    HK = cache_k.shape[3]
    vd = cache_v.shape[4]
    G = w_branch_attn.shape[1] // vd // HK
    H, rh = rwkv_r_k.shape[1], rwkv_r_k.shape[2]
    W = H * rh
    dl, al, gl = rwkv_w2.shape[1], rwkv_a2.shape[1], rwkv_g2.shape[1]
    lcols = dl + al + gl
    lp = -(-lcols // LANES) * LANES
    qkv_cols = HK * G * 2 * hd + HK * 2 * hd + HK * vd
    r_off = qkv_cols
    g_off = r_off + 3 * W + lcols
    n_pages, page = page_table.shape[1], cache_k.shape[2]
    past = n_pages * page
    cfg = dict(hd=hd, HK=HK, G=G, vd=vd, rh=rh, W=W, lp=lp, r_off=r_off)
    l = 0
    (cache_k, cache_v, state_rwkv, state_shift, norm1_w, w_in, gate_b, q_norm_w, k_norm_w, lambda_q1,
     lambda_k1, lambda_q2, lambda_k2, subln_w, rwkv_mu, rwkv_w0, rwkv_w2, rwkv_a0, rwkv_a2, rwkv_g2,
     rwkv_k_k, rwkv_k_a, rwkv_r_k, rwkv_ln_w, rwkv_ln_b, w_branch_attn, w_branch_rwkv, w_out, norm2_w,
     router_group_w, router_group_b, router_expert_w, router_expert_b, expert_w_gate, expert_w_up,
     expert_w_down) = [a.reshape(a.shape[1:]) for a in (
         cache_k, cache_v, state_rwkv, state_shift, norm1_w, w_in, gate_b, q_norm_w, k_norm_w, lambda_q1,
         lambda_k1, lambda_q2, lambda_k2, subln_w, rwkv_mu, rwkv_w0, rwkv_w2, rwkv_a0, rwkv_a2, rwkv_g2,
         rwkv_k_k, rwkv_k_a, rwkv_r_k, rwkv_ln_w, rwkv_ln_b, w_branch_attn, w_branch_rwkv, w_out, norm2_w,
         router_group_w, router_group_b, router_expert_w, router_expert_b, expert_w_gate, expert_w_up,
         expert_w_down)]
    lam_init = 0.8 - 0.6 * math.exp(-0.3 * l)
    lams = (lambda_q1, lambda_k1, lambda_q2, lambda_k2)

    mu = rwkv_mu
    g2p = jnp.pad(rwkv_g2, ((0, lp - lcols), (0, 0)))
    w_lora = jnp.pad(w_in[:, r_off + 3 * W:r_off + 3 * W + lcols], ((0, 0), (0, lp - lcols)))
    w = dict(
        norm1_w=norm1_w, q_norm_w=q_norm_w, k_norm_w=k_norm_w, gate_b=gate_b,
        mu_rkv=mu[:3 * W].reshape(1, 3 * W), mu_lora=jnp.pad(mu[3 * W:], (0, lp - lcols)).reshape(1, lp),
        rwkv_w0=rwkv_w0, rwkv_a0=rwkv_a0, rwkv_k_k=rwkv_k_k, rwkv_k_a=rwkv_k_a,
        rwkv_r_k=rwkv_r_k, rwkv_ln_w=rwkv_ln_w, rwkv_ln_b=rwkv_ln_b,
        norm2_w=norm2_w, router_group_w=router_group_w, router_group_b=router_group_b,
        router_expert_w=router_expert_w, router_expert_b=router_expert_b,
        expert_w_gate=expert_w_gate, expert_w_up=expert_w_up, expert_w_down=expert_w_down,
        w_qkv=w_in[:, :qkv_cols].astype(BF16), w_rkv=w_in[:, r_off:r_off + 3 * W].astype(BF16),
        w_lora=w_lora.astype(BF16), w_gates=w_in[:, g_off:g_off + 2 * D].astype(BF16),
        rwkv_w2=rwkv_w2.astype(BF16), rwkv_a2=rwkv_a2.astype(BF16), g2p=g2p.astype(BF16),
        w_branch_attn=w_branch_attn.astype(BF16), w_branch_rwkv=w_branch_rwkv.astype(BF16),
        w_out=w_out.astype(BF16))

    def prompt_attn(q, outs):
        return _flash_diff_attn(q, outs[3], outs[4], lams, subln_w, HK=HK, G=G, hd=hd, vd=vd,
                                lam_init=lam_init, out_dtype=BF16)

    zero_prev = (jnp.zeros((1, 3 * W), F32), jnp.zeros((1, lp), F32))
    xp1, kp, vp, sp, rkv_p, lora_p = _mixer(
        x_prompt.reshape(T, D), jnp.arange(T), zero_prev, jnp.zeros((1, H, rh, rh), F32), 1, cfg, w,
        exact=False, attn_fn=prompt_attn, q_scale=hd ** -0.5 * math.log2(math.e), emit_bf16=True)
    shift_p = jnp.concatenate([rkv_p[T - 1], lora_p[T - 1, :lcols]])

    def sample_attn(q, outs):
        return _decode_diff_attn(q, outs[1], outs[2], cache_k, cache_v, page_table, lams, subln_w,
                                 HK=HK, G=G, hd=hd, vd=vd, lam_init=lam_init)

    sh = state_shift
    prev_s = (sh[:, :3 * W], jnp.pad(sh[:, 3 * W:], ((0, 0), (0, lp - lcols))))
    xs1, ks, vs, ss, rkv_s, lora_s = _mixer(
        x_sample.reshape(DB, D), jnp.full((DB,), past, jnp.int32), prev_s, state_rwkv, DB, cfg, w,
        exact=True, attn_fn=sample_attn, q_scale=1.0, emit_bf16=False)
    shift_s = jnp.concatenate([rkv_s, lora_s[:, :lcols]], axis=1)

    yp, ys = _hier_moe([xp1, xs1], norm2_w, router_group_w, router_group_b, router_expert_w,
                       router_expert_b, expert_w_gate, expert_w_up, expert_w_down,
                       blk=min(MOE_ROW_BLOCK, max(2 * SUBLANES, T // 16)), tn=MOE_HIDDEN_TILE)

    return (yp.reshape(1, T, D), ys.reshape(DB, 1, D),
            kp.reshape(1, 1, T, HK, 2, hd), vp.reshape(1, 1, T, HK, vd),
            sp.reshape(1, 1, H, rh, rh), shift_p.reshape(1, 1, -1),
            ks.reshape(1, DB, 1, HK, 2, hd), vs.reshape(1, DB, 1, HK, vd),
            ss.reshape(1, DB, H, rh, rh), shift_s.reshape(1, DB, -1))
```

```python
import functools
import math

import jax
import jax.numpy as jnp
from jax import lax
from jax.experimental import pallas as pl
from jax.experimental.pallas import tpu as pltpu

F32 = jnp.float32
BF16 = jnp.bfloat16
HIGHEST = lax.Precision.HIGHEST

LANES = 128
SUBLANES = 8
VMEM_LIMIT = 56 * 1024 * 1024

ROPE_THETA = 10000.0
NORM_EPS = 1e-6
GN_EPS = 64e-5
TOP_K = 2
SLABS_PER_GROUP = 16
DECODE_PAGES_PER_STEP = 8
MOE_ROW_BLOCK = 512
MOE_HIDDEN_TILE = 256

_NN = (((1,), (0,)), ((), ()))
_NT = (((1,), (1,)), ((), ()))
_TN = (((0,), (0,)), ((), ()))


def _cparams(*sem):
    return pltpu.CompilerParams(dimension_semantics=sem, vmem_limit_bytes=VMEM_LIMIT)


def _pick(n, pref, unit=LANES):
    if n <= pref:
        return n
    t = pref // unit * unit
    while t > unit and n % t:
        t -= unit
    assert n % t == 0, (n, pref)
    return t


def _dg(a, b, dn, precise):
    if precise:
        return lax.dot_general(a.astype(F32), b.astype(F32), dn, precision=HIGHEST,
                               preferred_element_type=F32)
    return lax.dot_general(a.astype(BF16), b.astype(BF16), dn, preferred_element_type=F32)


def _sigmoid(x):
    return 1.0 / (1.0 + jnp.exp(-x))


def _softplus(x):
    return jnp.maximum(x, 0.0) + jnp.log(1.0 + jnp.exp(-jnp.abs(x)))


def _rms_kernel(x_ref, w_ref, o_ref, *, eps):
    x = x_ref[...]
    ms = jnp.mean(x * x, axis=-1, keepdims=True)
    o_ref[...] = (x * lax.rsqrt(ms + eps) * w_ref[...]).astype(o_ref.dtype)


def _rmsnorm(x, w, out_dtype, tm=512):
    M, D = x.shape
    tm = _pick(M, tm, SUBLANES)
    return pl.pallas_call(
        functools.partial(_rms_kernel, eps=NORM_EPS),
        grid=(M // tm,),
        in_specs=[pl.BlockSpec((tm, D), lambda i: (i, 0)),
                  pl.BlockSpec((1, D), lambda i: (0, 0))],
        out_specs=pl.BlockSpec((tm, D), lambda i: (i, 0)),
        out_shape=jax.ShapeDtypeStruct((M, D), out_dtype),
        compiler_params=_cparams("parallel"),
        name="rmsnorm",
    )(x, w.reshape(1, D))


def _mm_kernel(a_ref, b_ref, o_ref, *, precise):
    o_ref[...] = _dg(a_ref[...], b_ref[...], _NN, precise).astype(o_ref.dtype)


def _matmul(a, b, n, col0=0, *, precise, out_dtype=F32, tm=1024, tn=512):
    M, K = a.shape
    tm = _pick(M, tm, SUBLANES)
    tn = _pick(n, tn)
    if col0 % tn:
        b, col0 = b[:, col0:col0 + n], 0
    c0 = col0 // tn
    return pl.pallas_call(
        functools.partial(_mm_kernel, precise=precise),
        grid=(M // tm, n // tn),
        in_specs=[pl.BlockSpec((tm, K), lambda i, j: (i, 0)),
                  pl.BlockSpec((K, tn), lambda i, j: (0, c0 + j))],
        out_specs=pl.BlockSpec((tm, tn), lambda i, j: (i, j)),
        out_shape=jax.ShapeDtypeStruct((M, n), out_dtype),
        compiler_params=_cparams("parallel", "arbitrary"),
        name="matmul",
    )(a, b)


def _merge_kernel(a1_ref, b1_ref, a2_ref, b2_ref, g1_ref, g2_ref, gb1_ref, gb2_ref, o_ref, *, precise):
    o1 = _dg(a1_ref[...], b1_ref[...], _NN, precise)
    o2 = _dg(a2_ref[...], b2_ref[...], _NN, precise)
    s1 = _sigmoid(g1_ref[...] + gb1_ref[...])
    s2 = _sigmoid(g2_ref[...] + gb2_ref[...])
    o_ref[...] = (s1 * o1 + s2 * o2).astype(o_ref.dtype)


def _merge(a1, b1, a2, b2, gates, gate_b, *, precise, out_dtype, tm=512, tn=512):
    M, K = a1.shape
    K2 = a2.shape[1]
    D = b1.shape[1]
    tm = _pick(M, tm, SUBLANES)
    tn = _pick(D, tn)
    nj = D // tn
    gb = gate_b.reshape(1, 2 * D)
    return pl.pallas_call(
        functools.partial(_merge_kernel, precise=precise),
        grid=(M // tm, nj),
        in_specs=[pl.BlockSpec((tm, K), lambda i, j: (i, 0)),
                  pl.BlockSpec((K, tn), lambda i, j: (0, j)),
                  pl.BlockSpec((tm, K2), lambda i, j: (i, 0)),
                  pl.BlockSpec((K2, tn), lambda i, j: (0, j)),
                  pl.BlockSpec((tm, tn), lambda i, j: (i, j)),
                  pl.BlockSpec((tm, tn), lambda i, j: (i, nj + j)),
                  pl.BlockSpec((1, tn), lambda i, j: (0, j)),
                  pl.BlockSpec((1, tn), lambda i, j: (0, nj + j))],
        out_specs=pl.BlockSpec((tm, tn), lambda i, j: (i, j)),
        out_shape=jax.ShapeDtypeStruct((M, D), out_dtype),
        compiler_params=_cparams("parallel", "arbitrary"),
        name="merge",
    )(a1, b1, a2, b2, gates, gates, gb, gb)


def _mm_res_kernel(a_ref, b_ref, r_ref, o_ref, *, precise):
    o_ref[...] = r_ref[...] + _dg(a_ref[...], b_ref[...], _NN, precise)


def _matmul_residual(a, b, res, *, precise, tm=1024, tn=512):
    M, K = a.shape
    N = b.shape[1]
    tm = _pick(M, tm, SUBLANES)
    tn = _pick(N, tn)
    return pl.pallas_call(
        functools.partial(_mm_res_kernel, precise=precise),
        grid=(M // tm, N // tn),
        in_specs=[pl.BlockSpec((tm, K), lambda i, j: (i, 0)),
                  pl.BlockSpec((K, tn), lambda i, j: (0, j)),
                  pl.BlockSpec((tm, tn), lambda i, j: (i, j))],
        out_specs=pl.BlockSpec((tm, tn), lambda i, j: (i, j)),
        out_shape=jax.ShapeDtypeStruct((M, N), F32),
        compiler_params=_cparams("parallel", "arbitrary"),
        name="matmul_residual",
    )(a, b, res)


def _qkrope_kernel(qkv_ref, cos_ref, sin_ref, qw_ref, kw_ref, *out_refs, nq, nk, hd, vcols, scale, emit_bf16):
    if emit_bf16:
        q_ref, k_ref, v_ref, kb_ref, vb_ref = out_refs
    else:
        q_ref, k_ref, v_ref = out_refs
    cos = cos_ref[...]
    sin = sin_ref[...]

    def norm_rope(x, w):
        ms = jnp.mean(x * x, axis=-1, keepdims=True)
        xn = x * lax.rsqrt(ms + NORM_EPS) * w
        return xn * cos + pltpu.roll(xn, hd // 2, axis=1) * sin

    qw = qw_ref[...]
    kw = kw_ref[...]
    for c in range(nq):
        o = norm_rope(qkv_ref[:, c * hd:(c + 1) * hd], qw) * scale
        q_ref[:, c * hd:(c + 1) * hd] = o.astype(q_ref.dtype)
    for c in range(nk):
        o = norm_rope(qkv_ref[:, (nq + c) * hd:(nq + c + 1) * hd], kw)
        k_ref[:, c * hd:(c + 1) * hd] = o
        if emit_bf16:
            kb_ref[:, c * hd:(c + 1) * hd] = o.astype(BF16)
    v = qkv_ref[:, (nq + nk) * hd:(nq + nk) * hd + vcols]
    v_ref[...] = v
    if emit_bf16:
        vb_ref[...] = v.astype(BF16)


def _qk_rope(qkv, cos_full, sin_signed, q_norm_w, k_norm_w, qcols, kcols, vcols, *, q_scale, q_dtype,
             emit_bf16, tq=256):
    T = qkv.shape[0]
    hd = q_norm_w.shape[0]
    tq = _pick(T, tq, SUBLANES)
    nq, nk = qcols // hd, kcols // hd
    row = lambda i: (i, 0)
    out_shape = [jax.ShapeDtypeStruct((T, qcols), q_dtype),
                 jax.ShapeDtypeStruct((T, kcols), F32),
                 jax.ShapeDtypeStruct((T, vcols), F32)]
    out_specs = [pl.BlockSpec((tq, qcols), row), pl.BlockSpec((tq, kcols), row), pl.BlockSpec((tq, vcols), row)]
    if emit_bf16:
        out_shape += [jax.ShapeDtypeStruct((T, kcols), BF16), jax.ShapeDtypeStruct((T, vcols), BF16)]
        out_specs += [pl.BlockSpec((tq, kcols), row), pl.BlockSpec((tq, vcols), row)]
    return pl.pallas_call(
        functools.partial(_qkrope_kernel, nq=nq, nk=nk, hd=hd, vcols=vcols, scale=q_scale,
                          emit_bf16=emit_bf16),
        grid=(T // tq,),
        in_specs=[pl.BlockSpec((tq, qkv.shape[1]), row),
                  pl.BlockSpec((tq, hd), row), pl.BlockSpec((tq, hd), row),
                  pl.BlockSpec((1, hd), lambda i: (0, 0)), pl.BlockSpec((1, hd), lambda i: (0, 0))],
        out_specs=out_specs,
        out_shape=out_shape,
        compiler_params=_cparams("parallel"),
        name="qk_norm_rope",
    )(qkv, cos_full, sin_signed, q_norm_w.reshape(1, hd), k_norm_w.reshape(1, hd))


def _rope_tables(positions, hd):
    inv = 1.0 / (ROPE_THETA ** (jnp.arange(0, hd, 2, dtype=F32) / hd))
    ang = positions.astype(F32)[:, None] * inv[None, :]
    cos, sin = jnp.cos(ang), jnp.sin(ang)
    return jnp.concatenate([cos, cos], axis=-1), jnp.concatenate([-sin, sin], axis=-1)


def _diff_lambda_in_kernel(lq1_ref, lk1_ref, lq2_ref, lk2_ref, lam_init):
    e1 = jnp.exp(jnp.sum(lq1_ref[...] * lk1_ref[...], axis=-1, keepdims=True))
    e2 = jnp.exp(jnp.sum(lq2_ref[...] * lk2_ref[...], axis=-1, keepdims=True))
    return e1 - e2 + lam_init


def _attn_finalize(acc1, l1, acc2, l2, lam, subw, lam_init):
    o = acc1 / l1 - lam * (acc2 / l2)
    ms = jnp.mean(o * o, axis=-1, keepdims=True)
    return o * lax.rsqrt(ms + NORM_EPS) * subw * (1.0 - lam_init)


def _flash_kernel(qi_ref, ki_ref, q_ref, k_ref, v_ref, lq1_ref, lk1_ref, lq2_ref, lk2_ref, subw_ref, o_ref,
                  m_sc, l_sc, acc_sc, s_sc, *, G, hd, vd, tq, lam_init):
    qi = qi_ref[pl.program_id(1)]
    ki = ki_ref[pl.program_id(1)]

    @pl.when(ki == 0)
    def _():
        m_sc[...] = jnp.full(m_sc.shape, -jnp.inf, F32)
        l_sc[...] = jnp.zeros(l_sc.shape, F32)
        acc_sc[...] = jnp.zeros(acc_sc.shape, F32)

    def tile(diagonal):
        v = v_ref[...]
        for m in range(2):
            kb = k_ref[:, m * hd:(m + 1) * hd]
            for g in range(G):
                idx = g * 2 + m
                q = q_ref[:, idx * hd:(idx + 1) * hd]
                s = lax.dot_general(q, kb, _NT, preferred_element_type=F32)
                if diagonal:
                    rows = lax.broadcasted_iota(jnp.int32, (tq, tq), 0)
                    cols = lax.broadcasted_iota(jnp.int32, (tq, tq), 1)
                    s = jnp.where(cols <= rows, s, -jnp.inf)
                s_sc[...] = s
                m_prev = m_sc[idx]
                m_new = jnp.maximum(m_prev, jnp.max(s_sc[...], axis=-1, keepdims=True))
                m_sc[idx] = m_new
                alpha = jnp.exp2(m_prev - m_new)
                p = jnp.exp2(s_sc[...] - jnp.tile(m_new, (1, tq // LANES)))
                l_sc[idx] = alpha * l_sc[idx] + jnp.sum(p, axis=-1, keepdims=True)
                pv = jnp.dot(p.astype(BF16), v, preferred_element_type=F32)
                acc_sc[idx] = jnp.tile(alpha, (1, vd // LANES)) * acc_sc[idx] + pv

    @pl.when(ki < qi)
    def _():
        tile(False)

    @pl.when(ki == qi)
    def _():
        tile(True)
        lam = _diff_lambda_in_kernel(lq1_ref, lk1_ref, lq2_ref, lk2_ref, lam_init)
        subw = subw_ref[...]
        for g in range(G):
            o = _attn_finalize(acc_sc[2 * g], l_sc[2 * g][:, :1], acc_sc[2 * g + 1], l_sc[2 * g + 1][:, :1],
                               lam, subw, lam_init)
            o_ref[:, g * vd:(g + 1) * vd] = o.astype(o_ref.dtype)


def _flash_diff_attn(q, kb, vb, lams, subln_w, *, HK, G, hd, vd, lam_init, out_dtype, tq=512):
    T = q.shape[0]
    tq = _pick(T, tq)
    nq = T // tq
    vec = lambda n: pl.BlockSpec((1, n), lambda h, s, qt, kt: (0, 0))
    pairs = [(i, j) for i in range(nq) for j in range(i + 1)]
    qi_tab = jnp.asarray([p[0] for p in pairs], jnp.int32)
    ki_tab = jnp.asarray([p[1] for p in pairs], jnp.int32)
    grid_spec = pltpu.PrefetchScalarGridSpec(
        num_scalar_prefetch=2,
        grid=(HK, len(pairs)),
        in_specs=[pl.BlockSpec((tq, G * 2 * hd), lambda h, s, qt, kt: (qt[s], h)),
                  pl.BlockSpec((tq, 2 * hd), lambda h, s, qt, kt: (kt[s], h)),
                  pl.BlockSpec((tq, vd), lambda h, s, qt, kt: (kt[s], h)),
                  vec(hd), vec(hd), vec(hd), vec(hd), vec(vd)],
        out_specs=pl.BlockSpec((tq, G * vd), lambda h, s, qt, kt: (qt[s], h)),
        scratch_shapes=[pltpu.VMEM((2 * G, tq, LANES), F32), pltpu.VMEM((2 * G, tq, LANES), F32),
                        pltpu.VMEM((2 * G, tq, vd), F32), pltpu.VMEM((tq, tq), F32)],
    )
    return pl.pallas_call(
        functools.partial(_flash_kernel, G=G, hd=hd, vd=vd, tq=tq, lam_init=lam_init),
        grid_spec=grid_spec,
        out_shape=jax.ShapeDtypeStruct((T, HK * G * vd), out_dtype),
        compiler_params=_cparams("parallel", "arbitrary"),
        name="flash_diff_attn",
    )(qi_tab, ki_tab, q, kb, vb, *[x.reshape(1, -1) for x in lams], subln_w.reshape(1, vd))


def _decode_kernel(pt_ref, q_ref, kn_ref, vn_ref, *refs, HK, G, vd, pps, scale, lam_init):
    kc_refs, vc_refs = refs[:pps], refs[pps:2 * pps]
    (lq1_ref, lk1_ref, lq2_ref, lk2_ref, subw_ref, o_ref, s_sc, a_sc, anew_sc, acc_sc) = refs[2 * pps:]
    ph = pl.program_id(1)
    pg = pl.program_id(2)
    last = pl.num_programs(2) - 1
    NI = q_ref.shape[1]
    half = NI // 2
    q = q_ref[0]
    nk = q.shape[1] // LANES
    page = kc_refs[0].shape[0] // nk
    nv = vc_refs[0].shape[0] // page
    v_order = [et * HK + hk for hk in range(HK) for et in range(nv // HK)]

    def token_rows(ref, order):
        x = ref[...].reshape(page, len(order) * LANES)
        if list(order) != list(range(len(order))):
            x = jnp.concatenate([x[:, s * LANES:(s + 1) * LANES] for s in order], axis=1)
        return x.astype(BF16)

    row_hk = lax.broadcasted_iota(jnp.int32, (half, vd), 0) // G

    def own_head(x):
        out = x[:, :vd]
        for hk in range(1, HK):
            out = jnp.where(row_hk == hk, x[:, hk * vd:(hk + 1) * vd], out)
        return out

    @pl.when(ph == 0)
    def _():
        s_sc[pg] = scale * jnp.concatenate(
            [lax.dot_general(q, token_rows(kc, range(nk)), _NT, preferred_element_type=F32) for kc in kc_refs],
            axis=1)

    @pl.when(jnp.logical_and(ph == 0, pg == last))
    def _():
        lam = _diff_lambda_in_kernel(lq1_ref, lk1_ref, lq2_ref, lk2_ref, lam_init)
        s_new = scale * jnp.sum(q.astype(F32) * kn_ref[0].astype(BF16).astype(F32), axis=-1, keepdims=True)
        s = s_sc[...]
        m = jnp.maximum(jnp.max(jnp.max(s, axis=0), axis=-1, keepdims=True), s_new)
        e = jnp.exp(s - m[None])
        e_new = jnp.exp(s_new - m)
        denom = jnp.sum(jnp.sum(e, axis=0), axis=-1, keepdims=True) + e_new
        p = e / denom[None]
        p_new = e_new / denom
        a_sc[...] = (p[:, :half] - lam * p[:, half:]).astype(BF16)
        anew_sc[...] = jnp.broadcast_to(p_new[:half] - lam * p_new[half:], (half, LANES))
        acc_sc[...] = jnp.zeros(acc_sc.shape, F32)

    @pl.when(ph == 1)
    def _():
        a = a_sc[pg]
        pv = sum(jnp.dot(a[:, j * page:(j + 1) * page], token_rows(vc, v_order), preferred_element_type=F32)
                 for j, vc in enumerate(vc_refs))
        acc_sc[...] += own_head(pv)

    @pl.when(jnp.logical_and(ph == 1, pg == last))
    def _():
        a_new = anew_sc[...][:, :1]
        v_new = own_head(jnp.broadcast_to(vn_ref[0], (half, HK * vd)))
        o = acc_sc[...] + a_new * v_new
        ms = jnp.mean(o * o, axis=-1, keepdims=True)
        o_ref[0] = o * lax.rsqrt(ms + NORM_EPS) * subw_ref[...] * (1.0 - lam_init)


def _decode_diff_attn(q, k_new, v_new, cache_k, cache_v, page_table, lams, subln_w, *, HK, G, hd, vd, lam_init):
    DB, n_pages = page_table.shape
    n_phys, page = cache_k.shape[0], cache_k.shape[1]
    assert page % LANES == 0 and vd % LANES == 0 and hd == LANES
    kcw, vcw = HK * 2 * hd, HK * vd
    nk, nv = kcw // LANES, vcw // LANES
    kc = cache_k.reshape(n_phys * page * nk, LANES)
    vc = jnp.transpose(cache_v.reshape(n_phys, page, HK, vd // LANES, LANES),
                       (0, 1, 3, 2, 4)).reshape(n_phys * page * nv, LANES)
    NI = 2 * HK * G
    qt = jnp.transpose(q.reshape(DB, HK, G, 2, hd), (0, 3, 1, 2, 4))
    own = (jnp.eye(HK, dtype=q.dtype)[None, None, :, None, :, None, None]
           * jnp.eye(2, dtype=q.dtype)[None, :, None, None, None, :, None])
    qbd = (qt[:, :, :, :, None, None, :] * own).reshape(DB, NI, kcw)
    pps = math.gcd(n_pages, DECODE_PAGES_PER_STEP)
    steps = n_pages // pps
    per_b = lambda r, n: pl.BlockSpec((1, r, n), lambda b, ph, pg, pt: (b, 0, 0))
    vec = lambda n: pl.BlockSpec((1, n), lambda b, ph, pg, pt: (0, 0))

    def k_spec(j):
        return pl.BlockSpec((page * nk, LANES),
                            lambda b, ph, pg, pt: (pt[b, jnp.where(ph == 0, pg, steps - 1) * pps + j], 0))

    def v_spec(j):
        return pl.BlockSpec((page * nv, LANES),
                            lambda b, ph, pg, pt: (pt[b, jnp.where(ph == 1, pg, 0) * pps + j], 0))

    grid_spec = pltpu.PrefetchScalarGridSpec(
        num_scalar_prefetch=1,
        grid=(DB, 2, steps),
        in_specs=[per_b(NI, kcw), per_b(1, kcw), per_b(1, vcw)]
        + [k_spec(j) for j in range(pps)] + [v_spec(j) for j in range(pps)]
        + [vec(hd), vec(hd), vec(hd), vec(hd), vec(vd)],
        out_specs=pl.BlockSpec((1, NI // 2, vd), lambda b, ph, pg, pt: (b, 0, 0)),
        scratch_shapes=[pltpu.VMEM((steps, NI, pps * page), F32), pltpu.VMEM((steps, NI // 2, pps * page), BF16),
                        pltpu.VMEM((NI // 2, LANES), F32), pltpu.VMEM((NI // 2, vd), F32)],
    )
    out = pl.pallas_call(
        functools.partial(_decode_kernel, HK=HK, G=G, vd=vd, pps=pps, scale=hd ** -0.5, lam_init=lam_init),
        grid_spec=grid_spec,
        out_shape=jax.ShapeDtypeStruct((DB, NI // 2, vd), F32),
        compiler_params=_cparams("parallel", "arbitrary", "arbitrary"),
        name="decode_diff_attn",
    )(page_table, qbd, k_new.reshape(DB, 1, kcw), v_new.reshape(DB, 1, vcw), *([kc] * pps), *([vc] * pps),
      *[x.reshape(1, -1) for x in lams], subln_w.reshape(1, vd))
    return out.reshape(DB, HK * G * vd)


def _segsum_matrix(n, seg):
    r = lax.broadcasted_iota(jnp.int32, (n, n), 0) // seg
    c = lax.broadcasted_iota(jnp.int32, (n, n), 1) // seg
    return (r == c).astype(F32)


def _prep_kernel(*refs, halo, rh, exact, tc):
    if halo:
        (r_ref, k_ref, v_ref, l_ref, rp_ref, kp_ref, vp_ref, lp_ref, ri_ref, ki_ref, vi_ref, li_ref,
         mur_ref, muk_ref, muv_ref, mul_ref, w0_ref, w2_ref, a0_ref, a2_ref, g2_ref, kkw_ref, kaw_ref,
         ro_ref, lwo_ref, ko_ref, vo_ref, kko_ref, bo_ref, go_ref) = refs
    else:
        (r_ref, k_ref, v_ref, l_ref, rp_ref, kp_ref, vp_ref, lp_ref,
         mur_ref, muk_ref, muv_ref, mul_ref, w0_ref, w2_ref, a0_ref, a2_ref, g2_ref, kkw_ref, kaw_ref,
         ro_ref, lwo_ref, ko_ref, vo_ref, kko_ref, bo_ref, go_ref) = refs
    i = pl.program_id(0)

    def shifted(x_ref, prev_ref, init_ref):
        x = x_ref[...]
        if not halo:
            return x, prev_ref[...]
        last = prev_ref[SUBLANES - 1:SUBLANES, :]
        first = jnp.where(i == 0, init_ref[...], last)
        xs = pltpu.roll(x, 1, axis=0)
        rows = lax.broadcasted_iota(jnp.int32, x.shape, 0)
        return x, jnp.where(rows == 0, first, xs)

    def mix(x_ref, prev_ref, init_ref, mu_ref):
        x, xp = shifted(x_ref, prev_ref, init_ref)
        return x + (xp - x) * mu_ref[...]

    r = mix(r_ref, rp_ref, ri_ref if halo else None, mur_ref)
    k = mix(k_ref, kp_ref, ki_ref if halo else None, muk_ref)
    v = mix(v_ref, vp_ref, vi_ref if halo else None, muv_ref)
    lo = mix(l_ref, lp_ref, li_ref if halo else None, mul_ref)
    dl = w2_ref.shape[0]
    al = a2_ref.shape[0]
    xw = lo[:, :dl]
    xa = lo[:, dl:dl + al]
    xg = lo[:, dl + al:]
    wl = w0_ref[...] + _dg(jnp.tanh(xw), w2_ref[...], _NN, False)
    lw = -jnp.exp(-_softplus(-wl) - 0.5)
    a = _sigmoid(a0_ref[...] + _dg(xa, a2_ref[...], _NN, False))
    g = _dg(_sigmoid(xg), g2_ref[...], _NN, False)
    seg = _segsum_matrix(tc, rh)
    kk = k * kkw_ref[...]
    ss = _dg(kk * kk, seg, _NN, exact)
    kk = kk / jnp.maximum(jnp.sqrt(ss), 1e-12)
    kf = k * (1.0 + (a - 1.0) * kaw_ref[...])
    b = kk * a
    for pi in range(tc // LANES):
        sl = slice(pi * LANES, (pi + 1) * LANES)
        ro_ref[pi] = r[:, sl]
        lwo_ref[pi] = lw[:, sl]
        ko_ref[pi] = kf[:, sl]
        vo_ref[pi] = v[:, sl]
        kko_ref[pi] = kk[:, sl]
        bo_ref[pi] = b[:, sl]
        go_ref[pi] = g[:, sl]


def _rwkv_prep(rkv, lora, prev, mu_rkv, mu_lora, w0, w2, a0, a2, g2p, k_k, k_a, *, rh, exact, tt=256, tc=512):
    T, W3 = rkv.shape
    W = W3 // 3
    LP = lora.shape[1]
    tc = _pick(W, tc)
    nj = W // tc
    halo = prev[0].shape[0] != T or T == 1
    tt = _pick(T, tt, SUBLANES)
    hb = tt // SUBLANES
    xblk = lambda off: pl.BlockSpec((tt, tc), lambda i, j: (i, off * nj + j))
    pblk = lambda off: pl.BlockSpec((SUBLANES, tc), lambda i, j: (jnp.maximum(i * hb - 1, 0), off * nj + j))
    cvec = lambda off: pl.BlockSpec((1, tc), lambda i, j: (0, off * nj + j))
    in_specs = [xblk(0), xblk(1), xblk(2), pl.BlockSpec((tt, LP), lambda i, j: (i, 0))]
    args = [rkv, rkv, rkv, lora]
    if halo:
        in_specs += [pblk(0), pblk(1), pblk(2),
                     pl.BlockSpec((SUBLANES, LP), lambda i, j: (jnp.maximum(i * hb - 1, 0), 0)),
                     cvec(0), cvec(1), cvec(2), pl.BlockSpec((1, LP), lambda i, j: (0, 0))]
        args += [rkv, rkv, rkv, lora, prev[0], prev[0], prev[0], prev[1]]
    else:
        in_specs += [xblk(0), xblk(1), xblk(2), pl.BlockSpec((tt, LP), lambda i, j: (i, 0))]
        args += [prev[0], prev[0], prev[0], prev[1]]
    in_specs += [cvec(0), cvec(1), cvec(2), pl.BlockSpec((1, LP), lambda i, j: (0, 0)),
                 cvec(0), pl.BlockSpec((w2.shape[0], tc), lambda i, j: (0, j)),
                 cvec(0), pl.BlockSpec((a2.shape[0], tc), lambda i, j: (0, j)),
                 pl.BlockSpec((g2p.shape[0], tc), lambda i, j: (0, j)),
                 cvec(0), cvec(0)]
    args += [mu_rkv, mu_rkv, mu_rkv, mu_lora, w0.reshape(1, W), w2, a0.reshape(1, W), a2, g2p,
             k_k.reshape(1, W), k_a.reshape(1, W)]
    npair = tc // LANES
    oshape = jax.ShapeDtypeStruct((W // LANES, T, LANES), F32)
    ospec = pl.BlockSpec((npair, tt, LANES), lambda i, j: (j, i, 0))
    return pl.pallas_call(
        functools.partial(_prep_kernel, halo=halo, rh=rh, exact=exact, tc=tc),
        grid=(T // tt, nj),
        in_specs=in_specs,
        out_specs=[ospec] * 7,
        out_shape=[oshape] * 7,
        compiler_params=_cparams("parallel", "arbitrary"),
        name="rwkv_prep",
    )(*args)


def _scan_kernel(r_ref, lw_ref, k_ref, v_ref, kk_ref, b_ref, g_ref, s0_ref, rk_ref, lnw_ref, lnb_ref,
                 y_ref, sout_ref, s_sc, *, C, rh, npair, precise, t_real):
    c = pl.program_id(1)
    R = 2 * C
    lane = lax.broadcasted_iota(jnp.int32, (C, LANES), 1)
    head0 = lane < rh
    ri = lax.broadcasted_iota(jnp.int32, (R, R), 0)
    ci = lax.broadcasted_iota(jnp.int32, (R, R), 1)
    same = (ri // C) == (ci // C)
    strict = jnp.logical_and(same, (ci % C) < (ri % C))
    incl = jnp.logical_and(same, (ci % C) <= (ri % C))
    tril_c = (lax.broadcasted_iota(jnp.int32, (C, C), 1)
              <= lax.broadcasted_iota(jnp.int32, (C, C), 0)).astype(F32)
    seg = _segsum_matrix(LANES, rh)
    same_head = seg > 0.5
    zero = jnp.zeros((rh, rh), F32)

    def stack_masked(x):
        return jnp.concatenate([jnp.where(head0, x, 0.0), jnp.where(head0, 0.0, x)], axis=0)

    def stack_dup(x):
        return jnp.concatenate([x, x], axis=0)

    @pl.when(c == 0)
    def _():
        def init(p, carry):
            top = jnp.concatenate([s0_ref[0, 2 * p], zero], axis=1)
            bot = jnp.concatenate([zero, s0_ref[0, 2 * p + 1]], axis=1)
            s_sc[p] = jnp.concatenate([top, bot], axis=0)
            return carry
        lax.fori_loop(0, npair, init, 0)

    def single_step(r, lw, k, v, kk, b, S):
        sa = yield -kk, S, _NT, False
        o1 = yield sa, b, _TN, True
        o2 = yield v, k, _TN, True
        s_new = S * jnp.exp(lw[0:1, :]) + jnp.where(same_head, o1 + o2, 0.0)
        y = yield r, s_new, _NT, False
        return y, s_new

    def chunk_steps(r, lw, k, v, kk, b, S):
        cum = yield tril_c, lw, _NN, True
        tot = cum[C - 1:C, :]
        e_incl = jnp.exp(cum)
        e_inv = jnp.exp(-cum)
        e_end = jnp.exp(tot - cum)
        a2 = stack_masked(-kk * jnp.exp(cum - lw))
        r2 = stack_masked(r * e_incl)
        v2 = stack_masked(v)
        bd2 = stack_masked(b * e_end)
        kd2 = stack_masked(k * e_end)
        b2 = stack_dup(b * e_inv)
        k2 = stack_dup(k * e_inv)
        ar = jnp.concatenate([a2, r2], axis=0)
        arb = yield ar, b2, _NT, precise
        ark = yield ar, k2, _NT, precise
        ars = yield ar, S, _NT, precise
        mrb = jnp.where(incl, arb[R:], 0.0)
        mrk = jnp.where(incl, ark[R:], 0.0)
        x = ars[:R]
        n_it = (t_real - 1).bit_length()
        if n_it:
            x = x + (yield jnp.where(strict, ark[:R], 0.0), v2, _NN, precise)
            pw = jnp.where(strict, arb[:R], 0.0)
        for it in range(n_it):
            x = x + (yield pw, x, _NN, precise)
            if it + 1 < n_it:
                pw = yield pw, pw, _NN, precise
        y2 = ars[R:] + (yield mrb, x, _NN, precise)
        y2 = y2 + (yield mrk, v2, _NN, precise)
        y = y2[:C] + y2[C:]
        s_new = S * jnp.exp(tot) + (yield x, bd2, _TN, precise)
        s_new = s_new + (yield v2, kd2, _TN, precise)
        return y, s_new

    def slab(r, lw, k, v, kk, b, g, S, rk, lnw, lnb):
        steps = single_step if t_real == 1 else chunk_steps
        y, s_new = yield from steps(r, lw, k, v, kk, b, S)
        mean = (yield y, seg, _NN, precise) * (1.0 / rh)
        d = y - mean
        var = (yield d * d, seg, _NN, precise) * (1.0 / rh)
        yn = d * lax.rsqrt(var + GN_EPS) * lnw + lnb
        bonus = (yield r * k * rk, seg, _NN, precise) * v
        return (yn + bonus) * g, s_new

    def run_lockstep(gens):
        reqs = [next(gen) for gen in gens]
        results = [None] * len(gens)
        live = list(range(len(gens)))
        while live:
            vals = [_dg(*reqs[i]) for i in live]
            nxt = []
            for i, val in zip(live, vals):
                try:
                    reqs[i] = gens[i].send(val)
                    nxt.append(i)
                except StopIteration as done:
                    results[i] = done.value
            live = nxt
        return results

    U = math.gcd(npair, SLABS_PER_GROUP)

    def body(i, carry):
        grp = pl.ds(pl.multiple_of(i * U, U), U)
        ins = [ref[grp] for ref in (r_ref, lw_ref, k_ref, v_ref, kk_ref, b_ref, g_ref, s_sc,
                                    rk_ref, lnw_ref, lnb_ref)]
        outs = run_lockstep([slab(*[a[u] for a in ins]) for u in range(U)])
        y_ref[grp] = jnp.stack([o[0] for o in outs]).astype(y_ref.dtype)
        s_sc[grp] = jnp.stack([o[1] for o in outs])
        return carry

    lax.fori_loop(0, npair // U, body, 0)

    @pl.when(c == pl.num_programs(1) - 1)
    def _():
        def fin(p, carry):
            S = s_sc[p]
            sout_ref[0, 2 * p] = S[:rh, :rh]
            sout_ref[0, 2 * p + 1] = S[rh:, rh:]
            return carry
        lax.fori_loop(0, npair, fin, 0)


def _rwkv_scan(prep, s0, r_k, ln_w, ln_b, *, B, C, rh, precise, out_dtype, t_real):
    npair, BT, _ = prep[0].shape
    Tb = BT // B
    nc = Tb // C
    H = s0.shape[1]
    xspec = pl.BlockSpec((npair, C, LANES), lambda b, c: (0, b * nc + c, 0))
    sspec = pl.BlockSpec((1, H, rh, rh), lambda b, c: (b, 0, 0, 0))
    pspec = pl.BlockSpec((npair, 1, LANES), lambda b, c: (0, 0, 0))
    y, s_out = pl.pallas_call(
        functools.partial(_scan_kernel, C=C, rh=rh, npair=npair, precise=precise, t_real=t_real),
        grid=(B, nc),
        in_specs=[xspec] * 7 + [sspec, pspec, pspec, pspec],
        out_specs=[xspec, sspec],
        out_shape=[jax.ShapeDtypeStruct((npair, BT, LANES), out_dtype),
                   jax.ShapeDtypeStruct((B, H, rh, rh), F32)],
        scratch_shapes=[pltpu.VMEM((npair, 2 * rh, 2 * rh), F32)],
        compiler_params=_cparams("parallel", "arbitrary"),
        name="rwkv_scan",
    )(*prep, s0, r_k.reshape(npair, 1, LANES), ln_w.reshape(npair, 1, LANES), ln_b.reshape(npair, 1, LANES))
    return y, s_out


def _router_kernel(x_ref, nw_ref, rw_ref, rb_ref, h_ref, lg_ref, *, precise):
    x = x_ref[...]
    ms = jnp.mean(x * x, axis=-1, keepdims=True)
    h = x * lax.rsqrt(ms + NORM_EPS) * nw_ref[...]
    h_ref[...] = h.astype(h_ref.dtype)
    lg_ref[...] = _dg(h, rw_ref[...], _NN, precise) + rb_ref[...]


def _router(x, norm_w, rw, rb, *, precise, h_dtype, tm=512):
    M, D = x.shape
    NR = rw.shape[1]
    tm = _pick(M, tm, SUBLANES)
    return pl.pallas_call(
        functools.partial(_router_kernel, precise=precise),
        grid=(M // tm,),
        in_specs=[pl.BlockSpec((tm, D), lambda i: (i, 0)), pl.BlockSpec((1, D), lambda i: (0, 0)),
                  pl.BlockSpec((D, NR), lambda i: (0, 0)), pl.BlockSpec((1, NR), lambda i: (0, 0))],
        out_specs=[pl.BlockSpec((tm, D), lambda i: (i, 0)), pl.BlockSpec((tm, NR), lambda i: (i, 0))],
        out_shape=[jax.ShapeDtypeStruct((M, D), h_dtype), jax.ShapeDtypeStruct((M, NR), F32)],
        compiler_params=_cparams("parallel"),
        name="moe_router",
    )(x, norm_w.reshape(1, D), rw, rb)


def _moe_kernel(be_ref, nu_ref, x_ref, wg_ref, wu_ref, wd_ref, sw_ref, o_ref):
    b = pl.program_id(0)
    j = pl.program_id(1)
    last = j == pl.num_programs(1) - 1
    used = b < nu_ref[0]

    @pl.when(jnp.logical_and(used, j == 0))
    def _():
        o_ref[...] = jnp.zeros(o_ref.shape, F32)

    @pl.when(used)
    def _():
        x = x_ref[...]
        tn = wg_ref.shape[2]
        wgu = jnp.concatenate([wg_ref[0].astype(BF16), wu_ref[0].astype(BF16)], axis=1)
        hgu = jnp.dot(x, wgu, preferred_element_type=F32)
        hg, hu = hgu[:, :tn], hgu[:, tn:]
        h = hg * _sigmoid(hg) * hu
        o_ref[...] += jnp.dot(h.astype(BF16), wd_ref[0].astype(BF16), preferred_element_type=F32)

    @pl.when(jnp.logical_and(used, last))
    def _():
        o_ref[...] = o_ref[...] * sw_ref[...]

    @pl.when(jnp.logical_and(jnp.logical_not(used), last))
    def _():
        o_ref[...] = jnp.zeros(o_ref.shape, F32)


def _moe_experts(xs, blk_e, n_used, slot_w, wg, wu, wd, *, blk, tn):
    P, D = xs.shape
    DE = wg.shape[2]
    tn = _pick(DE, tn)
    nj = DE // tn
    nb = P // blk

    def eff(b, j, nu):
        live = b < nu[0]
        return jnp.where(live, b, nu[0] - 1), jnp.where(live, j, nj - 1)

    def wg_map(b, j, be, nu):
        bb, jj = eff(b, j, nu)
        return be[bb], 0, jj

    def wd_map(b, j, be, nu):
        bb, jj = eff(b, j, nu)
        return be[bb], jj, 0

    def x_map(b, j, be, nu):
        return eff(b, j, nu)[0], 0

    grid_spec = pltpu.PrefetchScalarGridSpec(
        num_scalar_prefetch=2,
        grid=(nb, nj),
        in_specs=[pl.BlockSpec((blk, D), x_map),
                  pl.BlockSpec((1, D, tn), wg_map), pl.BlockSpec((1, D, tn), wg_map),
                  pl.BlockSpec((1, tn, D), wd_map),
                  pl.BlockSpec((blk, 1), x_map)],
        out_specs=pl.BlockSpec((blk, D), lambda b, j, be, nu: (b, 0)),
    )
    return pl.pallas_call(
        _moe_kernel,
        grid_spec=grid_spec,
        out_shape=jax.ShapeDtypeStruct((P, D), F32),
        compiler_params=_cparams("arbitrary", "arbitrary"),
        name="moe_experts",
    )(blk_e, n_used, xs, wg, wu, wd, slot_w.reshape(P, 1))


def _hier_moe(x_groups, norm_w, rg_w, rg_b, re_w, re_b, wg, wu, wd, *, blk, tn):
    D = x_groups[0].shape[1]
    NG = rg_w.shape[1]
    E = re_w.shape[1]
    EPG = E // NG
    NR = -(-(NG + E) // LANES) * LANES
    rw = jnp.pad(jnp.concatenate([rg_w, re_w], axis=1), ((0, 0), (0, NR - NG - E))).astype(BF16)
    rb = jnp.pad(jnp.concatenate([rg_b, re_b]), (0, NR - NG - E)).reshape(1, NR)
    routed = [_router(x, norm_w, rw, rb, precise=False, h_dtype=BF16) for x in x_groups]
    logits = jnp.concatenate([r[1] for r in routed], axis=0)
    T = logits.shape[0]
    gprob = jax.nn.softmax(logits[:, :NG], axis=-1)
    gsel = jnp.argmax(gprob, axis=-1)
    gp = jnp.take_along_axis(gprob, gsel[:, None], axis=-1)[:, 0]
    elog = logits[:, NG:NG + E].reshape(T, NG, EPG)
    elog = jnp.take_along_axis(elog, gsel[:, None, None], axis=1)[:, 0]
    top_p, top_i = lax.top_k(jax.nn.softmax(elog, axis=-1), TOP_K)
    top_p = top_p / jnp.sum(top_p, axis=-1, keepdims=True)
    weights = gp[:, None] * top_p
    expert_idx = gsel[:, None].astype(jnp.int32) * EPG + top_i.astype(jnp.int32)

    TK = T * TOP_K
    nb = -(-TK // blk) + E
    P = nb * blk
    flat_e = expert_idx.reshape(-1)
    flat_w = weights.reshape(-1)
    flat_t = jnp.repeat(jnp.arange(T, dtype=jnp.int32), TOP_K)
    order = jnp.argsort(flat_e)
    se = flat_e[order]
    counts = jnp.bincount(flat_e, length=E)
    pcounts = (counts + blk - 1) // blk * blk
    pend = jnp.cumsum(pcounts)
    pstart = pend - pcounts
    start = jnp.cumsum(counts) - counts
    dest = (pstart[se] + jnp.arange(TK) - start[se]).astype(jnp.int32)
    slot_t = jnp.full((P,), T, jnp.int32).at[dest].set(flat_t[order])
    slot_w = jnp.zeros((P,), F32).at[dest].set(flat_w[order])
    blk_e = jnp.minimum(jnp.searchsorted(pend, jnp.arange(nb) * blk, side='right'), E - 1).astype(jnp.int32)
    n_used = (pend[-1] // blk).astype(jnp.int32).reshape(1)
    pos = jnp.zeros((TK,), jnp.int32).at[order].set(dest)

    xs = jnp.concatenate([r[0] for r in routed] + [jnp.zeros((1, D), BF16)], axis=0)[slot_t]
    ys = _moe_experts(xs, blk_e, n_used, slot_w, wg, wu, wd, blk=blk, tn=tn)
    pos = pos.reshape(T, TOP_K)
    outs, t0 = [], 0
    for x in x_groups:
        pg = pos[t0:t0 + x.shape[0]]
        outs.append(x + ys[pg[:, 0]] + ys[pg[:, 1]])
        t0 += x.shape[0]
    return outs


def _pairs_to_rows(y):
    npair, T, _ = y.shape
    return jnp.transpose(y, (1, 0, 2)).reshape(T, npair * LANES)


def _mixer(x, positions, prev_rows, s0, B, cfg, w, *, exact, attn_fn, q_scale, emit_bf16):
    T, D = x.shape
    hd, HK, G, vd, rh = cfg['hd'], cfg['HK'], cfg['G'], cfg['vd'], cfg['rh']
    qcols, kcols, vcols = HK * G * 2 * hd, HK * 2 * hd, HK * vd
    W = cfg['W']
    mm = functools.partial(_matmul, precise=False)

    h = _rmsnorm(x, w['norm1_w'], BF16)
    qkv = mm(h, w['w_qkv'], qcols + kcols + vcols)
    rkv = mm(h, w['w_rkv'], 3 * W)
    lora = mm(h, w['w_lora'], cfg['lp'], tn=cfg['lp'])
    gates = mm(h, w['w_gates'], 2 * D)

    cos_full, sin_signed = _rope_tables(positions, hd)
    outs = _qk_rope(qkv, cos_full, sin_signed, w['q_norm_w'], w['k_norm_w'], qcols, kcols, vcols,
                    q_scale=q_scale, q_dtype=BF16, emit_bf16=emit_bf16)
    q, k_new, v_new = outs[:3]
    o_attn = attn_fn(q, outs)

    prep = _rwkv_prep(rkv, lora, prev_rows, w['mu_rkv'], w['mu_lora'], w['rwkv_w0'], w['rwkv_w2'],
                      w['rwkv_a0'], w['rwkv_a2'], w['g2p'], w['rwkv_k_k'], w['rwkv_k_a'],
                      rh=rh, exact=exact)
    Tb = T // B
    if Tb % 64 == 0:
        C = 64
    else:
        C = SUBLANES
        pad = -Tb % C
        prep = [jnp.pad(a.reshape(-1, B, Tb, LANES), ((0, 0), (0, 0), (0, pad), (0, 0)))
                .reshape(a.shape[0], B * (Tb + pad), LANES) for a in prep]
    y_r, s_fin = _rwkv_scan(prep, s0, w['rwkv_r_k'], w['rwkv_ln_w'], w['rwkv_ln_b'],
                            B=B, C=C, rh=rh, precise=exact, out_dtype=BF16, t_real=min(Tb, C))
    if Tb % 64:
        y_r = y_r.reshape(y_r.shape[0], B, -1, LANES)[:, :, :Tb].reshape(y_r.shape[0], T, LANES)
    o_rwkv = _pairs_to_rows(y_r)

    mixed = _merge(o_attn, w['w_branch_attn'], o_rwkv, w['w_branch_rwkv'], gates, w['gate_b'],
                   precise=False, out_dtype=BF16)
    x1 = _matmul_residual(mixed, w['w_out'], x, precise=False)
    return x1, k_new, v_new, s_fin, rkv, lora


def kernel(x_prompt, x_sample, cache_k, cache_v, state_rwkv, state_shift, page_table, norm1_w, w_in, gate_b, q_norm_w, k_norm_w, lambda_q1, lambda_k1, lambda_q2, lambda_k2, subln_w, rwkv_mu, rwkv_w0, rwkv_w2, rwkv_a0, rwkv_a2, rwkv_g2, rwkv_k_k, rwkv_k_a, rwkv_r_k, rwkv_ln_w, rwkv_ln_b, w_branch_attn, w_branch_rwkv, w_out, norm2_w, router_group_w, router_group_b, router_expert_w, router_expert_b, expert_w_gate, expert_w_up, expert_w_down):
    B, T, D = x_prompt.shape
    DB, S, _ = x_sample.shape
    L = norm1_w.shape[0]
    assert L == 1 and B == 1 and S == 1, "one trunk layer, one prompt sequence, one new token per sample"
    hd = q_norm_w.shape[1]
    HK = cache_k.shape[3]
    vd = cache_v.shape[4]
    G = w_branch_attn.shape[1] // vd // HK
    H, rh = rwkv_r_k.shape[1], rwkv_r_k.shape[2]
    W = H * rh
    dl, al, gl = rwkv_w2.shape[1], rwkv_a2.shape[1], rwkv_g2.shape[1]
    lcols = dl + al + gl
    lp = -(-lcols // LANES) * LANES
    qkv_cols = HK * G * 2 * hd + HK * 2 * hd + HK * vd
    r_off = qkv_cols
    g_off = r_off + 3 * W + lcols
    n_pages, page = page_table.shape[1], cache_k.shape[2]
    past = n_pages * page
    cfg = dict(hd=hd, HK=HK, G=G, vd=vd, rh=rh, W=W, lp=lp, r_off=r_off)
    l = 0
    (cache_k, cache_v, state_rwkv, state_shift, norm1_w, w_in, gate_b, q_norm_w, k_norm_w, lambda_q1,
     lambda_k1, lambda_q2, lambda_k2, subln_w, rwkv_mu, rwkv_w0, rwkv_w2, rwkv_a0, rwkv_a2, rwkv_g2,
     rwkv_k_k, rwkv_k_a, rwkv_r_k, rwkv_ln_w, rwkv_ln_b, w_branch_attn, w_branch_rwkv, w_out, norm2_w,
     router_group_w, router_group_b, router_expert_w, router_expert_b, expert_w_gate, expert_w_up,
     expert_w_down) = [a.reshape(a.shape[1:]) for a in (
         cache_k, cache_v, state_rwkv, state_shift, norm1_w, w_in, gate_b, q_norm_w, k_norm_w, lambda_q1,
         lambda_k1, lambda_q2, lambda_k2, subln_w, rwkv_mu, rwkv_w0, rwkv_w2, rwkv_a0, rwkv_a2, rwkv_g2,
         rwkv_k_k, rwkv_k_a, rwkv_r_k, rwkv_ln_w, rwkv_ln_b, w_branch_attn, w_branch_rwkv, w_out, norm2_w,
         router_group_w, router_group_b, router_expert_w, router_expert_b, expert_w_gate, expert_w_up,
         expert_w_down)]
    lam_init = 0.8 - 0.6 * math.exp(-0.3 * l)
    lams = (lambda_q1, lambda_k1, lambda_q2, lambda_k2)

    mu = rwkv_mu
    g2p = jnp.pad(rwkv_g2, ((0, lp - lcols), (0, 0)))
    w_lora = jnp.pad(w_in[:, r_off + 3 * W:r_off + 3 * W + lcols], ((0, 0), (0, lp - lcols)))
    w = dict(
        norm1_w=norm1_w, q_norm_w=q_norm_w, k_norm_w=k_norm_w, gate_b=gate_b,
        mu_rkv=mu[:3 * W].reshape(1, 3 * W), mu_lora=jnp.pad(mu[3 * W:], (0, lp - lcols)).reshape(1, lp),
        rwkv_w0=rwkv_w0, rwkv_a0=rwkv_a0, rwkv_k_k=rwkv_k_k, rwkv_k_a=rwkv_k_a,
        rwkv_r_k=rwkv_r_k, rwkv_ln_w=rwkv_ln_w, rwkv_ln_b=rwkv_ln_b,
        norm2_w=norm2_w, router_group_w=router_group_w, router_group_b=router_group_b,
        router_expert_w=router_expert_w, router_expert_b=router_expert_b,
        expert_w_gate=expert_w_gate, expert_w_up=expert_w_up, expert_w_down=expert_w_down,
        w_qkv=w_in[:, :qkv_cols].astype(BF16), w_rkv=w_in[:, r_off:r_off + 3 * W].astype(BF16),
        w_lora=w_lora.astype(BF16), w_gates=w_in[:, g_off:g_off + 2 * D].astype(BF16),
        rwkv_w2=rwkv_w2.astype(BF16), rwkv_a2=rwkv_a2.astype(BF16), g2p=g2p.astype(BF16),
        w_branch_attn=w_branch_attn.astype(BF16), w_branch_rwkv=w_branch_rwkv.astype(BF16),
        w_out=w_out.astype(BF16))

    def prompt_attn(q, outs):
        return _flash_diff_attn(q, outs[3], outs[4], lams, subln_w, HK=HK, G=G, hd=hd, vd=vd,
                                lam_init=lam_init, out_dtype=BF16)

    zero_prev = (jnp.zeros((1, 3 * W), F32), jnp.zeros((1, lp), F32))
    xp1, kp, vp, sp, rkv_p, lora_p = _mixer(
        x_prompt.reshape(T, D), jnp.arange(T), zero_prev, jnp.zeros((1, H, rh, rh), F32), 1, cfg, w,
        exact=False, attn_fn=prompt_attn, q_scale=hd ** -0.5 * math.log2(math.e), emit_bf16=True)
    shift_p = jnp.concatenate([rkv_p[T - 1], lora_p[T - 1, :lcols]])
    moe_w = (norm2_w, router_group_w, router_group_b, router_expert_w, router_expert_b,
             expert_w_gate, expert_w_up, expert_w_down)
    yp, = _hier_moe([xp1], *moe_w, blk=min(MOE_ROW_BLOCK, max(2 * SUBLANES, T // 16)), tn=MOE_HIDDEN_TILE)

    def sample_attn(q, outs):
        return _decode_diff_attn(q, outs[1], outs[2], cache_k, cache_v, page_table, lams, subln_w,
                                 HK=HK, G=G, hd=hd, vd=vd, lam_init=lam_init)

    sh = state_shift
    prev_s = (sh[:, :3 * W], jnp.pad(sh[:, 3 * W:], ((0, 0), (0, lp - lcols))))
    xs1, ks, vs, ss, rkv_s, lora_s = _mixer(
        x_sample.reshape(DB, D), jnp.full((DB,), past, jnp.int32), prev_s, state_rwkv, DB, cfg, w,
        exact=True, attn_fn=sample_attn, q_scale=1.0, emit_bf16=False)
    shift_s = jnp.concatenate([rkv_s, lora_s[:, :lcols]], axis=1)

    ys, = _hier_moe([xs1], *moe_w, blk=2 * SUBLANES, tn=MOE_HIDDEN_TILE)

    return (yp.reshape(1, T, D), ys.reshape(DB, 1, D),
            kp.reshape(1, 1, T, HK, 2, hd), vp.reshape(1, 1, T, HK, vd),
            sp.reshape(1, 1, H, rh, rh), shift_p.reshape(1, 1, -1),
            ks.reshape(1, DB, 1, HK, 2, hd), vs.reshape(1, DB, 1, HK, vd),
            ss.reshape(1, DB, H, rh, rh), shift_s.reshape(1, DB, -1))
```

```python
import functools
import math

import jax
import jax.numpy as jnp
from jax import lax
from jax.experimental import pallas as pl
from jax.experimental.pallas import tpu as pltpu

F32 = jnp.float32
BF16 = jnp.bfloat16
HIGHEST = lax.Precision.HIGHEST

LANES = 128
SUBLANES = 8
VMEM_LIMIT = 56 * 1024 * 1024

ROPE_THETA = 10000.0
NORM_EPS = 1e-6
GN_EPS = 64e-5
TOP_K = 2
SLABS_PER_GROUP = 16
DECODE_PAGES_PER_STEP = 8
MOE_ROW_BLOCK = 512
MOE_HIDDEN_TILE = 256

_NN = (((1,), (0,)), ((), ()))
_NT = (((1,), (1,)), ((), ()))
_TN = (((0,), (0,)), ((), ()))


def _cparams(*sem):
    return pltpu.CompilerParams(dimension_semantics=sem, vmem_limit_bytes=VMEM_LIMIT)


def _pick(n, pref, unit=LANES):
    if n <= pref:
        return n
    t = pref // unit * unit
    while t > unit and n % t:
        t -= unit
    assert n % t == 0, (n, pref)
    return t


def _dg(a, b, dn, precise):
    if precise:
        return lax.dot_general(a.astype(F32), b.astype(F32), dn, precision=HIGHEST,
                               preferred_element_type=F32)
    return lax.dot_general(a.astype(BF16), b.astype(BF16), dn, preferred_element_type=F32)


def _sigmoid(x):
    return 1.0 / (1.0 + jnp.exp(-x))


def _softplus(x):
    return jnp.maximum(x, 0.0) + jnp.log(1.0 + jnp.exp(-jnp.abs(x)))


def _rms_kernel(x_ref, w_ref, o_ref, *, eps):
    x = x_ref[...]
    ms = jnp.mean(x * x, axis=-1, keepdims=True)
    o_ref[...] = (x * lax.rsqrt(ms + eps) * w_ref[...]).astype(o_ref.dtype)


def _rmsnorm(x, w, out_dtype, tm=512):
    M, D = x.shape
    tm = _pick(M, tm, SUBLANES)
    return pl.pallas_call(
        functools.partial(_rms_kernel, eps=NORM_EPS),
        grid=(M // tm,),
        in_specs=[pl.BlockSpec((tm, D), lambda i: (i, 0)),
                  pl.BlockSpec((1, D), lambda i: (0, 0))],
        out_specs=pl.BlockSpec((tm, D), lambda i: (i, 0)),
        out_shape=jax.ShapeDtypeStruct((M, D), out_dtype),
        compiler_params=_cparams("parallel"),
        name="rmsnorm",
    )(x, w.reshape(1, D))


def _mm_kernel(a_ref, b_ref, o_ref, *, precise):
    o_ref[...] = _dg(a_ref[...], b_ref[...], _NN, precise).astype(o_ref.dtype)


def _matmul(a, b, n, col0=0, *, precise, out_dtype=F32, tm=1024, tn=512):
    M, K = a.shape
    tm = _pick(M, tm, SUBLANES)
    tn = _pick(n, tn)
    if col0 % tn:
        b, col0 = b[:, col0:col0 + n], 0
    c0 = col0 // tn
    return pl.pallas_call(
        functools.partial(_mm_kernel, precise=precise),
        grid=(M // tm, n // tn),
        in_specs=[pl.BlockSpec((tm, K), lambda i, j: (i, 0)),
                  pl.BlockSpec((K, tn), lambda i, j: (0, c0 + j))],
        out_specs=pl.BlockSpec((tm, tn), lambda i, j: (i, j)),
        out_shape=jax.ShapeDtypeStruct((M, n), out_dtype),
        compiler_params=_cparams("parallel", "arbitrary"),
        name="matmul",
    )(a, b)


def _merge_kernel(a1_ref, b1_ref, a2_ref, b2_ref, g1_ref, g2_ref, gb1_ref, gb2_ref, o_ref, *, precise):
    o1 = _dg(a1_ref[...], b1_ref[...], _NN, precise)
    o2 = _dg(a2_ref[...], b2_ref[...], _NN, precise)
    s1 = _sigmoid(g1_ref[...] + gb1_ref[...])
    s2 = _sigmoid(g2_ref[...] + gb2_ref[...])
    o_ref[...] = (s1 * o1 + s2 * o2).astype(o_ref.dtype)


def _merge(a1, b1, a2, b2, gates, gate_b, *, precise, out_dtype, tm=512, tn=512):
    M, K = a1.shape
    K2 = a2.shape[1]
    D = b1.shape[1]
    tm = _pick(M, tm, SUBLANES)
    tn = _pick(D, tn)
    nj = D // tn
    gb = gate_b.reshape(1, 2 * D)
    return pl.pallas_call(
        functools.partial(_merge_kernel, precise=precise),
        grid=(M // tm, nj),
        in_specs=[pl.BlockSpec((tm, K), lambda i, j: (i, 0)),
                  pl.BlockSpec((K, tn), lambda i, j: (0, j)),
                  pl.BlockSpec((tm, K2), lambda i, j: (i, 0)),
                  pl.BlockSpec((K2, tn), lambda i, j: (0, j)),
                  pl.BlockSpec((tm, tn), lambda i, j: (i, j)),
                  pl.BlockSpec((tm, tn), lambda i, j: (i, nj + j)),
                  pl.BlockSpec((1, tn), lambda i, j: (0, j)),
                  pl.BlockSpec((1, tn), lambda i, j: (0, nj + j))],
        out_specs=pl.BlockSpec((tm, tn), lambda i, j: (i, j)),
        out_shape=jax.ShapeDtypeStruct((M, D), out_dtype),
        compiler_params=_cparams("parallel", "arbitrary"),
        name="merge",
    )(a1, b1, a2, b2, gates, gates, gb, gb)


def _mm_res_kernel(a_ref, b_ref, r_ref, o_ref, *, precise):
    o_ref[...] = r_ref[...] + _dg(a_ref[...], b_ref[...], _NN, precise)


def _matmul_residual(a, b, res, *, precise, tm=1024, tn=512):
    M, K = a.shape
    N = b.shape[1]
    tm = _pick(M, tm, SUBLANES)
    tn = _pick(N, tn)
    return pl.pallas_call(
        functools.partial(_mm_res_kernel, precise=precise),
        grid=(M // tm, N // tn),
        in_specs=[pl.BlockSpec((tm, K), lambda i, j: (i, 0)),
                  pl.BlockSpec((K, tn), lambda i, j: (0, j)),
                  pl.BlockSpec((tm, tn), lambda i, j: (i, j))],
        out_specs=pl.BlockSpec((tm, tn), lambda i, j: (i, j)),
        out_shape=jax.ShapeDtypeStruct((M, N), F32),
        compiler_params=_cparams("parallel", "arbitrary"),
        name="matmul_residual",
    )(a, b, res)


def _qkrope_kernel(qkv_ref, cos_ref, sin_ref, qw_ref, kw_ref, *out_refs, nq, nk, hd, vcols, scale, emit_bf16):
    if emit_bf16:
        q_ref, k_ref, v_ref, kb_ref, vb_ref = out_refs
    else:
        q_ref, k_ref, v_ref = out_refs
    cos = cos_ref[...]
    sin = sin_ref[...]

    def norm_rope(x, w):
        ms = jnp.mean(x * x, axis=-1, keepdims=True)
        xn = x * lax.rsqrt(ms + NORM_EPS) * w
        return xn * cos + pltpu.roll(xn, hd // 2, axis=1) * sin

    qw = qw_ref[...]
    kw = kw_ref[...]
    for c in range(nq):
        o = norm_rope(qkv_ref[:, c * hd:(c + 1) * hd], qw) * scale
        q_ref[:, c * hd:(c + 1) * hd] = o.astype(q_ref.dtype)
    for c in range(nk):
        o = norm_rope(qkv_ref[:, (nq + c) * hd:(nq + c + 1) * hd], kw)
        k_ref[:, c * hd:(c + 1) * hd] = o
        if emit_bf16:
            kb_ref[:, c * hd:(c + 1) * hd] = o.astype(BF16)
    v = qkv_ref[:, (nq + nk) * hd:(nq + nk) * hd + vcols]
    v_ref[...] = v
    if emit_bf16:
        vb_ref[...] = v.astype(BF16)


def _qk_rope(qkv, cos_full, sin_signed, q_norm_w, k_norm_w, qcols, kcols, vcols, *, q_scale, q_dtype,
             emit_bf16, tq=256):
    T = qkv.shape[0]
    hd = q_norm_w.shape[0]
    tq = _pick(T, tq, SUBLANES)
    nq, nk = qcols // hd, kcols // hd
    row = lambda i: (i, 0)
    out_shape = [jax.ShapeDtypeStruct((T, qcols), q_dtype),
                 jax.ShapeDtypeStruct((T, kcols), F32),
                 jax.ShapeDtypeStruct((T, vcols), F32)]
    out_specs = [pl.BlockSpec((tq, qcols), row), pl.BlockSpec((tq, kcols), row), pl.BlockSpec((tq, vcols), row)]
    if emit_bf16:
        out_shape += [jax.ShapeDtypeStruct((T, kcols), BF16), jax.ShapeDtypeStruct((T, vcols), BF16)]
        out_specs += [pl.BlockSpec((tq, kcols), row), pl.BlockSpec((tq, vcols), row)]
    return pl.pallas_call(
        functools.partial(_qkrope_kernel, nq=nq, nk=nk, hd=hd, vcols=vcols, scale=q_scale,
                          emit_bf16=emit_bf16),
        grid=(T // tq,),
        in_specs=[pl.BlockSpec((tq, qkv.shape[1]), row),
                  pl.BlockSpec((tq, hd), row), pl.BlockSpec((tq, hd), row),
                  pl.BlockSpec((1, hd), lambda i: (0, 0)), pl.BlockSpec((1, hd), lambda i: (0, 0))],
        out_specs=out_specs,
        out_shape=out_shape,
        compiler_params=_cparams("parallel"),
        name="qk_norm_rope",
    )(qkv, cos_full, sin_signed, q_norm_w.reshape(1, hd), k_norm_w.reshape(1, hd))


def _rope_tables(positions, hd):
    inv = 1.0 / (ROPE_THETA ** (jnp.arange(0, hd, 2, dtype=F32) / hd))
    ang = positions.astype(F32)[:, None] * inv[None, :]
    cos, sin = jnp.cos(ang), jnp.sin(ang)
    return jnp.concatenate([cos, cos], axis=-1), jnp.concatenate([-sin, sin], axis=-1)


def _diff_lambda_in_kernel(lq1_ref, lk1_ref, lq2_ref, lk2_ref, lam_init):
    e1 = jnp.exp(jnp.sum(lq1_ref[...] * lk1_ref[...], axis=-1, keepdims=True))
    e2 = jnp.exp(jnp.sum(lq2_ref[...] * lk2_ref[...], axis=-1, keepdims=True))
    return e1 - e2 + lam_init


def _attn_finalize(acc1, l1, acc2, l2, lam, subw, lam_init):
    o = acc1 / l1 - lam * (acc2 / l2)
    ms = jnp.mean(o * o, axis=-1, keepdims=True)
    return o * lax.rsqrt(ms + NORM_EPS) * subw * (1.0 - lam_init)


def _flash_kernel(qi_ref, ki_ref, q_ref, k_ref, v_ref, lq1_ref, lk1_ref, lq2_ref, lk2_ref, subw_ref, o_ref,
                  m_sc, l_sc, acc_sc, s_sc, *, G, hd, vd, tq, lam_init):
    qi = qi_ref[pl.program_id(1)]
    ki = ki_ref[pl.program_id(1)]

    @pl.when(ki == 0)
    def _():
        m_sc[...] = jnp.full(m_sc.shape, -jnp.inf, F32)
        l_sc[...] = jnp.zeros(l_sc.shape, F32)
        acc_sc[...] = jnp.zeros(acc_sc.shape, F32)

    def tile(diagonal):
        v = v_ref[...]
        for m in range(2):
            kb = k_ref[:, m * hd:(m + 1) * hd]
            for g in range(G):
                idx = g * 2 + m
                q = q_ref[:, idx * hd:(idx + 1) * hd]
                s = lax.dot_general(q, kb, _NT, preferred_element_type=F32)
                if diagonal:
                    rows = lax.broadcasted_iota(jnp.int32, (tq, tq), 0)
                    cols = lax.broadcasted_iota(jnp.int32, (tq, tq), 1)
                    s = jnp.where(cols <= rows, s, -jnp.inf)
                s_sc[...] = s
                m_prev = m_sc[idx]
                m_new = jnp.maximum(m_prev, jnp.max(s_sc[...], axis=-1, keepdims=True))
                m_sc[idx] = m_new
                alpha = jnp.exp2(m_prev - m_new)
                p = jnp.exp2(s_sc[...] - jnp.tile(m_new, (1, tq // LANES)))
                l_sc[idx] = alpha * l_sc[idx] + jnp.sum(p, axis=-1, keepdims=True)
                pv = jnp.dot(p.astype(BF16), v, preferred_element_type=F32)
                acc_sc[idx] = jnp.tile(alpha, (1, vd // LANES)) * acc_sc[idx] + pv

    @pl.when(ki < qi)
    def _():
        tile(False)

    @pl.when(ki == qi)
    def _():
        tile(True)
        lam = _diff_lambda_in_kernel(lq1_ref, lk1_ref, lq2_ref, lk2_ref, lam_init)
        subw = subw_ref[...]
        for g in range(G):
            o = _attn_finalize(acc_sc[2 * g], l_sc[2 * g][:, :1], acc_sc[2 * g + 1], l_sc[2 * g + 1][:, :1],
                               lam, subw, lam_init)
            o_ref[:, g * vd:(g + 1) * vd] = o.astype(o_ref.dtype)


def _flash_diff_attn(q, kb, vb, lams, subln_w, *, HK, G, hd, vd, lam_init, out_dtype, tq=512):
    T = q.shape[0]
    tq = _pick(T, tq)
    nq = T // tq
    vec = lambda n: pl.BlockSpec((1, n), lambda h, s, qt, kt: (0, 0))
    pairs = [(i, j) for i in range(nq) for j in range(i + 1)]
    qi_tab = jnp.asarray([p[0] for p in pairs], jnp.int32)
    ki_tab = jnp.asarray([p[1] for p in pairs], jnp.int32)
    grid_spec = pltpu.PrefetchScalarGridSpec(
        num_scalar_prefetch=2,
        grid=(HK, len(pairs)),
        in_specs=[pl.BlockSpec((tq, G * 2 * hd), lambda h, s, qt, kt: (qt[s], h)),
                  pl.BlockSpec((tq, 2 * hd), lambda h, s, qt, kt: (kt[s], h)),
                  pl.BlockSpec((tq, vd), lambda h, s, qt, kt: (kt[s], h)),
                  vec(hd), vec(hd), vec(hd), vec(hd), vec(vd)],
        out_specs=pl.BlockSpec((tq, G * vd), lambda h, s, qt, kt: (qt[s], h)),
        scratch_shapes=[pltpu.VMEM((2 * G, tq, LANES), F32), pltpu.VMEM((2 * G, tq, LANES), F32),
                        pltpu.VMEM((2 * G, tq, vd), F32), pltpu.VMEM((tq, tq), F32)],
    )
    return pl.pallas_call(
        functools.partial(_flash_kernel, G=G, hd=hd, vd=vd, tq=tq, lam_init=lam_init),
        grid_spec=grid_spec,
        out_shape=jax.ShapeDtypeStruct((T, HK * G * vd), out_dtype),
        compiler_params=_cparams("parallel", "arbitrary"),
        name="flash_diff_attn",
    )(qi_tab, ki_tab, q, kb, vb, *[x.reshape(1, -1) for x in lams], subln_w.reshape(1, vd))


def _decode_kernel(pt_ref, q_ref, kn_ref, vn_ref, *refs, HK, G, vd, pps, scale, lam_init):
    kc_refs, vc_refs = refs[:pps], refs[pps:2 * pps]
    (lq1_ref, lk1_ref, lq2_ref, lk2_ref, subw_ref, o_ref, s_sc, a_sc, anew_sc, acc_sc) = refs[2 * pps:]
    ph = pl.program_id(1)
    pg = pl.program_id(2)
    last = pl.num_programs(2) - 1
    NI = q_ref.shape[1]
    half = NI // 2
    q = q_ref[0]
    nk = q.shape[1] // LANES
    page = kc_refs[0].shape[0] // nk
    nv = vc_refs[0].shape[0] // page
    v_order = [et * HK + hk for hk in range(HK) for et in range(nv // HK)]

    def token_rows(ref, order):
        x = ref[...].reshape(page, len(order) * LANES)
        if list(order) != list(range(len(order))):
            x = jnp.concatenate([x[:, s * LANES:(s + 1) * LANES] for s in order], axis=1)
        return x.astype(BF16)

    row_hk = lax.broadcasted_iota(jnp.int32, (half, vd), 0) // G

    def own_head(x):
        out = x[:, :vd]
        for hk in range(1, HK):
            out = jnp.where(row_hk == hk, x[:, hk * vd:(hk + 1) * vd], out)
        return out

    @pl.when(ph == 0)
    def _():
        s_sc[pg] = scale * jnp.concatenate(
            [lax.dot_general(q, token_rows(kc, range(nk)), _NT, preferred_element_type=F32) for kc in kc_refs],
            axis=1)

    @pl.when(jnp.logical_and(ph == 0, pg == last))
    def _():
        lam = _diff_lambda_in_kernel(lq1_ref, lk1_ref, lq2_ref, lk2_ref, lam_init)
        s_new = scale * jnp.sum(q.astype(F32) * kn_ref[0].astype(BF16).astype(F32), axis=-1, keepdims=True)
        s = s_sc[...]
        m = jnp.maximum(jnp.max(jnp.max(s, axis=0), axis=-1, keepdims=True), s_new)
        e = jnp.exp(s - m[None])
        e_new = jnp.exp(s_new - m)
        denom = jnp.sum(jnp.sum(e, axis=0), axis=-1, keepdims=True) + e_new
        p = e / denom[None]
        p_new = e_new / denom
        a_sc[...] = (p[:, :half] - lam * p[:, half:]).astype(BF16)
        anew_sc[...] = jnp.broadcast_to(p_new[:half] - lam * p_new[half:], (half, LANES))
        acc_sc[...] = jnp.zeros(acc_sc.shape, F32)

    @pl.when(ph == 1)
    def _():
        a = a_sc[pg]
        pv = sum(jnp.dot(a[:, j * page:(j + 1) * page], token_rows(vc, v_order), preferred_element_type=F32)
                 for j, vc in enumerate(vc_refs))
        acc_sc[...] += own_head(pv)

    @pl.when(jnp.logical_and(ph == 1, pg == last))
    def _():
        a_new = anew_sc[...][:, :1]
        v_new = own_head(jnp.broadcast_to(vn_ref[0], (half, HK * vd)))
        o = acc_sc[...] + a_new * v_new
        ms = jnp.mean(o * o, axis=-1, keepdims=True)
        o_ref[0] = o * lax.rsqrt(ms + NORM_EPS) * subw_ref[...] * (1.0 - lam_init)


def _decode_diff_attn(q, k_new, v_new, cache_k, cache_v, page_table, lams, subln_w, *, HK, G, hd, vd, lam_init):
    DB, n_pages = page_table.shape
    n_phys, page = cache_k.shape[0], cache_k.shape[1]
    assert page % LANES == 0 and vd % LANES == 0 and hd == LANES
    kcw, vcw = HK * 2 * hd, HK * vd
    nk, nv = kcw // LANES, vcw // LANES
    kc = cache_k.reshape(n_phys * page * nk, LANES)
    vc = jnp.transpose(cache_v.reshape(n_phys, page, HK, vd // LANES, LANES),
                       (0, 1, 3, 2, 4)).reshape(n_phys * page * nv, LANES)
    NI = 2 * HK * G
    qt = jnp.transpose(q.reshape(DB, HK, G, 2, hd), (0, 3, 1, 2, 4))
    own = (jnp.eye(HK, dtype=q.dtype)[None, None, :, None, :, None, None]
           * jnp.eye(2, dtype=q.dtype)[None, :, None, None, None, :, None])
    qbd = (qt[:, :, :, :, None, None, :] * own).reshape(DB, NI, kcw)
    pps = math.gcd(n_pages, DECODE_PAGES_PER_STEP)
    steps = n_pages // pps
    per_b = lambda r, n: pl.BlockSpec((1, r, n), lambda b, ph, pg, pt: (b, 0, 0))
    vec = lambda n: pl.BlockSpec((1, n), lambda b, ph, pg, pt: (0, 0))

    def k_spec(j):
        return pl.BlockSpec((page * nk, LANES),
                            lambda b, ph, pg, pt: (pt[b, jnp.where(ph == 0, pg, steps - 1) * pps + j], 0))

    def v_spec(j):
        return pl.BlockSpec((page * nv, LANES),
                            lambda b, ph, pg, pt: (pt[b, jnp.where(ph == 1, pg, 0) * pps + j], 0))

    grid_spec = pltpu.PrefetchScalarGridSpec(
        num_scalar_prefetch=1,
        grid=(DB, 2, steps),
        in_specs=[per_b(NI, kcw), per_b(1, kcw), per_b(1, vcw)]
        + [k_spec(j) for j in range(pps)] + [v_spec(j) for j in range(pps)]
        + [vec(hd), vec(hd), vec(hd), vec(hd), vec(vd)],
        out_specs=pl.BlockSpec((1, NI // 2, vd), lambda b, ph, pg, pt: (b, 0, 0)),
        scratch_shapes=[pltpu.VMEM((steps, NI, pps * page), F32), pltpu.VMEM((steps, NI // 2, pps * page), BF16),
                        pltpu.VMEM((NI // 2, LANES), F32), pltpu.VMEM((NI // 2, vd), F32)],
    )
    out = pl.pallas_call(
        functools.partial(_decode_kernel, HK=HK, G=G, vd=vd, pps=pps, scale=hd ** -0.5, lam_init=lam_init),
        grid_spec=grid_spec,
        out_shape=jax.ShapeDtypeStruct((DB, NI // 2, vd), F32),
        compiler_params=_cparams("parallel", "arbitrary", "arbitrary"),
        name="decode_diff_attn",
    )(page_table, qbd, k_new.reshape(DB, 1, kcw), v_new.reshape(DB, 1, vcw), *([kc] * pps), *([vc] * pps),
      *[x.reshape(1, -1) for x in lams], subln_w.reshape(1, vd))
    return out.reshape(DB, HK * G * vd)


def _segsum_matrix(n, seg):
    r = lax.broadcasted_iota(jnp.int32, (n, n), 0) // seg
    c = lax.broadcasted_iota(jnp.int32, (n, n), 1) // seg
    return (r == c).astype(F32)


def _prep_kernel(*refs, halo, rh, exact, tc):
    if halo:
        (r_ref, k_ref, v_ref, l_ref, rp_ref, kp_ref, vp_ref, lp_ref, ri_ref, ki_ref, vi_ref, li_ref,
         mur_ref, muk_ref, muv_ref, mul_ref, w0_ref, w2_ref, a0_ref, a2_ref, g2_ref, kkw_ref, kaw_ref,
         ro_ref, lwo_ref, ko_ref, vo_ref, kko_ref, bo_ref, go_ref) = refs
    else:
        (r_ref, k_ref, v_ref, l_ref, rp_ref, kp_ref, vp_ref, lp_ref,
         mur_ref, muk_ref, muv_ref, mul_ref, w0_ref, w2_ref, a0_ref, a2_ref, g2_ref, kkw_ref, kaw_ref,
         ro_ref, lwo_ref, ko_ref, vo_ref, kko_ref, bo_ref, go_ref) = refs
    i = pl.program_id(0)

    def shifted(x_ref, prev_ref, init_ref):
        x = x_ref[...]
        if not halo:
            return x, prev_ref[...]
        last = prev_ref[SUBLANES - 1:SUBLANES, :]
        first = jnp.where(i == 0, init_ref[...], last)
        xs = pltpu.roll(x, 1, axis=0)
        rows = lax.broadcasted_iota(jnp.int32, x.shape, 0)
        return x, jnp.where(rows == 0, first, xs)

    def mix(x_ref, prev_ref, init_ref, mu_ref):
        x, xp = shifted(x_ref, prev_ref, init_ref)
        return x + (xp - x) * mu_ref[...]

    r = mix(r_ref, rp_ref, ri_ref if halo else None, mur_ref)
    k = mix(k_ref, kp_ref, ki_ref if halo else None, muk_ref)
    v = mix(v_ref, vp_ref, vi_ref if halo else None, muv_ref)
    lo = mix(l_ref, lp_ref, li_ref if halo else None, mul_ref)
    dl = w2_ref.shape[0]
    al = a2_ref.shape[0]
    xw = lo[:, :dl]
    xa = lo[:, dl:dl + al]
    xg = lo[:, dl + al:]
    wl = w0_ref[...] + _dg(jnp.tanh(xw), w2_ref[...], _NN, False)
    lw = -jnp.exp(-_softplus(-wl) - 0.5)
    a = _sigmoid(a0_ref[...] + _dg(xa, a2_ref[...], _NN, False))
    g = _dg(_sigmoid(xg), g2_ref[...], _NN, False)
    seg = _segsum_matrix(tc, rh)
    kk = k * kkw_ref[...]
    ss = _dg(kk * kk, seg, _NN, exact)
    kk = kk / jnp.maximum(jnp.sqrt(ss), 1e-12)
    kf = k * (1.0 + (a - 1.0) * kaw_ref[...])
    b = kk * a
    for pi in range(tc // LANES):
        sl = slice(pi * LANES, (pi + 1) * LANES)
        ro_ref[pi] = r[:, sl]
        lwo_ref[pi] = lw[:, sl]
        ko_ref[pi] = kf[:, sl]
        vo_ref[pi] = v[:, sl]
        kko_ref[pi] = kk[:, sl]
        bo_ref[pi] = b[:, sl]
        go_ref[pi] = g[:, sl]


def _rwkv_prep(rkv, lora, prev, mu_rkv, mu_lora, w0, w2, a0, a2, g2p, k_k, k_a, *, rh, exact, tt=256, tc=512):
    T, W3 = rkv.shape
    W = W3 // 3
    LP = lora.shape[1]
    tc = _pick(W, tc)
    nj = W // tc
    halo = prev[0].shape[0] != T or T == 1
    tt = _pick(T, tt, SUBLANES)
    hb = tt // SUBLANES
    xblk = lambda off: pl.BlockSpec((tt, tc), lambda i, j: (i, off * nj + j))
    pblk = lambda off: pl.BlockSpec((SUBLANES, tc), lambda i, j: (jnp.maximum(i * hb - 1, 0), off * nj + j))
    cvec = lambda off: pl.BlockSpec((1, tc), lambda i, j: (0, off * nj + j))
    in_specs = [xblk(0), xblk(1), xblk(2), pl.BlockSpec((tt, LP), lambda i, j: (i, 0))]
    args = [rkv, rkv, rkv, lora]
    if halo:
        in_specs += [pblk(0), pblk(1), pblk(2),
                     pl.BlockSpec((SUBLANES, LP), lambda i, j: (jnp.maximum(i * hb - 1, 0), 0)),
                     cvec(0), cvec(1), cvec(2), pl.BlockSpec((1, LP), lambda i, j: (0, 0))]
        args += [rkv, rkv, rkv, lora, prev[0], prev[0], prev[0], prev[1]]
    else:
        in_specs += [xblk(0), xblk(1), xblk(2), pl.BlockSpec((tt, LP), lambda i, j: (i, 0))]
        args += [prev[0], prev[0], prev[0], prev[1]]
    in_specs += [cvec(0), cvec(1), cvec(2), pl.BlockSpec((1, LP), lambda i, j: (0, 0)),
                 cvec(0), pl.BlockSpec((w2.shape[0], tc), lambda i, j: (0, j)),
                 cvec(0), pl.BlockSpec((a2.shape[0], tc), lambda i, j: (0, j)),
                 pl.BlockSpec((g2p.shape[0], tc), lambda i, j: (0, j)),
                 cvec(0), cvec(0)]
    args += [mu_rkv, mu_rkv, mu_rkv, mu_lora, w0.reshape(1, W), w2, a0.reshape(1, W), a2, g2p,
             k_k.reshape(1, W), k_a.reshape(1, W)]
    npair = tc // LANES
    oshape = jax.ShapeDtypeStruct((W // LANES, T, LANES), F32)
    ospec = pl.BlockSpec((npair, tt, LANES), lambda i, j: (j, i, 0))
    return pl.pallas_call(
        functools.partial(_prep_kernel, halo=halo, rh=rh, exact=exact, tc=tc),
        grid=(T // tt, nj),
        in_specs=in_specs,
        out_specs=[ospec] * 7,
        out_shape=[oshape] * 7,
        compiler_params=_cparams("parallel", "arbitrary"),
        name="rwkv_prep",
    )(*args)


def _scan_kernel(r_ref, lw_ref, k_ref, v_ref, kk_ref, b_ref, g_ref, s0_ref, rk_ref, lnw_ref, lnb_ref,
                 y_ref, sout_ref, s_sc, *, C, rh, npair, precise, t_real):
    c = pl.program_id(1)
    R = 2 * C
    lane = lax.broadcasted_iota(jnp.int32, (C, LANES), 1)
    head0 = lane < rh
    ri = lax.broadcasted_iota(jnp.int32, (R, R), 0)
    ci = lax.broadcasted_iota(jnp.int32, (R, R), 1)
    same = (ri // C) == (ci // C)
    strict = jnp.logical_and(same, (ci % C) < (ri % C))
    incl = jnp.logical_and(same, (ci % C) <= (ri % C))
    tril_c = (lax.broadcasted_iota(jnp.int32, (C, C), 1)
              <= lax.broadcasted_iota(jnp.int32, (C, C), 0)).astype(F32)
    seg = _segsum_matrix(LANES, rh)
    same_head = seg > 0.5
    zero = jnp.zeros((rh, rh), F32)

    def stack_masked(x):
        return jnp.concatenate([jnp.where(head0, x, 0.0), jnp.where(head0, 0.0, x)], axis=0)

    def stack_dup(x):
        return jnp.concatenate([x, x], axis=0)

    @pl.when(c == 0)
    def _():
        def init(p, carry):
            top = jnp.concatenate([s0_ref[0, 2 * p], zero], axis=1)
            bot = jnp.concatenate([zero, s0_ref[0, 2 * p + 1]], axis=1)
            s_sc[p] = jnp.concatenate([top, bot], axis=0)
            return carry
        lax.fori_loop(0, npair, init, 0)

    def single_step(r, lw, k, v, kk, b, S):
        sa = yield -kk, S, _NT, False
        o1 = yield sa, b, _TN, True
        o2 = yield v, k, _TN, True
        s_new = S * jnp.exp(lw[0:1, :]) + jnp.where(same_head, o1 + o2, 0.0)
        y = yield r, s_new, _NT, False
        return y, s_new

    def chunk_steps(r, lw, k, v, kk, b, S):
        cum = yield tril_c, lw, _NN, True
        tot = cum[C - 1:C, :]
        e_incl = jnp.exp(cum)
        e_inv = jnp.exp(-cum)
        e_end = jnp.exp(tot - cum)
        a2 = stack_masked(-kk * jnp.exp(cum - lw))
        r2 = stack_masked(r * e_incl)
        v2 = stack_masked(v)
        bd2 = stack_masked(b * e_end)
        kd2 = stack_masked(k * e_end)
        b2 = stack_dup(b * e_inv)
        k2 = stack_dup(k * e_inv)
        ar = jnp.concatenate([a2, r2], axis=0)
        arb = yield ar, b2, _NT, precise
        ark = yield ar, k2, _NT, precise
        ars = yield ar, S, _NT, precise
        mrb = jnp.where(incl, arb[R:], 0.0)
        mrk = jnp.where(incl, ark[R:], 0.0)
        x = ars[:R]
        n_it = (t_real - 1).bit_length()
        if n_it:
            x = x + (yield jnp.where(strict, ark[:R], 0.0), v2, _NN, precise)
            pw = jnp.where(strict, arb[:R], 0.0)
        for it in range(n_it):
            x = x + (yield pw, x, _NN, precise)
            if it + 1 < n_it:
                pw = yield pw, pw, _NN, precise
        y2 = ars[R:] + (yield mrb, x, _NN, precise)
        y2 = y2 + (yield mrk, v2, _NN, precise)
        y = y2[:C] + y2[C:]
        s_new = S * jnp.exp(tot) + (yield x, bd2, _TN, precise)
        s_new = s_new + (yield v2, kd2, _TN, precise)
        return y, s_new

    def slab(r, lw, k, v, kk, b, g, S, rk, lnw, lnb):
        steps = single_step if t_real == 1 else chunk_steps
        y, s_new = yield from steps(r, lw, k, v, kk, b, S)
        mean = (yield y, seg, _NN, precise) * (1.0 / rh)
        d = y - mean
        var = (yield d * d, seg, _NN, precise) * (1.0 / rh)
        yn = d * lax.rsqrt(var + GN_EPS) * lnw + lnb
        bonus = (yield r * k * rk, seg, _NN, precise) * v
        return (yn + bonus) * g, s_new

    def run_lockstep(gens):
        reqs = [next(gen) for gen in gens]
        results = [None] * len(gens)
        live = list(range(len(gens)))
        while live:
            vals = [_dg(*reqs[i]) for i in live]
            nxt = []
            for i, val in zip(live, vals):
                try:
                    reqs[i] = gens[i].send(val)
                    nxt.append(i)
                except StopIteration as done:
                    results[i] = done.value
            live = nxt
        return results

    U = math.gcd(npair, SLABS_PER_GROUP)

    def body(i, carry):
        grp = pl.ds(pl.multiple_of(i * U, U), U)
        ins = [ref[grp] for ref in (r_ref, lw_ref, k_ref, v_ref, kk_ref, b_ref, g_ref, s_sc,
                                    rk_ref, lnw_ref, lnb_ref)]
        outs = run_lockstep([slab(*[a[u] for a in ins]) for u in range(U)])
        y_ref[grp] = jnp.stack([o[0] for o in outs]).astype(y_ref.dtype)
        s_sc[grp] = jnp.stack([o[1] for o in outs])
        return carry

    lax.fori_loop(0, npair // U, body, 0)

    @pl.when(c == pl.num_programs(1) - 1)
    def _():
        def fin(p, carry):
            S = s_sc[p]
            sout_ref[0, 2 * p] = S[:rh, :rh]
            sout_ref[0, 2 * p + 1] = S[rh:, rh:]
            return carry
        lax.fori_loop(0, npair, fin, 0)


def _rwkv_scan(prep, s0, r_k, ln_w, ln_b, *, B, C, rh, precise, out_dtype, t_real):
    npair, BT, _ = prep[0].shape
    Tb = BT // B
    nc = Tb // C
    H = s0.shape[1]
    xspec = pl.BlockSpec((npair, C, LANES), lambda b, c: (0, b * nc + c, 0))
    sspec = pl.BlockSpec((1, H, rh, rh), lambda b, c: (b, 0, 0, 0))
    pspec = pl.BlockSpec((npair, 1, LANES), lambda b, c: (0, 0, 0))
    y, s_out = pl.pallas_call(
        functools.partial(_scan_kernel, C=C, rh=rh, npair=npair, precise=precise, t_real=t_real),
        grid=(B, nc),
        in_specs=[xspec] * 7 + [sspec, pspec, pspec, pspec],
        out_specs=[xspec, sspec],
        out_shape=[jax.ShapeDtypeStruct((npair, BT, LANES), out_dtype),
                   jax.ShapeDtypeStruct((B, H, rh, rh), F32)],
        scratch_shapes=[pltpu.VMEM((npair, 2 * rh, 2 * rh), F32)],
        compiler_params=_cparams("parallel", "arbitrary"),
        name="rwkv_scan",
    )(*prep, s0, r_k.reshape(npair, 1, LANES), ln_w.reshape(npair, 1, LANES), ln_b.reshape(npair, 1, LANES))
    return y, s_out


def _router_kernel(x_ref, nw_ref, rw_ref, rb_ref, h_ref, lg_ref, *, precise):
    x = x_ref[...]
    ms = jnp.mean(x * x, axis=-1, keepdims=True)
    h = x * lax.rsqrt(ms + NORM_EPS) * nw_ref[...]
    h_ref[...] = h.astype(h_ref.dtype)
    lg_ref[...] = _dg(h, rw_ref[...], _NN, precise) + rb_ref[...]


def _router(x, norm_w, rw, rb, *, precise, h_dtype, tm=512):
    M, D = x.shape
    NR = rw.shape[1]
    tm = _pick(M, tm, SUBLANES)
    return pl.pallas_call(
        functools.partial(_router_kernel, precise=precise),
        grid=(M // tm,),
        in_specs=[pl.BlockSpec((tm, D), lambda i: (i, 0)), pl.BlockSpec((1, D), lambda i: (0, 0)),
                  pl.BlockSpec((D, NR), lambda i: (0, 0)), pl.BlockSpec((1, NR), lambda i: (0, 0))],
        out_specs=[pl.BlockSpec((tm, D), lambda i: (i, 0)), pl.BlockSpec((tm, NR), lambda i: (i, 0))],
        out_shape=[jax.ShapeDtypeStruct((M, D), h_dtype), jax.ShapeDtypeStruct((M, NR), F32)],
        compiler_params=_cparams("parallel"),
        name="moe_router",
    )(x, norm_w.reshape(1, D), rw, rb)


def _moe_kernel(be_ref, nu_ref, xr_ref, x_ref, wg_ref, wu_ref, wd_ref, sw_ref, o_ref):
    b = pl.program_id(0)
    j = pl.program_id(1)
    last = j == pl.num_programs(1) - 1
    used = b < nu_ref[0]

    @pl.when(jnp.logical_and(used, j == 0))
    def _():
        o_ref[...] = jnp.zeros(o_ref.shape, F32)

    @pl.when(used)
    def _():
        x = x_ref[...]
        tn = wg_ref.shape[2]
        wgu = jnp.concatenate([wg_ref[0].astype(BF16), wu_ref[0].astype(BF16)], axis=1)
        hgu = jnp.dot(x, wgu, preferred_element_type=F32)
        hg, hu = hgu[:, :tn], hgu[:, tn:]
        h = hg * _sigmoid(hg) * hu
        o_ref[...] += jnp.dot(h.astype(BF16), wd_ref[0].astype(BF16), preferred_element_type=F32)

    @pl.when(jnp.logical_and(used, last))
    def _():
        o_ref[...] = o_ref[...] * sw_ref[...]

    @pl.when(jnp.logical_and(jnp.logical_not(used), last))
    def _():
        o_ref[...] = jnp.zeros(o_ref.shape, F32)


def _moe_experts(xs, x_row0, blk_e, n_used, slot_w, wg, wu, wd, *, blk, tn):
    D = xs.shape[1]
    P = slot_w.shape[0]
    DE = wg.shape[2]
    tn = _pick(DE, tn)
    nj = DE // tn
    nb = P // blk

    def eff(b, j, nu):
        live = b < nu[0]
        return jnp.where(live, b, nu[0] - 1), jnp.where(live, j, nj - 1)

    def wg_map(b, j, be, nu, xr):
        bb, jj = eff(b, j, nu)
        return be[bb], 0, jj

    def wd_map(b, j, be, nu, xr):
        bb, jj = eff(b, j, nu)
        return be[bb], jj, 0

    grid_spec = pltpu.PrefetchScalarGridSpec(
        num_scalar_prefetch=3,
        grid=(nb, nj),
        in_specs=[pl.BlockSpec((pl.Element(blk), pl.Element(D)),
                               lambda b, j, be, nu, xr: (pl.multiple_of(xr[eff(b, j, nu)[0]], 2 * SUBLANES), 0)),
                  pl.BlockSpec((1, D, tn), wg_map), pl.BlockSpec((1, D, tn), wg_map),
                  pl.BlockSpec((1, tn, D), wd_map),
                  pl.BlockSpec((blk, 1), lambda b, j, be, nu, xr: (eff(b, j, nu)[0], 0))],
        out_specs=pl.BlockSpec((blk, D), lambda b, j, be, nu, xr: (b, 0)),
    )
    return pl.pallas_call(
        _moe_kernel,
        grid_spec=grid_spec,
        out_shape=jax.ShapeDtypeStruct((P, D), F32),
        compiler_params=_cparams("arbitrary", "arbitrary"),
        name="moe_experts",
    )(blk_e, n_used, x_row0, xs, wg, wu, wd, slot_w.reshape(P, 1))


def _hier_moe(x_groups, norm_w, rg_w, rg_b, re_w, re_b, wg, wu, wd, *, blk, tn):
    D = x_groups[0].shape[1]
    NG = rg_w.shape[1]
    E = re_w.shape[1]
    EPG = E // NG
    NR = -(-(NG + E) // LANES) * LANES
    rw = jnp.pad(jnp.concatenate([rg_w, re_w], axis=1), ((0, 0), (0, NR - NG - E))).astype(BF16)
    rb = jnp.pad(jnp.concatenate([rg_b, re_b]), (0, NR - NG - E)).reshape(1, NR)
    routed = [_router(x, norm_w, rw, rb, precise=False, h_dtype=BF16) for x in x_groups]
    logits = jnp.concatenate([r[1] for r in routed], axis=0)
    T = logits.shape[0]
    gprob = jax.nn.softmax(logits[:, :NG], axis=-1)
    gsel = jnp.argmax(gprob, axis=-1)
    gp = jnp.take_along_axis(gprob, gsel[:, None], axis=-1)[:, 0]
    elog = logits[:, NG:NG + E].reshape(T, NG, EPG)
    elog = jnp.take_along_axis(elog, gsel[:, None, None], axis=1)[:, 0]
    top_p, top_i = lax.top_k(jax.nn.softmax(elog, axis=-1), TOP_K)
    top_p = top_p / jnp.sum(top_p, axis=-1, keepdims=True)
    weights = gp[:, None] * top_p
    expert_idx = gsel[:, None].astype(jnp.int32) * EPG + top_i.astype(jnp.int32)

    TK = T * TOP_K
    nb = -(-TK // blk) + E
    P = nb * blk
    flat_e = expert_idx.reshape(-1)
    flat_w = weights.reshape(-1)
    flat_t = jnp.repeat(jnp.arange(T, dtype=jnp.int32), TOP_K)
    order = jnp.argsort(flat_e)
    se = flat_e[order]
    counts = jnp.bincount(flat_e, length=E)
    pcounts = (counts + blk - 1) // blk * blk
    pend = jnp.cumsum(pcounts)
    pstart = pend - pcounts
    start = jnp.cumsum(counts) - counts
    dest = (pstart[se] + jnp.arange(TK) - start[se]).astype(jnp.int32)
    slot_w = jnp.zeros((P,), F32).at[dest].set(flat_w[order])
    seg = 2 * SUBLANES
    ccounts = (counts + seg - 1) // seg * seg
    cstart = jnp.cumsum(ccounts) - ccounts
    n_rows = -(-(TK + seg * E) // seg) * seg + blk
    cdest = (cstart[se] + jnp.arange(TK) - start[se]).astype(jnp.int32)
    slot_t = jnp.full((n_rows,), T, jnp.int32).at[cdest].set(flat_t[order])
    blk_e = jnp.minimum(jnp.searchsorted(pend, jnp.arange(nb) * blk, side='right'), E - 1).astype(jnp.int32)
    n_used = (pend[-1] // blk).astype(jnp.int32).reshape(1)
    pos = jnp.zeros((TK,), jnp.int32).at[order].set(dest)

    x_row0 = (cstart[blk_e] + (jnp.arange(nb, dtype=jnp.int32) - pstart[blk_e] // blk) * blk).astype(jnp.int32)
    x_row0 = jnp.clip(x_row0, 0, n_rows - blk)
    xs = jnp.concatenate([r[0] for r in routed] + [jnp.zeros((1, D), BF16)], axis=0)[slot_t]
    ys = _moe_experts(xs, x_row0, blk_e, n_used, slot_w, wg, wu, wd, blk=blk, tn=tn)
    pos = pos.reshape(T, TOP_K)
    outs, t0 = [], 0
    for x in x_groups:
        pg = pos[t0:t0 + x.shape[0]]
        outs.append(x + ys[pg[:, 0]] + ys[pg[:, 1]])
        t0 += x.shape[0]
    return outs


def _pairs_to_rows(y):
    npair, T, _ = y.shape
    return jnp.transpose(y, (1, 0, 2)).reshape(T, npair * LANES)


def _mixer(x, positions, prev_rows, s0, B, cfg, w, *, exact, attn_fn, q_scale, emit_bf16):
    T, D = x.shape
    hd, HK, G, vd, rh = cfg['hd'], cfg['HK'], cfg['G'], cfg['vd'], cfg['rh']
    qcols, kcols, vcols = HK * G * 2 * hd, HK * 2 * hd, HK * vd
    W = cfg['W']
    mm = functools.partial(_matmul, precise=False)

    h = _rmsnorm(x, w['norm1_w'], BF16)
    qkv = mm(h, w['w_qkv'], qcols + kcols + vcols)
    rkv = mm(h, w['w_rkv'], 3 * W)
    lora = mm(h, w['w_lora'], cfg['lp'], tn=cfg['lp'])
    gates = mm(h, w['w_gates'], 2 * D)

    cos_full, sin_signed = _rope_tables(positions, hd)
    outs = _qk_rope(qkv, cos_full, sin_signed, w['q_norm_w'], w['k_norm_w'], qcols, kcols, vcols,
                    q_scale=q_scale, q_dtype=BF16, emit_bf16=emit_bf16)
    q, k_new, v_new = outs[:3]
    o_attn = attn_fn(q, outs)

    prep = _rwkv_prep(rkv, lora, prev_rows, w['mu_rkv'], w['mu_lora'], w['rwkv_w0'], w['rwkv_w2'],
                      w['rwkv_a0'], w['rwkv_a2'], w['g2p'], w['rwkv_k_k'], w['rwkv_k_a'],
                      rh=rh, exact=exact)
    Tb = T // B
    if Tb % 64 == 0:
        C = 64
    else:
        C = SUBLANES
        pad = -Tb % C
        prep = [jnp.pad(a.reshape(-1, B, Tb, LANES), ((0, 0), (0, 0), (0, pad), (0, 0)))
                .reshape(a.shape[0], B * (Tb + pad), LANES) for a in prep]
    y_r, s_fin = _rwkv_scan(prep, s0, w['rwkv_r_k'], w['rwkv_ln_w'], w['rwkv_ln_b'],
                            B=B, C=C, rh=rh, precise=exact, out_dtype=BF16, t_real=min(Tb, C))
    if Tb % 64:
        y_r = y_r.reshape(y_r.shape[0], B, -1, LANES)[:, :, :Tb].reshape(y_r.shape[0], T, LANES)
    o_rwkv = _pairs_to_rows(y_r)

    mixed = _merge(o_attn, w['w_branch_attn'], o_rwkv, w['w_branch_rwkv'], gates, w['gate_b'],
                   precise=False, out_dtype=BF16)
    x1 = _matmul_residual(mixed, w['w_out'], x, precise=False)
    return x1, k_new, v_new, s_fin, rkv, lora


def kernel(x_prompt, x_sample, cache_k, cache_v, state_rwkv, state_shift, page_table, norm1_w, w_in, gate_b, q_norm_w, k_norm_w, lambda_q1, lambda_k1, lambda_q2, lambda_k2, subln_w, rwkv_mu, rwkv_w0, rwkv_w2, rwkv_a0, rwkv_a2, rwkv_g2, rwkv_k_k, rwkv_k_a, rwkv_r_k, rwkv_ln_w, rwkv_ln_b, w_branch_attn, w_branch_rwkv, w_out, norm2_w, router_group_w, router_group_b, router_expert_w, router_expert_b, expert_w_gate, expert_w_up, expert_w_down):
    B, T, D = x_prompt.shape
    DB, S, _ = x_sample.shape
    L = norm1_w.shape[0]
    assert L == 1 and B == 1 and S == 1, "one trunk layer, one prompt sequence, one new token per sample"
    hd = q_norm_w.shape[1]
    HK = cache_k.shape[3]
    vd = cache_v.shape[4]
    G = w_branch_attn.shape[1] // vd // HK
    H, rh = rwkv_r_k.shape[1], rwkv_r_k.shape[2]
    W = H * rh
    dl, al, gl = rwkv_w2.shape[1], rwkv_a2.shape[1], rwkv_g2.shape[1]
    lcols = dl + al + gl
    lp = -(-lcols // LANES) * LANES
    qkv_cols = HK * G * 2 * hd + HK * 2 * hd + HK * vd
    r_off = qkv_cols
    g_off = r_off + 3 * W + lcols
    n_pages, page = page_table.shape[1], cache_k.shape[2]
    past = n_pages * page
    cfg = dict(hd=hd, HK=HK, G=G, vd=vd, rh=rh, W=W, lp=lp, r_off=r_off)
    l = 0
    (cache_k, cache_v, state_rwkv, state_shift, norm1_w, w_in, gate_b, q_norm_w, k_norm_w, lambda_q1,
     lambda_k1, lambda_q2, lambda_k2, subln_w, rwkv_mu, rwkv_w0, rwkv_w2, rwkv_a0, rwkv_a2, rwkv_g2,
     rwkv_k_k, rwkv_k_a, rwkv_r_k, rwkv_ln_w, rwkv_ln_b, w_branch_attn, w_branch_rwkv, w_out, norm2_w,
     router_group_w, router_group_b, router_expert_w, router_expert_b, expert_w_gate, expert_w_up,
     expert_w_down) = [a.reshape(a.shape[1:]) for a in (
         cache_k, cache_v, state_rwkv, state_shift, norm1_w, w_in, gate_b, q_norm_w, k_norm_w, lambda_q1,
         lambda_k1, lambda_q2, lambda_k2, subln_w, rwkv_mu, rwkv_w0, rwkv_w2, rwkv_a0, rwkv_a2, rwkv_g2,
         rwkv_k_k, rwkv_k_a, rwkv_r_k, rwkv_ln_w, rwkv_ln_b, w_branch_attn, w_branch_rwkv, w_out, norm2_w,
         router_group_w, router_group_b, router_expert_w, router_expert_b, expert_w_gate, expert_w_up,
         expert_w_down)]
    lam_init = 0.8 - 0.6 * math.exp(-0.3 * l)
    lams = (lambda_q1, lambda_k1, lambda_q2, lambda_k2)

    mu = rwkv_mu
    g2p = jnp.pad(rwkv_g2, ((0, lp - lcols), (0, 0)))
    w_lora = jnp.pad(w_in[:, r_off + 3 * W:r_off + 3 * W + lcols], ((0, 0), (0, lp - lcols)))
    w = dict(
        norm1_w=norm1_w, q_norm_w=q_norm_w, k_norm_w=k_norm_w, gate_b=gate_b,
        mu_rkv=mu[:3 * W].reshape(1, 3 * W), mu_lora=jnp.pad(mu[3 * W:], (0, lp - lcols)).reshape(1, lp),
        rwkv_w0=rwkv_w0, rwkv_a0=rwkv_a0, rwkv_k_k=rwkv_k_k, rwkv_k_a=rwkv_k_a,
        rwkv_r_k=rwkv_r_k, rwkv_ln_w=rwkv_ln_w, rwkv_ln_b=rwkv_ln_b,
        norm2_w=norm2_w, router_group_w=router_group_w, router_group_b=router_group_b,
        router_expert_w=router_expert_w, router_expert_b=router_expert_b,
        expert_w_gate=expert_w_gate, expert_w_up=expert_w_up, expert_w_down=expert_w_down,
        w_qkv=w_in[:, :qkv_cols].astype(BF16), w_rkv=w_in[:, r_off:r_off + 3 * W].astype(BF16),
        w_lora=w_lora.astype(BF16), w_gates=w_in[:, g_off:g_off + 2 * D].astype(BF16),
        rwkv_w2=rwkv_w2.astype(BF16), rwkv_a2=rwkv_a2.astype(BF16), g2p=g2p.astype(BF16),
        w_branch_attn=w_branch_attn.astype(BF16), w_branch_rwkv=w_branch_rwkv.astype(BF16),
        w_out=w_out.astype(BF16))

    def prompt_attn(q, outs):
        return _flash_diff_attn(q, outs[3], outs[4], lams, subln_w, HK=HK, G=G, hd=hd, vd=vd,
                                lam_init=lam_init, out_dtype=BF16)

    zero_prev = (jnp.zeros((1, 3 * W), F32), jnp.zeros((1, lp), F32))
    xp1, kp, vp, sp, rkv_p, lora_p = _mixer(
        x_prompt.reshape(T, D), jnp.arange(T), zero_prev, jnp.zeros((1, H, rh, rh), F32), 1, cfg, w,
        exact=False, attn_fn=prompt_attn, q_scale=hd ** -0.5 * math.log2(math.e), emit_bf16=True)
    shift_p = jnp.concatenate([rkv_p[T - 1], lora_p[T - 1, :lcols]])

    def sample_attn(q, outs):
        return _decode_diff_attn(q, outs[1], outs[2], cache_k, cache_v, page_table, lams, subln_w,
                                 HK=HK, G=G, hd=hd, vd=vd, lam_init=lam_init)

    sh = state_shift
    prev_s = (sh[:, :3 * W], jnp.pad(sh[:, 3 * W:], ((0, 0), (0, lp - lcols))))
    xs1, ks, vs, ss, rkv_s, lora_s = _mixer(
        x_sample.reshape(DB, D), jnp.full((DB,), past, jnp.int32), prev_s, state_rwkv, DB, cfg, w,
        exact=True, attn_fn=sample_attn, q_scale=1.0, emit_bf16=False)
    shift_s = jnp.concatenate([rkv_s, lora_s[:, :lcols]], axis=1)


    yp, ys = _hier_moe([xp1, xs1], norm2_w, router_group_w, router_group_b, router_expert_w,
                       router_expert_b, expert_w_gate, expert_w_up, expert_w_down,
                       blk=min(MOE_ROW_BLOCK, max(2 * SUBLANES, T // 16)), tn=MOE_HIDDEN_TILE)

    return (yp.reshape(1, T, D), ys.reshape(DB, 1, D),
            kp.reshape(1, 1, T, HK, 2, hd), vp.reshape(1, 1, T, HK, vd),
            sp.reshape(1, 1, H, rh, rh), shift_p.reshape(1, 1, -1),
            ks.reshape(1, DB, 1, HK, 2, hd), vs.reshape(1, DB, 1, HK, vd),
            ss.reshape(1, DB, H, rh, rh), shift_s.reshape(1, DB, -1))
```

```python
import functools
import math

import jax
import jax.numpy as jnp
from jax import lax
from jax.experimental import pallas as pl
from jax.experimental.pallas import tpu as pltpu

F32 = jnp.float32
BF16 = jnp.bfloat16
HIGHEST = lax.Precision.HIGHEST

LANES = 128
SUBLANES = 8
VMEM_LIMIT = 56 * 1024 * 1024

ROPE_THETA = 10000.0
NORM_EPS = 1e-6
GN_EPS = 64e-5
TOP_K = 2
SLABS_PER_GROUP = 16
DECODE_PAGES_PER_STEP = 8
MOE_ROW_BLOCK = 512
MOE_HIDDEN_TILE = 256

_NN = (((1,), (0,)), ((), ()))
_NT = (((1,), (1,)), ((), ()))
_TN = (((0,), (0,)), ((), ()))


def _cparams(*sem):
    return pltpu.CompilerParams(dimension_semantics=sem, vmem_limit_bytes=VMEM_LIMIT)


def _pick(n, pref, unit=LANES):
    if n <= pref:
        return n
    t = pref // unit * unit
    while t > unit and n % t:
        t -= unit
    assert n % t == 0, (n, pref)
    return t


def _dg(a, b, dn, precise):
    if precise:
        return lax.dot_general(a.astype(F32), b.astype(F32), dn, precision=HIGHEST,
                               preferred_element_type=F32)
    return lax.dot_general(a.astype(BF16), b.astype(BF16), dn, preferred_element_type=F32)


def _sigmoid(x):
    return 1.0 / (1.0 + jnp.exp(-x))


def _softplus(x):
    return jnp.maximum(x, 0.0) + jnp.log(1.0 + jnp.exp(-jnp.abs(x)))


def _rms_kernel(x_ref, w_ref, o_ref, *, eps):
    x = x_ref[...]
    ms = jnp.mean(x * x, axis=-1, keepdims=True)
    o_ref[...] = (x * lax.rsqrt(ms + eps) * w_ref[...]).astype(o_ref.dtype)


def _rmsnorm(x, w, out_dtype, tm=512):
    M, D = x.shape
    tm = _pick(M, tm, SUBLANES)
    return pl.pallas_call(
        functools.partial(_rms_kernel, eps=NORM_EPS),
        grid=(M // tm,),
        in_specs=[pl.BlockSpec((tm, D), lambda i: (i, 0)),
                  pl.BlockSpec((1, D), lambda i: (0, 0))],
        out_specs=pl.BlockSpec((tm, D), lambda i: (i, 0)),
        out_shape=jax.ShapeDtypeStruct((M, D), out_dtype),
        compiler_params=_cparams("parallel"),
        name="rmsnorm",
    )(x, w.reshape(1, D))


def _mm_kernel(a_ref, b_ref, o_ref, *, precise):
    o_ref[...] = _dg(a_ref[...], b_ref[...], _NN, precise).astype(o_ref.dtype)


def _matmul(a, b, n, col0=0, *, precise, out_dtype=F32, tm=1024, tn=512):
    M, K = a.shape
    tm = _pick(M, tm, SUBLANES)
    tn = _pick(n, tn)
    if col0 % tn:
        b, col0 = b[:, col0:col0 + n], 0
    c0 = col0 // tn
    return pl.pallas_call(
        functools.partial(_mm_kernel, precise=precise),
        grid=(M // tm, n // tn),
        in_specs=[pl.BlockSpec((tm, K), lambda i, j: (i, 0)),
                  pl.BlockSpec((K, tn), lambda i, j: (0, c0 + j))],
        out_specs=pl.BlockSpec((tm, tn), lambda i, j: (i, j)),
        out_shape=jax.ShapeDtypeStruct((M, n), out_dtype),
        compiler_params=_cparams("parallel", "arbitrary"),
        name="matmul",
    )(a, b)


def _merge_kernel(a1_ref, b1_ref, a2_ref, b2_ref, g1_ref, g2_ref, gb1_ref, gb2_ref, o_ref, *, precise):
    o1 = _dg(a1_ref[...], b1_ref[...], _NN, precise)
    o2 = _dg(a2_ref[...], b2_ref[...], _NN, precise)
    s1 = _sigmoid(g1_ref[...] + gb1_ref[...])
    s2 = _sigmoid(g2_ref[...] + gb2_ref[...])
    o_ref[...] = (s1 * o1 + s2 * o2).astype(o_ref.dtype)


def _merge(a1, b1, a2, b2, gates, gate_b, *, precise, out_dtype, tm=512, tn=512):
    M, K = a1.shape
    K2 = a2.shape[1]
    D = b1.shape[1]
    tm = _pick(M, tm, SUBLANES)
    tn = _pick(D, tn)
    nj = D // tn
    gb = gate_b.reshape(1, 2 * D)
    return pl.pallas_call(
        functools.partial(_merge_kernel, precise=precise),
        grid=(M // tm, nj),
        in_specs=[pl.BlockSpec((tm, K), lambda i, j: (i, 0)),
                  pl.BlockSpec((K, tn), lambda i, j: (0, j)),
                  pl.BlockSpec((tm, K2), lambda i, j: (i, 0)),
                  pl.BlockSpec((K2, tn), lambda i, j: (0, j)),
                  pl.BlockSpec((tm, tn), lambda i, j: (i, j)),
                  pl.BlockSpec((tm, tn), lambda i, j: (i, nj + j)),
                  pl.BlockSpec((1, tn), lambda i, j: (0, j)),
                  pl.BlockSpec((1, tn), lambda i, j: (0, nj + j))],
        out_specs=pl.BlockSpec((tm, tn), lambda i, j: (i, j)),
        out_shape=jax.ShapeDtypeStruct((M, D), out_dtype),
        compiler_params=_cparams("parallel", "arbitrary"),
        name="merge",
    )(a1, b1, a2, b2, gates, gates, gb, gb)


def _mm_res_kernel(a_ref, b_ref, r_ref, o_ref, *, precise):
    o_ref[...] = r_ref[...] + _dg(a_ref[...], b_ref[...], _NN, precise)


def _matmul_residual(a, b, res, *, precise, tm=1024, tn=512):
    M, K = a.shape
    N = b.shape[1]
    tm = _pick(M, tm, SUBLANES)
    tn = _pick(N, tn)
    return pl.pallas_call(
        functools.partial(_mm_res_kernel, precise=precise),
        grid=(M // tm, N // tn),
        in_specs=[pl.BlockSpec((tm, K), lambda i, j: (i, 0)),
                  pl.BlockSpec((K, tn), lambda i, j: (0, j)),
                  pl.BlockSpec((tm, tn), lambda i, j: (i, j))],
        out_specs=pl.BlockSpec((tm, tn), lambda i, j: (i, j)),
        out_shape=jax.ShapeDtypeStruct((M, N), F32),
        compiler_params=_cparams("parallel", "arbitrary"),
        name="matmul_residual",
    )(a, b, res)


def _qkrope_kernel(qkv_ref, cos_ref, sin_ref, qw_ref, kw_ref, *out_refs, nq, nk, hd, vcols, scale, emit_bf16):
    if emit_bf16:
        q_ref, k_ref, v_ref, kb_ref, vb_ref = out_refs
    else:
        q_ref, k_ref, v_ref = out_refs
    cos = cos_ref[...]
    sin = sin_ref[...]

    def norm_rope(x, w):
        ms = jnp.mean(x * x, axis=-1, keepdims=True)
        xn = x * lax.rsqrt(ms + NORM_EPS) * w
        return xn * cos + pltpu.roll(xn, hd // 2, axis=1) * sin

    qw = qw_ref[...]
    kw = kw_ref[...]
    for c in range(nq):
        o = norm_rope(qkv_ref[:, c * hd:(c + 1) * hd], qw) * scale
        q_ref[:, c * hd:(c + 1) * hd] = o.astype(q_ref.dtype)
    for c in range(nk):
        o = norm_rope(qkv_ref[:, (nq + c) * hd:(nq + c + 1) * hd], kw)
        k_ref[:, c * hd:(c + 1) * hd] = o
        if emit_bf16:
            kb_ref[:, c * hd:(c + 1) * hd] = o.astype(BF16)
    v = qkv_ref[:, (nq + nk) * hd:(nq + nk) * hd + vcols]
    v_ref[...] = v
    if emit_bf16:
        vb_ref[...] = v.astype(BF16)


def _qk_rope(qkv, cos_full, sin_signed, q_norm_w, k_norm_w, qcols, kcols, vcols, *, q_scale, q_dtype,
             emit_bf16, tq=256):
    T = qkv.shape[0]
    hd = q_norm_w.shape[0]
    tq = _pick(T, tq, SUBLANES)
    nq, nk = qcols // hd, kcols // hd
    row = lambda i: (i, 0)
    out_shape = [jax.ShapeDtypeStruct((T, qcols), q_dtype),
                 jax.ShapeDtypeStruct((T, kcols), F32),
                 jax.ShapeDtypeStruct((T, vcols), F32)]
    out_specs = [pl.BlockSpec((tq, qcols), row), pl.BlockSpec((tq, kcols), row), pl.BlockSpec((tq, vcols), row)]
    if emit_bf16:
        out_shape += [jax.ShapeDtypeStruct((T, kcols), BF16), jax.ShapeDtypeStruct((T, vcols), BF16)]
        out_specs += [pl.BlockSpec((tq, kcols), row), pl.BlockSpec((tq, vcols), row)]
    return pl.pallas_call(
        functools.partial(_qkrope_kernel, nq=nq, nk=nk, hd=hd, vcols=vcols, scale=q_scale,
                          emit_bf16=emit_bf16),
        grid=(T // tq,),
        in_specs=[pl.BlockSpec((tq, qkv.shape[1]), row),
                  pl.BlockSpec((tq, hd), row), pl.BlockSpec((tq, hd), row),
                  pl.BlockSpec((1, hd), lambda i: (0, 0)), pl.BlockSpec((1, hd), lambda i: (0, 0))],
        out_specs=out_specs,
        out_shape=out_shape,
        compiler_params=_cparams("parallel"),
        name="qk_norm_rope",
    )(qkv, cos_full, sin_signed, q_norm_w.reshape(1, hd), k_norm_w.reshape(1, hd))


def _rope_tables(positions, hd):
    inv = 1.0 / (ROPE_THETA ** (jnp.arange(0, hd, 2, dtype=F32) / hd))
    ang = positions.astype(F32)[:, None] * inv[None, :]
    cos, sin = jnp.cos(ang), jnp.sin(ang)
    return jnp.concatenate([cos, cos], axis=-1), jnp.concatenate([-sin, sin], axis=-1)


def _diff_lambda_in_kernel(lq1_ref, lk1_ref, lq2_ref, lk2_ref, lam_init):
    e1 = jnp.exp(jnp.sum(lq1_ref[...] * lk1_ref[...], axis=-1, keepdims=True))
    e2 = jnp.exp(jnp.sum(lq2_ref[...] * lk2_ref[...], axis=-1, keepdims=True))
    return e1 - e2 + lam_init


def _attn_finalize(acc1, l1, acc2, l2, lam, subw, lam_init):
    o = acc1 / l1 - lam * (acc2 / l2)
    ms = jnp.mean(o * o, axis=-1, keepdims=True)
    return o * lax.rsqrt(ms + NORM_EPS) * subw * (1.0 - lam_init)


def _flash_kernel(qi_ref, ki_ref, q_ref, k_ref, v_ref, lq1_ref, lk1_ref, lq2_ref, lk2_ref, subw_ref, o_ref,
                  m_sc, l_sc, acc_sc, s_sc, *, G, hd, vd, tq, lam_init):
    qi = qi_ref[pl.program_id(1)]
    ki = ki_ref[pl.program_id(1)]

    @pl.when(ki == 0)
    def _():
        m_sc[...] = jnp.full(m_sc.shape, -jnp.inf, F32)
        l_sc[...] = jnp.zeros(l_sc.shape, F32)
        acc_sc[...] = jnp.zeros(acc_sc.shape, F32)

    def tile(diagonal):
        v = v_ref[...]
        for m in range(2):
            kb = k_ref[:, m * hd:(m + 1) * hd]
            for g in range(G):
                idx = g * 2 + m
                q = q_ref[:, idx * hd:(idx + 1) * hd]
                s = lax.dot_general(q, kb, _NT, preferred_element_type=F32)
                if diagonal:
                    rows = lax.broadcasted_iota(jnp.int32, (tq, tq), 0)
                    cols = lax.broadcasted_iota(jnp.int32, (tq, tq), 1)
                    s = jnp.where(cols <= rows, s, -jnp.inf)
                s_sc[...] = s
                m_prev = m_sc[idx]
                m_new = jnp.maximum(m_prev, jnp.max(s_sc[...], axis=-1, keepdims=True))
                m_sc[idx] = m_new
                alpha = jnp.exp2(m_prev - m_new)
                p = jnp.exp2(s_sc[...] - jnp.tile(m_new, (1, tq // LANES)))
                l_sc[idx] = alpha * l_sc[idx] + jnp.sum(p, axis=-1, keepdims=True)
                pv = jnp.dot(p.astype(BF16), v, preferred_element_type=F32)
                acc_sc[idx] = jnp.tile(alpha, (1, vd // LANES)) * acc_sc[idx] + pv

    @pl.when(ki < qi)
    def _():
        tile(False)

    @pl.when(ki == qi)
    def _():
        tile(True)
        lam = _diff_lambda_in_kernel(lq1_ref, lk1_ref, lq2_ref, lk2_ref, lam_init)
        subw = subw_ref[...]
        for g in range(G):
            o = _attn_finalize(acc_sc[2 * g], l_sc[2 * g][:, :1], acc_sc[2 * g + 1], l_sc[2 * g + 1][:, :1],
                               lam, subw, lam_init)
            o_ref[:, g * vd:(g + 1) * vd] = o.astype(o_ref.dtype)


def _flash_diff_attn(q, kb, vb, lams, subln_w, *, HK, G, hd, vd, lam_init, out_dtype, tq=512):
    T = q.shape[0]
    tq = _pick(T, tq)
    nq = T // tq
    vec = lambda n: pl.BlockSpec((1, n), lambda h, s, qt, kt: (0, 0))
    pairs = [(i, j) for i in range(nq) for j in range(i + 1)]
    qi_tab = jnp.asarray([p[0] for p in pairs], jnp.int32)
    ki_tab = jnp.asarray([p[1] for p in pairs], jnp.int32)
    grid_spec = pltpu.PrefetchScalarGridSpec(
        num_scalar_prefetch=2,
        grid=(HK, len(pairs)),
        in_specs=[pl.BlockSpec((tq, G * 2 * hd), lambda h, s, qt, kt: (qt[s], h)),
                  pl.BlockSpec((tq, 2 * hd), lambda h, s, qt, kt: (kt[s], h)),
                  pl.BlockSpec((tq, vd), lambda h, s, qt, kt: (kt[s], h)),
                  vec(hd), vec(hd), vec(hd), vec(hd), vec(vd)],
        out_specs=pl.BlockSpec((tq, G * vd), lambda h, s, qt, kt: (qt[s], h)),
        scratch_shapes=[pltpu.VMEM((2 * G, tq, LANES), F32), pltpu.VMEM((2 * G, tq, LANES), F32),
                        pltpu.VMEM((2 * G, tq, vd), F32), pltpu.VMEM((tq, tq), F32)],
    )
    return pl.pallas_call(
        functools.partial(_flash_kernel, G=G, hd=hd, vd=vd, tq=tq, lam_init=lam_init),
        grid_spec=grid_spec,
        out_shape=jax.ShapeDtypeStruct((T, HK * G * vd), out_dtype),
        compiler_params=_cparams("parallel", "arbitrary"),
        name="flash_diff_attn",
    )(qi_tab, ki_tab, q, kb, vb, *[x.reshape(1, -1) for x in lams], subln_w.reshape(1, vd))


def _decode_kernel(pt_ref, q_ref, kn_ref, vn_ref, *refs, HK, G, vd, pps, scale, lam_init):
    kc_refs, vc_refs = refs[:pps], refs[pps:2 * pps]
    (lq1_ref, lk1_ref, lq2_ref, lk2_ref, subw_ref, o_ref, s_sc, a_sc, anew_sc, acc_sc) = refs[2 * pps:]
    ph = pl.program_id(1)
    pg = pl.program_id(2)
    last = pl.num_programs(2) - 1
    NI = q_ref.shape[1]
    half = NI // 2
    q = q_ref[0]
    nk = q.shape[1] // LANES
    page = kc_refs[0].shape[0] // nk
    nv = vc_refs[0].shape[0] // page
    v_order = [et * HK + hk for hk in range(HK) for et in range(nv // HK)]

    def token_rows(ref, order):
        x = ref[...].reshape(page, len(order) * LANES)
        if list(order) != list(range(len(order))):
            x = jnp.concatenate([x[:, s * LANES:(s + 1) * LANES] for s in order], axis=1)
        return x.astype(BF16)

    row_hk = lax.broadcasted_iota(jnp.int32, (half, vd), 0) // G

    def own_head(x):
        out = x[:, :vd]
        for hk in range(1, HK):
            out = jnp.where(row_hk == hk, x[:, hk * vd:(hk + 1) * vd], out)
        return out

    @pl.when(ph == 0)
    def _():
        s_sc[pg] = scale * jnp.concatenate(
            [lax.dot_general(q, token_rows(kc, range(nk)), _NT, preferred_element_type=F32) for kc in kc_refs],
            axis=1)

    @pl.when(jnp.logical_and(ph == 0, pg == last))
    def _():
        lam = _diff_lambda_in_kernel(lq1_ref, lk1_ref, lq2_ref, lk2_ref, lam_init)
        s_new = scale * jnp.sum(q.astype(F32) * kn_ref[0].astype(BF16).astype(F32), axis=-1, keepdims=True)
        s = s_sc[...]
        m = jnp.maximum(jnp.max(jnp.max(s, axis=0), axis=-1, keepdims=True), s_new)
        e = jnp.exp(s - m[None])
        e_new = jnp.exp(s_new - m)
        denom = jnp.sum(jnp.sum(e, axis=0), axis=-1, keepdims=True) + e_new
        p = e / denom[None]
        p_new = e_new / denom
        a_sc[...] = (p[:, :half] - lam * p[:, half:]).astype(BF16)
        anew_sc[...] = jnp.broadcast_to(p_new[:half] - lam * p_new[half:], (half, LANES))
        acc_sc[...] = jnp.zeros(acc_sc.shape, F32)

    @pl.when(ph == 1)
    def _():
        a = a_sc[pg]
        pv = sum(jnp.dot(a[:, j * page:(j + 1) * page], token_rows(vc, v_order), preferred_element_type=F32)
                 for j, vc in enumerate(vc_refs))
        acc_sc[...] += own_head(pv)

    @pl.when(jnp.logical_and(ph == 1, pg == last))
    def _():
        a_new = anew_sc[...][:, :1]
        v_new = own_head(jnp.broadcast_to(vn_ref[0], (half, HK * vd)))
        o = acc_sc[...] + a_new * v_new
        ms = jnp.mean(o * o, axis=-1, keepdims=True)
        o_ref[0] = o * lax.rsqrt(ms + NORM_EPS) * subw_ref[...] * (1.0 - lam_init)


def _decode_diff_attn(q, k_new, v_new, cache_k, cache_v, page_table, lams, subln_w, *, HK, G, hd, vd, lam_init):
    DB, n_pages = page_table.shape
    n_phys, page = cache_k.shape[0], cache_k.shape[1]
    assert page % LANES == 0 and vd % LANES == 0 and hd == LANES
    kcw, vcw = HK * 2 * hd, HK * vd
    nk, nv = kcw // LANES, vcw // LANES
    kc = cache_k.reshape(n_phys * page * nk, LANES)
    vc = jnp.transpose(cache_v.reshape(n_phys, page, HK, vd // LANES, LANES),
                       (0, 1, 3, 2, 4)).reshape(n_phys * page * nv, LANES)
    NI = 2 * HK * G
    qt = jnp.transpose(q.reshape(DB, HK, G, 2, hd), (0, 3, 1, 2, 4))
    own = (jnp.eye(HK, dtype=q.dtype)[None, None, :, None, :, None, None]
           * jnp.eye(2, dtype=q.dtype)[None, :, None, None, None, :, None])
    qbd = (qt[:, :, :, :, None, None, :] * own).reshape(DB, NI, kcw)
    pps = math.gcd(n_pages, DECODE_PAGES_PER_STEP)
    steps = n_pages // pps
    per_b = lambda r, n: pl.BlockSpec((1, r, n), lambda b, ph, pg, pt: (b, 0, 0))
    vec = lambda n: pl.BlockSpec((1, n), lambda b, ph, pg, pt: (0, 0))

    def k_spec(j):
        return pl.BlockSpec((page * nk, LANES),
                            lambda b, ph, pg, pt: (pt[b, jnp.where(ph == 0, pg, steps - 1) * pps + j], 0))

    def v_spec(j):
        return pl.BlockSpec((page * nv, LANES),
                            lambda b, ph, pg, pt: (pt[b, jnp.where(ph == 1, pg, 0) * pps + j], 0))

    grid_spec = pltpu.PrefetchScalarGridSpec(
        num_scalar_prefetch=1,
        grid=(DB, 2, steps),
        in_specs=[per_b(NI, kcw), per_b(1, kcw), per_b(1, vcw)]
        + [k_spec(j) for j in range(pps)] + [v_spec(j) for j in range(pps)]
        + [vec(hd), vec(hd), vec(hd), vec(hd), vec(vd)],
        out_specs=pl.BlockSpec((1, NI // 2, vd), lambda b, ph, pg, pt: (b, 0, 0)),
        scratch_shapes=[pltpu.VMEM((steps, NI, pps * page), F32), pltpu.VMEM((steps, NI // 2, pps * page), BF16),
                        pltpu.VMEM((NI // 2, LANES), F32), pltpu.VMEM((NI // 2, vd), F32)],
    )
    out = pl.pallas_call(
        functools.partial(_decode_kernel, HK=HK, G=G, vd=vd, pps=pps, scale=hd ** -0.5, lam_init=lam_init),
        grid_spec=grid_spec,
        out_shape=jax.ShapeDtypeStruct((DB, NI // 2, vd), F32),
        compiler_params=_cparams("parallel", "arbitrary", "arbitrary"),
        name="decode_diff_attn",
    )(page_table, qbd, k_new.reshape(DB, 1, kcw), v_new.reshape(DB, 1, vcw), *([kc] * pps), *([vc] * pps),
      *[x.reshape(1, -1) for x in lams], subln_w.reshape(1, vd))
    return out.reshape(DB, HK * G * vd)


def _segsum_matrix(n, seg):
    r = lax.broadcasted_iota(jnp.int32, (n, n), 0) // seg
    c = lax.broadcasted_iota(jnp.int32, (n, n), 1) // seg
    return (r == c).astype(F32)


def _prep_kernel(*refs, halo, rh, exact, tc):
    if halo:
        (r_ref, k_ref, v_ref, l_ref, rp_ref, kp_ref, vp_ref, lp_ref, ri_ref, ki_ref, vi_ref, li_ref,
         mur_ref, muk_ref, muv_ref, mul_ref, w0_ref, w2_ref, a0_ref, a2_ref, g2_ref, kkw_ref, kaw_ref,
         ro_ref, lwo_ref, ko_ref, vo_ref, kko_ref, bo_ref, go_ref) = refs
    else:
        (r_ref, k_ref, v_ref, l_ref, rp_ref, kp_ref, vp_ref, lp_ref,
         mur_ref, muk_ref, muv_ref, mul_ref, w0_ref, w2_ref, a0_ref, a2_ref, g2_ref, kkw_ref, kaw_ref,
         ro_ref, lwo_ref, ko_ref, vo_ref, kko_ref, bo_ref, go_ref) = refs
    i = pl.program_id(0)

    def shifted(x_ref, prev_ref, init_ref):
        x = x_ref[...]
        if not halo:
            return x, prev_ref[...]
        last = prev_ref[SUBLANES - 1:SUBLANES, :]
        first = jnp.where(i == 0, init_ref[...], last)
        xs = pltpu.roll(x, 1, axis=0)
        rows = lax.broadcasted_iota(jnp.int32, x.shape, 0)
        return x, jnp.where(rows == 0, first, xs)

    def mix(x_ref, prev_ref, init_ref, mu_ref):
        x, xp = shifted(x_ref, prev_ref, init_ref)
        return x + (xp - x) * mu_ref[...]

    r = mix(r_ref, rp_ref, ri_ref if halo else None, mur_ref)
    k = mix(k_ref, kp_ref, ki_ref if halo else None, muk_ref)
    v = mix(v_ref, vp_ref, vi_ref if halo else None, muv_ref)
    lo = mix(l_ref, lp_ref, li_ref if halo else None, mul_ref)
    dl = w2_ref.shape[0]
    al = a2_ref.shape[0]
    xw = lo[:, :dl]
    xa = lo[:, dl:dl + al]
    xg = lo[:, dl + al:]
    wl = w0_ref[...] + _dg(jnp.tanh(xw), w2_ref[...], _NN, False)
    lw = -jnp.exp(-_softplus(-wl) - 0.5)
    a = _sigmoid(a0_ref[...] + _dg(xa, a2_ref[...], _NN, False))
    g = _dg(_sigmoid(xg), g2_ref[...], _NN, False)
    seg = _segsum_matrix(tc, rh)
    kk = k * kkw_ref[...]
    ss = _dg(kk * kk, seg, _NN, exact)
    kk = kk / jnp.maximum(jnp.sqrt(ss), 1e-12)
    kf = k * (1.0 + (a - 1.0) * kaw_ref[...])
    b = kk * a
    for pi in range(tc // LANES):
        sl = slice(pi * LANES, (pi + 1) * LANES)
        ro_ref[pi] = r[:, sl].astype(ro_ref.dtype)
        lwo_ref[pi] = lw[:, sl]
        ko_ref[pi] = kf[:, sl].astype(ko_ref.dtype)
        vo_ref[pi] = v[:, sl].astype(vo_ref.dtype)
        kko_ref[pi] = kk[:, sl].astype(kko_ref.dtype)
        bo_ref[pi] = b[:, sl].astype(bo_ref.dtype)
        go_ref[pi] = g[:, sl]


def _rwkv_prep(rkv, lora, prev, mu_rkv, mu_lora, w0, w2, a0, a2, g2p, k_k, k_a, *, rh, exact, tt=256, tc=512):
    T, W3 = rkv.shape
    W = W3 // 3
    LP = lora.shape[1]
    tc = _pick(W, tc)
    nj = W // tc
    halo = prev[0].shape[0] != T or T == 1
    tt = _pick(T, tt, SUBLANES)
    hb = tt // SUBLANES
    xblk = lambda off: pl.BlockSpec((tt, tc), lambda i, j: (i, off * nj + j))
    pblk = lambda off: pl.BlockSpec((SUBLANES, tc), lambda i, j: (jnp.maximum(i * hb - 1, 0), off * nj + j))
    cvec = lambda off: pl.BlockSpec((1, tc), lambda i, j: (0, off * nj + j))
    in_specs = [xblk(0), xblk(1), xblk(2), pl.BlockSpec((tt, LP), lambda i, j: (i, 0))]
    args = [rkv, rkv, rkv, lora]
    if halo:
        in_specs += [pblk(0), pblk(1), pblk(2),
                     pl.BlockSpec((SUBLANES, LP), lambda i, j: (jnp.maximum(i * hb - 1, 0), 0)),
                     cvec(0), cvec(1), cvec(2), pl.BlockSpec((1, LP), lambda i, j: (0, 0))]
        args += [rkv, rkv, rkv, lora, prev[0], prev[0], prev[0], prev[1]]
    else:
        in_specs += [xblk(0), xblk(1), xblk(2), pl.BlockSpec((tt, LP), lambda i, j: (i, 0))]
        args += [prev[0], prev[0], prev[0], prev[1]]
    in_specs += [cvec(0), cvec(1), cvec(2), pl.BlockSpec((1, LP), lambda i, j: (0, 0)),
                 cvec(0), pl.BlockSpec((w2.shape[0], tc), lambda i, j: (0, j)),
                 cvec(0), pl.BlockSpec((a2.shape[0], tc), lambda i, j: (0, j)),
                 pl.BlockSpec((g2p.shape[0], tc), lambda i, j: (0, j)),
                 cvec(0), cvec(0)]
    args += [mu_rkv, mu_rkv, mu_rkv, mu_lora, w0.reshape(1, W), w2, a0.reshape(1, W), a2, g2p,
             k_k.reshape(1, W), k_a.reshape(1, W)]
    npair = tc // LANES
    store = F32 if exact else BF16
    oshape = [jax.ShapeDtypeStruct((W // LANES, T, LANES), dt) for dt in (store, F32, store, store, store, store, F32)]
    ospec = pl.BlockSpec((npair, tt, LANES), lambda i, j: (j, i, 0))
    return pl.pallas_call(
        functools.partial(_prep_kernel, halo=halo, rh=rh, exact=exact, tc=tc),
        grid=(T // tt, nj),
        in_specs=in_specs,
        out_specs=[ospec] * 7,
        out_shape=oshape,
        compiler_params=_cparams("parallel", "arbitrary"),
        name="rwkv_prep",
    )(*args)


def _scan_kernel(r_ref, lw_ref, k_ref, v_ref, kk_ref, b_ref, g_ref, s0_ref, rk_ref, lnw_ref, lnb_ref,
                 y_ref, sout_ref, s_sc, *, C, rh, npair, precise, t_real):
    c = pl.program_id(1)
    R = 2 * C
    lane = lax.broadcasted_iota(jnp.int32, (C, LANES), 1)
    head0 = lane < rh
    ri = lax.broadcasted_iota(jnp.int32, (R, R), 0)
    ci = lax.broadcasted_iota(jnp.int32, (R, R), 1)
    same = (ri // C) == (ci // C)
    strict = jnp.logical_and(same, (ci % C) < (ri % C))
    incl = jnp.logical_and(same, (ci % C) <= (ri % C))
    tril_c = (lax.broadcasted_iota(jnp.int32, (C, C), 1)
              <= lax.broadcasted_iota(jnp.int32, (C, C), 0)).astype(F32)
    seg = _segsum_matrix(LANES, rh)
    same_head = seg > 0.5
    zero = jnp.zeros((rh, rh), F32)

    def stack_masked(x):
        return jnp.concatenate([jnp.where(head0, x, 0.0), jnp.where(head0, 0.0, x)], axis=0)

    def stack_dup(x):
        return jnp.concatenate([x, x], axis=0)

    @pl.when(c == 0)
    def _():
        def init(p, carry):
            top = jnp.concatenate([s0_ref[0, 2 * p], zero], axis=1)
            bot = jnp.concatenate([zero, s0_ref[0, 2 * p + 1]], axis=1)
            s_sc[p] = jnp.concatenate([top, bot], axis=0)
            return carry
        lax.fori_loop(0, npair, init, 0)

    def single_step(r, lw, k, v, kk, b, S):
        sa = yield -kk, S, _NT, False
        outer = yield jnp.concatenate([sa, v], axis=0), jnp.concatenate([b, k], axis=0), _TN, True
        s_new = S * jnp.exp(lw[0:1, :]) + jnp.where(same_head, outer, 0.0)
        y = yield r, s_new, _NT, False
        return y, s_new

    def chunk_steps(r, lw, k, v, kk, b, S):
        cum = yield tril_c, lw, _NN, True
        tot = cum[C - 1:C, :]
        e_incl = jnp.exp(cum)
        e_inv = jnp.exp(-cum)
        e_end = jnp.exp(tot - cum)
        a2 = stack_masked(-kk * jnp.exp(cum - lw))
        r2 = stack_masked(r * e_incl)
        v2 = stack_masked(v)
        bd2 = stack_masked(b * e_end)
        kd2 = stack_masked(k * e_end)
        b2 = stack_dup(b * e_inv)
        k2 = stack_dup(k * e_inv)
        ar = jnp.concatenate([a2, r2], axis=0)
        arb = yield ar, b2, _NT, precise
        ark = yield ar, k2, _NT, precise
        ars = yield ar, S, _NT, precise
        mrb = jnp.where(incl, arb[R:], 0.0)
        mrk = jnp.where(incl, ark[R:], 0.0)
        x = ars[:R]
        n_it = (t_real - 1).bit_length()
        if n_it:
            x = x + (yield jnp.where(strict, ark[:R], 0.0), v2, _NN, precise)
            pw = jnp.where(strict, arb[:R], 0.0)
        for it in range(n_it):
            x = x + (yield pw, x, _NN, precise)
            if it + 1 < n_it:
                pw = yield pw, pw, _NN, precise
        y2 = ars[R:] + (yield mrb, x, _NN, precise)
        y2 = y2 + (yield mrk, v2, _NN, precise)
        y = y2[:C] + y2[C:]
        s_new = S * jnp.exp(tot) + (yield x, bd2, _TN, precise)
        s_new = s_new + (yield v2, kd2, _TN, precise)
        return y, s_new

    def slab(r, lw, k, v, kk, b, g, S, rk, lnw, lnb):
        steps = single_step if t_real == 1 else chunk_steps
        y, s_new = yield from steps(r, lw, k, v, kk, b, S)
        mean = (yield y, seg, _NN, precise) * (1.0 / rh)
        d = y - mean
        var = (yield d * d, seg, _NN, precise) * (1.0 / rh)
        yn = d * lax.rsqrt(var + GN_EPS) * lnw + lnb
        bonus = (yield r * k * rk, seg, _NN, precise) * v
        return (yn + bonus) * g, s_new

    def run_lockstep(gens):
        reqs = [next(gen) for gen in gens]
        results = [None] * len(gens)
        live = list(range(len(gens)))
        while live:
            vals = [_dg(*reqs[i]) for i in live]
            nxt = []
            for i, val in zip(live, vals):
                try:
                    reqs[i] = gens[i].send(val)
                    nxt.append(i)
                except StopIteration as done:
                    results[i] = done.value
            live = nxt
        return results

    U = math.gcd(npair, SLABS_PER_GROUP)

    def body(i, carry):
        grp = pl.ds(pl.multiple_of(i * U, U), U)
        ins = [ref[grp].astype(F32) for ref in (r_ref, lw_ref, k_ref, v_ref, kk_ref, b_ref, g_ref, s_sc,
                                                rk_ref, lnw_ref, lnb_ref)]
        outs = run_lockstep([slab(*[a[u] for a in ins]) for u in range(U)])
        y_ref[grp] = jnp.stack([o[0] for o in outs]).astype(y_ref.dtype)
        s_sc[grp] = jnp.stack([o[1] for o in outs])
        return carry

    lax.fori_loop(0, npair // U, body, 0)

    @pl.when(c == pl.num_programs(1) - 1)
    def _():
        def fin(p, carry):
            S = s_sc[p]
            sout_ref[0, 2 * p] = S[:rh, :rh]
            sout_ref[0, 2 * p + 1] = S[rh:, rh:]
            return carry
        lax.fori_loop(0, npair, fin, 0)


def _rwkv_scan(prep, s0, r_k, ln_w, ln_b, *, B, C, rh, precise, out_dtype, t_real):
    npair, BT, _ = prep[0].shape
    Tb = BT // B
    nc = Tb // C
    H = s0.shape[1]
    xspec = pl.BlockSpec((npair, C, LANES), lambda b, c: (0, b * nc + c, 0))
    sspec = pl.BlockSpec((1, H, rh, rh), lambda b, c: (b, 0, 0, 0))
    pspec = pl.BlockSpec((npair, 1, LANES), lambda b, c: (0, 0, 0))
    y, s_out = pl.pallas_call(
        functools.partial(_scan_kernel, C=C, rh=rh, npair=npair, precise=precise, t_real=t_real),
        grid=(B, nc),
        in_specs=[xspec] * 7 + [sspec, pspec, pspec, pspec],
        out_specs=[xspec, sspec],
        out_shape=[jax.ShapeDtypeStruct((npair, BT, LANES), out_dtype),
                   jax.ShapeDtypeStruct((B, H, rh, rh), F32)],
        scratch_shapes=[pltpu.VMEM((npair, 2 * rh, 2 * rh), F32)],
        compiler_params=_cparams("parallel", "arbitrary"),
        name="rwkv_scan",
    )(*prep, s0, r_k.reshape(npair, 1, LANES), ln_w.reshape(npair, 1, LANES), ln_b.reshape(npair, 1, LANES))
    return y, s_out


def _router_kernel(x_ref, nw_ref, rw_ref, rb_ref, h_ref, lg_ref, *, precise):
    x = x_ref[...]
    ms = jnp.mean(x * x, axis=-1, keepdims=True)
    h = x * lax.rsqrt(ms + NORM_EPS) * nw_ref[...]
    h_ref[...] = h.astype(h_ref.dtype)
    lg_ref[...] = _dg(h, rw_ref[...], _NN, precise) + rb_ref[...]


def _router(x, norm_w, rw, rb, *, precise, h_dtype, tm=512):
    M, D = x.shape
    NR = rw.shape[1]
    tm = _pick(M, tm, SUBLANES)
    return pl.pallas_call(
        functools.partial(_router_kernel, precise=precise),
        grid=(M // tm,),
        in_specs=[pl.BlockSpec((tm, D), lambda i: (i, 0)), pl.BlockSpec((1, D), lambda i: (0, 0)),
                  pl.BlockSpec((D, NR), lambda i: (0, 0)), pl.BlockSpec((1, NR), lambda i: (0, 0))],
        out_specs=[pl.BlockSpec((tm, D), lambda i: (i, 0)), pl.BlockSpec((tm, NR), lambda i: (i, 0))],
        out_shape=[jax.ShapeDtypeStruct((M, D), h_dtype), jax.ShapeDtypeStruct((M, NR), F32)],
        compiler_params=_cparams("parallel"),
        name="moe_router",
    )(x, norm_w.reshape(1, D), rw, rb)


def _moe_kernel(be_ref, nu_ref, xr_ref, x_ref, wg_ref, wu_ref, wd_ref, sw_ref, o_ref):
    b = pl.program_id(0)
    j = pl.program_id(1)
    last = j == pl.num_programs(1) - 1
    used = b < nu_ref[0]

    @pl.when(jnp.logical_and(used, j == 0))
    def _():
        o_ref[...] = jnp.zeros(o_ref.shape, F32)

    @pl.when(used)
    def _():
        x = x_ref[...]
        tn = wg_ref.shape[2]
        wgu = jnp.concatenate([wg_ref[0].astype(BF16), wu_ref[0].astype(BF16)], axis=1)
        hgu = jnp.dot(x, wgu, preferred_element_type=F32)
        hg, hu = hgu[:, :tn], hgu[:, tn:]
        h = hg * _sigmoid(hg) * hu
        o_ref[...] += jnp.dot(h.astype(BF16), wd_ref[0].astype(BF16), preferred_element_type=F32)

    @pl.when(jnp.logical_and(used, last))
    def _():
        o_ref[...] = o_ref[...] * sw_ref[...]

    @pl.when(jnp.logical_and(jnp.logical_not(used), last))
    def _():
        o_ref[...] = jnp.zeros(o_ref.shape, F32)


def _moe_experts(xs, x_row0, blk_e, n_used, slot_w, wg, wu, wd, *, blk, tn):
    D = xs.shape[1]
    P = slot_w.shape[0]
    DE = wg.shape[2]
    tn = _pick(DE, tn)
    nj = DE // tn
    nb = P // blk

    def eff(b, j, nu):
        live = b < nu[0]
        return jnp.where(live, b, nu[0] - 1), jnp.where(live, j, nj - 1)

    def wg_map(b, j, be, nu, xr):
        bb, jj = eff(b, j, nu)
        return be[bb], 0, jj

    def wd_map(b, j, be, nu, xr):
        bb, jj = eff(b, j, nu)
        return be[bb], jj, 0

    grid_spec = pltpu.PrefetchScalarGridSpec(
        num_scalar_prefetch=3,
        grid=(nb, nj),
        in_specs=[pl.BlockSpec((pl.Element(blk), pl.Element(D)),
                               lambda b, j, be, nu, xr: (pl.multiple_of(xr[eff(b, j, nu)[0]], 2 * SUBLANES), 0)),
                  pl.BlockSpec((1, D, tn), wg_map), pl.BlockSpec((1, D, tn), wg_map),
                  pl.BlockSpec((1, tn, D), wd_map),
                  pl.BlockSpec((blk, 1), lambda b, j, be, nu, xr: (eff(b, j, nu)[0], 0))],
        out_specs=pl.BlockSpec((blk, D), lambda b, j, be, nu, xr: (b, 0)),
    )
    return pl.pallas_call(
        _moe_kernel,
        grid_spec=grid_spec,
        out_shape=jax.ShapeDtypeStruct((P, D), F32),
        compiler_params=_cparams("arbitrary", "arbitrary"),
        name="moe_experts",
    )(blk_e, n_used, x_row0, xs, wg, wu, wd, slot_w.reshape(P, 1))


def _hier_moe(x_groups, norm_w, rg_w, rg_b, re_w, re_b, wg, wu, wd, *, blk, tn):
    D = x_groups[0].shape[1]
    NG = rg_w.shape[1]
    E = re_w.shape[1]
    EPG = E // NG
    NR = -(-(NG + E) // LANES) * LANES
    rw = jnp.pad(jnp.concatenate([rg_w, re_w], axis=1), ((0, 0), (0, NR - NG - E))).astype(BF16)
    rb = jnp.pad(jnp.concatenate([rg_b, re_b]), (0, NR - NG - E)).reshape(1, NR)
    routed = [_router(x, norm_w, rw, rb, precise=False, h_dtype=BF16) for x in x_groups]
    logits = jnp.concatenate([r[1] for r in routed], axis=0)
    T = logits.shape[0]
    gprob = jax.nn.softmax(logits[:, :NG], axis=-1)
    gsel = jnp.argmax(gprob, axis=-1)
    gp = jnp.take_along_axis(gprob, gsel[:, None], axis=-1)[:, 0]
    elog = logits[:, NG:NG + E].reshape(T, NG, EPG)
    elog = jnp.take_along_axis(elog, gsel[:, None, None], axis=1)[:, 0]
    top_p, top_i = lax.top_k(jax.nn.softmax(elog, axis=-1), TOP_K)
    top_p = top_p / jnp.sum(top_p, axis=-1, keepdims=True)
    weights = gp[:, None] * top_p
    expert_idx = gsel[:, None].astype(jnp.int32) * EPG + top_i.astype(jnp.int32)

    TK = T * TOP_K
    nb = -(-TK // blk) + E
    P = nb * blk
    flat_e = expert_idx.reshape(-1)
    flat_w = weights.reshape(-1)
    flat_t = jnp.repeat(jnp.arange(T, dtype=jnp.int32), TOP_K)
    order = jnp.argsort(flat_e)
    se = flat_e[order]
    counts = jnp.bincount(flat_e, length=E)
    pcounts = (counts + blk - 1) // blk * blk
    pend = jnp.cumsum(pcounts)
    pstart = pend - pcounts
    start = jnp.cumsum(counts) - counts
    dest = (pstart[se] + jnp.arange(TK) - start[se]).astype(jnp.int32)
    slot_w = jnp.zeros((P,), F32).at[dest].set(flat_w[order])
    seg = 2 * SUBLANES
    ccounts = (counts + seg - 1) // seg * seg
    cstart = jnp.cumsum(ccounts) - ccounts
    n_rows = -(-(TK + seg * E) // seg) * seg + blk
    cdest = (cstart[se] + jnp.arange(TK) - start[se]).astype(jnp.int32)
    slot_t = jnp.full((n_rows,), T, jnp.int32).at[cdest].set(flat_t[order])
    blk_e = jnp.minimum(jnp.searchsorted(pend, jnp.arange(nb) * blk, side='right'), E - 1).astype(jnp.int32)
    n_used = (pend[-1] // blk).astype(jnp.int32).reshape(1)
    pos = jnp.zeros((TK,), jnp.int32).at[order].set(dest)

    x_row0 = (cstart[blk_e] + (jnp.arange(nb, dtype=jnp.int32) - pstart[blk_e] // blk) * blk).astype(jnp.int32)
    x_row0 = jnp.clip(x_row0, 0, n_rows - blk)
    xs = jnp.concatenate([r[0] for r in routed] + [jnp.zeros((1, D), BF16)], axis=0)[slot_t]
    ys = _moe_experts(xs, x_row0, blk_e, n_used, slot_w, wg, wu, wd, blk=blk, tn=tn)
    pos = pos.reshape(T, TOP_K)
    outs, t0 = [], 0
    for x in x_groups:
        pg = pos[t0:t0 + x.shape[0]]
        outs.append(x + ys[pg[:, 0]] + ys[pg[:, 1]])
        t0 += x.shape[0]
    return outs


def _pairs_to_rows(y):
    npair, T, _ = y.shape
    return jnp.transpose(y, (1, 0, 2)).reshape(T, npair * LANES)


def _mixer(x, positions, prev_rows, s0, B, cfg, w, *, exact, attn_fn, q_scale, emit_bf16):
    T, D = x.shape
    hd, HK, G, vd, rh = cfg['hd'], cfg['HK'], cfg['G'], cfg['vd'], cfg['rh']
    qcols, kcols, vcols = HK * G * 2 * hd, HK * 2 * hd, HK * vd
    W = cfg['W']
    mm = functools.partial(_matmul, precise=False)

    h = _rmsnorm(x, w['norm1_w'], BF16)
    qkv = mm(h, w['w_in'], qcols + kcols + vcols, 0)
    rkv = mm(h, w['w_in'], 3 * W, cfg['r_off'])
    lora = mm(h, w['w_in'], cfg['lp'], cfg['r_off'] + 3 * W, tn=cfg['lp'])
    gates = mm(h, w['w_gates'], 2 * D)

    cos_full, sin_signed = _rope_tables(positions, hd)
    outs = _qk_rope(qkv, cos_full, sin_signed, w['q_norm_w'], w['k_norm_w'], qcols, kcols, vcols,
                    q_scale=q_scale, q_dtype=BF16, emit_bf16=emit_bf16)
    q, k_new, v_new = outs[:3]
    o_attn = attn_fn(q, outs)

    prep = _rwkv_prep(rkv, lora, prev_rows, w['mu_rkv'], w['mu_lora'], w['rwkv_w0'], w['rwkv_w2'],
                      w['rwkv_a0'], w['rwkv_a2'], w['g2p'], w['rwkv_k_k'], w['rwkv_k_a'],
                      rh=rh, exact=exact)
    Tb = T // B
    if Tb % 64 == 0:
        C = 64
    else:
        C = SUBLANES
        pad = -Tb % C
        prep = [jnp.pad(a.reshape(-1, B, Tb, LANES), ((0, 0), (0, 0), (0, pad), (0, 0)))
                .reshape(a.shape[0], B * (Tb + pad), LANES) for a in prep]
    y_r, s_fin = _rwkv_scan(prep, s0, w['rwkv_r_k'], w['rwkv_ln_w'], w['rwkv_ln_b'],
                            B=B, C=C, rh=rh, precise=exact, out_dtype=BF16, t_real=min(Tb, C))
    if Tb % 64:
        y_r = y_r.reshape(y_r.shape[0], B, -1, LANES)[:, :, :Tb].reshape(y_r.shape[0], T, LANES)
    o_rwkv = _pairs_to_rows(y_r)

    mixed = _merge(o_attn, w['w_branch_attn'], o_rwkv, w['w_branch_rwkv'], gates, w['gate_b'],
                   precise=False, out_dtype=BF16)
    x1 = _matmul_residual(mixed, w['w_out'], x, precise=False)
    return x1, k_new, v_new, s_fin, rkv, lora


def kernel(x_prompt, x_sample, cache_k, cache_v, state_rwkv, state_shift, page_table, norm1_w, w_in, gate_b, q_norm_w, k_norm_w, lambda_q1, lambda_k1, lambda_q2, lambda_k2, subln_w, rwkv_mu, rwkv_w0, rwkv_w2, rwkv_a0, rwkv_a2, rwkv_g2, rwkv_k_k, rwkv_k_a, rwkv_r_k, rwkv_ln_w, rwkv_ln_b, w_branch_attn, w_branch_rwkv, w_out, norm2_w, router_group_w, router_group_b, router_expert_w, router_expert_b, expert_w_gate, expert_w_up, expert_w_down):
    B, T, D = x_prompt.shape
    DB, S, _ = x_sample.shape
    L = norm1_w.shape[0]
    assert L == 1 and B == 1 and S == 1, "one trunk layer, one prompt sequence, one new token per sample"
    hd = q_norm_w.shape[1]
    HK = cache_k.shape[3]
    vd = cache_v.shape[4]
    G = w_branch_attn.shape[1] // vd // HK
    H, rh = rwkv_r_k.shape[1], rwkv_r_k.shape[2]
    W = H * rh
    dl, al, gl = rwkv_w2.shape[1], rwkv_a2.shape[1], rwkv_g2.shape[1]
    lcols = dl + al + gl
    lp = -(-lcols // LANES) * LANES
    qkv_cols = HK * G * 2 * hd + HK * 2 * hd + HK * vd
    r_off = qkv_cols
    g_off = r_off + 3 * W + lcols
    n_pages, page = page_table.shape[1], cache_k.shape[2]
    past = n_pages * page
    cfg = dict(hd=hd, HK=HK, G=G, vd=vd, rh=rh, W=W, lp=lp, r_off=r_off)
    l = 0
    (cache_k, cache_v, state_rwkv, state_shift, norm1_w, w_in, gate_b, q_norm_w, k_norm_w, lambda_q1,
     lambda_k1, lambda_q2, lambda_k2, subln_w, rwkv_mu, rwkv_w0, rwkv_w2, rwkv_a0, rwkv_a2, rwkv_g2,
     rwkv_k_k, rwkv_k_a, rwkv_r_k, rwkv_ln_w, rwkv_ln_b, w_branch_attn, w_branch_rwkv, w_out, norm2_w,
     router_group_w, router_group_b, router_expert_w, router_expert_b, expert_w_gate, expert_w_up,
     expert_w_down) = [a.reshape(a.shape[1:]) for a in (
         cache_k, cache_v, state_rwkv, state_shift, norm1_w, w_in, gate_b, q_norm_w, k_norm_w, lambda_q1,
         lambda_k1, lambda_q2, lambda_k2, subln_w, rwkv_mu, rwkv_w0, rwkv_w2, rwkv_a0, rwkv_a2, rwkv_g2,
         rwkv_k_k, rwkv_k_a, rwkv_r_k, rwkv_ln_w, rwkv_ln_b, w_branch_attn, w_branch_rwkv, w_out, norm2_w,
         router_group_w, router_group_b, router_expert_w, router_expert_b, expert_w_gate, expert_w_up,
         expert_w_down)]
    lam_init = 0.8 - 0.6 * math.exp(-0.3 * l)
    lams = (lambda_q1, lambda_k1, lambda_q2, lambda_k2)

    mu = rwkv_mu
    g2p = jnp.pad(rwkv_g2, ((0, lp - lcols), (0, 0)))
    w_in_bf = w_in.astype(BF16)
    w = dict(
        norm1_w=norm1_w, q_norm_w=q_norm_w, k_norm_w=k_norm_w, gate_b=gate_b,
        mu_rkv=mu[:3 * W].reshape(1, 3 * W), mu_lora=jnp.pad(mu[3 * W:], (0, lp - lcols)).reshape(1, lp),
        rwkv_w0=rwkv_w0, rwkv_a0=rwkv_a0, rwkv_k_k=rwkv_k_k, rwkv_k_a=rwkv_k_a,
        rwkv_r_k=rwkv_r_k, rwkv_ln_w=rwkv_ln_w, rwkv_ln_b=rwkv_ln_b,
        norm2_w=norm2_w, router_group_w=router_group_w, router_group_b=router_group_b,
        router_expert_w=router_expert_w, router_expert_b=router_expert_b,
        expert_w_gate=expert_w_gate, expert_w_up=expert_w_up, expert_w_down=expert_w_down,
        w_in=w_in_bf, w_gates=w_in_bf[:, g_off:g_off + 2 * D],
        rwkv_w2=rwkv_w2.astype(BF16), rwkv_a2=rwkv_a2.astype(BF16), g2p=g2p.astype(BF16),
        w_branch_attn=w_branch_attn.astype(BF16), w_branch_rwkv=w_branch_rwkv.astype(BF16),
        w_out=w_out.astype(BF16))

    def prompt_attn(q, outs):
        return _flash_diff_attn(q, outs[3], outs[4], lams, subln_w, HK=HK, G=G, hd=hd, vd=vd,
                                lam_init=lam_init, out_dtype=BF16)

    zero_prev = (jnp.zeros((1, 3 * W), F32), jnp.zeros((1, lp), F32))
    xp1, kp, vp, sp, rkv_p, lora_p = _mixer(
        x_prompt.reshape(T, D), jnp.arange(T), zero_prev, jnp.zeros((1, H, rh, rh), F32), 1, cfg, w,
        exact=False, attn_fn=prompt_attn, q_scale=hd ** -0.5 * math.log2(math.e), emit_bf16=True)
    shift_p = jnp.concatenate([rkv_p[T - 1], lora_p[T - 1, :lcols]])

    def sample_attn(q, outs):
        return _decode_diff_attn(q, outs[1], outs[2], cache_k, cache_v, page_table, lams, subln_w,
                                 HK=HK, G=G, hd=hd, vd=vd, lam_init=lam_init)

    sh = state_shift
    prev_s = (sh[:, :3 * W], jnp.pad(sh[:, 3 * W:], ((0, 0), (0, lp - lcols))))
    xs1, ks, vs, ss, rkv_s, lora_s = _mixer(
        x_sample.reshape(DB, D), jnp.full((DB,), past, jnp.int32), prev_s, state_rwkv, DB, cfg, w,
        exact=True, attn_fn=sample_attn, q_scale=1.0, emit_bf16=False)
    shift_s = jnp.concatenate([rkv_s, lora_s[:, :lcols]], axis=1)


    yp, ys = _hier_moe([xp1, xs1], norm2_w, router_group_w, router_group_b, router_expert_w,
                       router_expert_b, expert_w_gate, expert_w_up, expert_w_down,
                       blk=min(MOE_ROW_BLOCK, max(2 * SUBLANES, T // 16)), tn=MOE_HIDDEN_TILE)

    return (yp.reshape(1, T, D), ys.reshape(DB, 1, D),
            kp.reshape(1, 1, T, HK, 2, hd), vp.reshape(1, 1, T, HK, vd),
            sp.reshape(1, 1, H, rh, rh), shift_p.reshape(1, 1, -1),
            ks.reshape(1, DB, 1, HK, 2, hd), vs.reshape(1, DB, 1, HK, vd),
            ss.reshape(1, DB, H, rh, rh), shift_s.reshape(1, DB, -1))
```

```python
import functools
import math

import jax
import jax.numpy as jnp
from jax import lax
from jax.experimental import pallas as pl
from jax.experimental.pallas import tpu as pltpu

F32 = jnp.float32
BF16 = jnp.bfloat16
HIGHEST = lax.Precision.HIGHEST

LANES = 128
SUBLANES = 8
VMEM_LIMIT = 56 * 1024 * 1024

ROPE_THETA = 10000.0
NORM_EPS = 1e-6
GN_EPS = 64e-5
TOP_K = 2
SLABS_PER_GROUP = 16
DECODE_PAGES_PER_STEP = 8
MOE_ROW_BLOCK = 512
MOE_HIDDEN_TILE = 256

_NN = (((1,), (0,)), ((), ()))
_NT = (((1,), (1,)), ((), ()))
_TN = (((0,), (0,)), ((), ()))


def _cparams(*sem):
    return pltpu.CompilerParams(dimension_semantics=sem, vmem_limit_bytes=VMEM_LIMIT)


def _pick(n, pref, unit=LANES):
    if n <= pref:
        return n
    t = pref // unit * unit
    while t > unit and n % t:
        t -= unit
    assert n % t == 0, (n, pref)
    return t


def _dg(a, b, dn, precise):
    if precise:
        return lax.dot_general(a.astype(F32), b.astype(F32), dn, precision=HIGHEST,
                               preferred_element_type=F32)
    return lax.dot_general(a.astype(BF16), b.astype(BF16), dn, preferred_element_type=F32)


def _sigmoid(x):
    return 1.0 / (1.0 + jnp.exp(-x))


def _softplus(x):
    return jnp.maximum(x, 0.0) + jnp.log(1.0 + jnp.exp(-jnp.abs(x)))


def _rms_kernel(x_ref, w_ref, o_ref, *, eps):
    x = x_ref[...]
    ms = jnp.mean(x * x, axis=-1, keepdims=True)
    o_ref[...] = (x * lax.rsqrt(ms + eps) * w_ref[...]).astype(o_ref.dtype)


def _rmsnorm(x, w, out_dtype, tm=512):
    M, D = x.shape
    tm = _pick(M, tm, SUBLANES)
    return pl.pallas_call(
        functools.partial(_rms_kernel, eps=NORM_EPS),
        grid=(M // tm,),
        in_specs=[pl.BlockSpec((tm, D), lambda i: (i, 0)),
                  pl.BlockSpec((1, D), lambda i: (0, 0))],
        out_specs=pl.BlockSpec((tm, D), lambda i: (i, 0)),
        out_shape=jax.ShapeDtypeStruct((M, D), out_dtype),
        compiler_params=_cparams("parallel"),
        name="rmsnorm",
    )(x, w.reshape(1, D))


def _mm_kernel(a_ref, b_ref, o_ref, *, precise):
    o_ref[...] = _dg(a_ref[...], b_ref[...], _NN, precise).astype(o_ref.dtype)


def _matmul(a, b, n, col0=0, *, precise, out_dtype=F32, tm=1024, tn=512):
    M, K = a.shape
    tm = _pick(M, tm, SUBLANES)
    tn = _pick(n, tn)
    if col0 % tn:
        b, col0 = b[:, col0:col0 + n], 0
    c0 = col0 // tn
    return pl.pallas_call(
        functools.partial(_mm_kernel, precise=precise),
        grid=(M // tm, n // tn),
        in_specs=[pl.BlockSpec((tm, K), lambda i, j: (i, 0)),
                  pl.BlockSpec((K, tn), lambda i, j: (0, c0 + j))],
        out_specs=pl.BlockSpec((tm, tn), lambda i, j: (i, j)),
        out_shape=jax.ShapeDtypeStruct((M, n), out_dtype),
        compiler_params=_cparams("parallel", "arbitrary"),
        name="matmul",
    )(a, b)


def _merge_kernel(a1_ref, b1_ref, a2_ref, b2_ref, g1_ref, g2_ref, gb1_ref, gb2_ref, o_ref, *, precise):
    o1 = _dg(a1_ref[...], b1_ref[...], _NN, precise)
    o2 = _dg(a2_ref[...], b2_ref[...], _NN, precise)
    s1 = _sigmoid(g1_ref[...] + gb1_ref[...])
    s2 = _sigmoid(g2_ref[...] + gb2_ref[...])
    o_ref[...] = (s1 * o1 + s2 * o2).astype(o_ref.dtype)


def _merge(a1, b1, a2, b2, gates, gate_b, *, precise, out_dtype, tm=512, tn=512):
    M, K = a1.shape
    K2 = a2.shape[1]
    D = b1.shape[1]
    tm = _pick(M, tm, SUBLANES)
    tn = _pick(D, tn)
    nj = D // tn
    gb = gate_b.reshape(1, 2 * D)
    return pl.pallas_call(
        functools.partial(_merge_kernel, precise=precise),
        grid=(M // tm, nj),
        in_specs=[pl.BlockSpec((tm, K), lambda i, j: (i, 0)),
                  pl.BlockSpec((K, tn), lambda i, j: (0, j)),
                  pl.BlockSpec((tm, K2), lambda i, j: (i, 0)),
                  pl.BlockSpec((K2, tn), lambda i, j: (0, j)),
                  pl.BlockSpec((tm, tn), lambda i, j: (i, j)),
                  pl.BlockSpec((tm, tn), lambda i, j: (i, nj + j)),
                  pl.BlockSpec((1, tn), lambda i, j: (0, j)),
                  pl.BlockSpec((1, tn), lambda i, j: (0, nj + j))],
        out_specs=pl.BlockSpec((tm, tn), lambda i, j: (i, j)),
        out_shape=jax.ShapeDtypeStruct((M, D), out_dtype),
        compiler_params=_cparams("parallel", "arbitrary"),
        name="merge",
    )(a1, b1, a2, b2, gates, gates, gb, gb)


def _mm_res_kernel(a_ref, b_ref, r_ref, o_ref, *, precise):
    o_ref[...] = r_ref[...] + _dg(a_ref[...], b_ref[...], _NN, precise)


def _matmul_residual(a, b, res, *, precise, tm=1024, tn=512):
    M, K = a.shape
    N = b.shape[1]
    tm = _pick(M, tm, SUBLANES)
    tn = _pick(N, tn)
    return pl.pallas_call(
        functools.partial(_mm_res_kernel, precise=precise),
        grid=(M // tm, N // tn),
        in_specs=[pl.BlockSpec((tm, K), lambda i, j: (i, 0)),
                  pl.BlockSpec((K, tn), lambda i, j: (0, j)),
                  pl.BlockSpec((tm, tn), lambda i, j: (i, j))],
        out_specs=pl.BlockSpec((tm, tn), lambda i, j: (i, j)),
        out_shape=jax.ShapeDtypeStruct((M, N), F32),
        compiler_params=_cparams("parallel", "arbitrary"),
        name="matmul_residual",
    )(a, b, res)


def _qkrope_kernel(qkv_ref, cos_ref, sin_ref, qw_ref, kw_ref, *out_refs, nq, nk, hd, vcols, scale, emit_bf16):
    if emit_bf16:
        q_ref, k_ref, v_ref, kb_ref, vb_ref = out_refs
    else:
        q_ref, k_ref, v_ref = out_refs
    cos = cos_ref[...]
    sin = sin_ref[...]

    def norm_rope(x, w):
        ms = jnp.mean(x * x, axis=-1, keepdims=True)
        xn = x * lax.rsqrt(ms + NORM_EPS) * w
        return xn * cos + pltpu.roll(xn, hd // 2, axis=1) * sin

    qw = qw_ref[...]
    kw = kw_ref[...]
    for c in range(nq):
        o = norm_rope(qkv_ref[:, c * hd:(c + 1) * hd], qw) * scale
        q_ref[:, c * hd:(c + 1) * hd] = o.astype(q_ref.dtype)
    for c in range(nk):
        o = norm_rope(qkv_ref[:, (nq + c) * hd:(nq + c + 1) * hd], kw)
        k_ref[:, c * hd:(c + 1) * hd] = o
        if emit_bf16:
            kb_ref[:, c * hd:(c + 1) * hd] = o.astype(BF16)
    v = qkv_ref[:, (nq + nk) * hd:(nq + nk) * hd + vcols]
    v_ref[...] = v
    if emit_bf16:
        vb_ref[...] = v.astype(BF16)


def _qk_rope(qkv, cos_full, sin_signed, q_norm_w, k_norm_w, qcols, kcols, vcols, *, q_scale, q_dtype,
             emit_bf16, tq=256):
    T = qkv.shape[0]
    hd = q_norm_w.shape[0]
    tq = _pick(T, tq, SUBLANES)
    nq, nk = qcols // hd, kcols // hd
    row = lambda i: (i, 0)
    out_shape = [jax.ShapeDtypeStruct((T, qcols), q_dtype),
                 jax.ShapeDtypeStruct((T, kcols), F32),
                 jax.ShapeDtypeStruct((T, vcols), F32)]
    out_specs = [pl.BlockSpec((tq, qcols), row), pl.BlockSpec((tq, kcols), row), pl.BlockSpec((tq, vcols), row)]
    if emit_bf16:
        out_shape += [jax.ShapeDtypeStruct((T, kcols), BF16), jax.ShapeDtypeStruct((T, vcols), BF16)]
        out_specs += [pl.BlockSpec((tq, kcols), row), pl.BlockSpec((tq, vcols), row)]
    return pl.pallas_call(
        functools.partial(_qkrope_kernel, nq=nq, nk=nk, hd=hd, vcols=vcols, scale=q_scale,
                          emit_bf16=emit_bf16),
        grid=(T // tq,),
        in_specs=[pl.BlockSpec((tq, qkv.shape[1]), row),
                  pl.BlockSpec((tq, hd), row), pl.BlockSpec((tq, hd), row),
                  pl.BlockSpec((1, hd), lambda i: (0, 0)), pl.BlockSpec((1, hd), lambda i: (0, 0))],
        out_specs=out_specs,
        out_shape=out_shape,
        compiler_params=_cparams("parallel"),
        name="qk_norm_rope",
    )(qkv, cos_full, sin_signed, q_norm_w.reshape(1, hd), k_norm_w.reshape(1, hd))


def _rope_tables(positions, hd):
    inv = 1.0 / (ROPE_THETA ** (jnp.arange(0, hd, 2, dtype=F32) / hd))
    ang = positions.astype(F32)[:, None] * inv[None, :]
    cos, sin = jnp.cos(ang), jnp.sin(ang)
    return jnp.concatenate([cos, cos], axis=-1), jnp.concatenate([-sin, sin], axis=-1)


def _diff_lambda_in_kernel(lq1_ref, lk1_ref, lq2_ref, lk2_ref, lam_init):
    e1 = jnp.exp(jnp.sum(lq1_ref[...] * lk1_ref[...], axis=-1, keepdims=True))
    e2 = jnp.exp(jnp.sum(lq2_ref[...] * lk2_ref[...], axis=-1, keepdims=True))
    return e1 - e2 + lam_init


def _attn_finalize(acc1, l1, acc2, l2, lam, subw, lam_init):
    o = acc1 / l1 - lam * (acc2 / l2)
    ms = jnp.mean(o * o, axis=-1, keepdims=True)
    return o * lax.rsqrt(ms + NORM_EPS) * subw * (1.0 - lam_init)


def _flash_kernel(qi_ref, ki_ref, q_ref, k_ref, v_ref, lq1_ref, lk1_ref, lq2_ref, lk2_ref, subw_ref, o_ref,
                  m_sc, l_sc, acc_sc, s_sc, *, G, hd, vd, tq, lam_init):
    qi = qi_ref[pl.program_id(1)]
    ki = ki_ref[pl.program_id(1)]

    @pl.when(ki == 0)
    def _():
        m_sc[...] = jnp.full(m_sc.shape, -jnp.inf, F32)
        l_sc[...] = jnp.zeros(l_sc.shape, F32)
        acc_sc[...] = jnp.zeros(acc_sc.shape, F32)

    def tile(diagonal):
        v = v_ref[...]
        for m in range(2):
            kb = k_ref[:, m * hd:(m + 1) * hd]
            for g in range(G):
                idx = g * 2 + m
                q = q_ref[:, idx * hd:(idx + 1) * hd]
                s = lax.dot_general(q, kb, _NT, preferred_element_type=F32)
                if diagonal:
                    rows = lax.broadcasted_iota(jnp.int32, (tq, tq), 0)
                    cols = lax.broadcasted_iota(jnp.int32, (tq, tq), 1)
                    s = jnp.where(cols <= rows, s, -jnp.inf)
                s_sc[...] = s
                m_prev = m_sc[idx]
                m_new = jnp.maximum(m_prev, jnp.max(s_sc[...], axis=-1, keepdims=True))
                m_sc[idx] = m_new
                alpha = jnp.exp2(m_prev - m_new)
                p = jnp.exp2(s_sc[...] - jnp.tile(m_new, (1, tq // LANES)))
                l_sc[idx] = alpha * l_sc[idx] + jnp.sum(p, axis=-1, keepdims=True)
                pv = jnp.dot(p.astype(BF16), v, preferred_element_type=F32)
                acc_sc[idx] = jnp.tile(alpha, (1, vd // LANES)) * acc_sc[idx] + pv

    @pl.when(ki < qi)
    def _():
        tile(False)

    @pl.when(ki == qi)
    def _():
        tile(True)
        lam = _diff_lambda_in_kernel(lq1_ref, lk1_ref, lq2_ref, lk2_ref, lam_init)
        subw = subw_ref[...]
        for g in range(G):
            o = _attn_finalize(acc_sc[2 * g], l_sc[2 * g][:, :1], acc_sc[2 * g + 1], l_sc[2 * g + 1][:, :1],
                               lam, subw, lam_init)
            o_ref[:, g * vd:(g + 1) * vd] = o.astype(o_ref.dtype)


def _flash_diff_attn(q, kb, vb, lams, subln_w, *, HK, G, hd, vd, lam_init, out_dtype, tq=512):
    T = q.shape[0]
    tq = _pick(T, tq)
    nq = T // tq
    vec = lambda n: pl.BlockSpec((1, n), lambda h, s, qt, kt: (0, 0))
    pairs = [(i, j) for i in range(nq) for j in range(i + 1)]
    qi_tab = jnp.asarray([p[0] for p in pairs], jnp.int32)
    ki_tab = jnp.asarray([p[1] for p in pairs], jnp.int32)
    grid_spec = pltpu.PrefetchScalarGridSpec(
        num_scalar_prefetch=2,
        grid=(HK, len(pairs)),
        in_specs=[pl.BlockSpec((tq, G * 2 * hd), lambda h, s, qt, kt: (qt[s], h)),
                  pl.BlockSpec((tq, 2 * hd), lambda h, s, qt, kt: (kt[s], h)),
                  pl.BlockSpec((tq, vd), lambda h, s, qt, kt: (kt[s], h)),
                  vec(hd), vec(hd), vec(hd), vec(hd), vec(vd)],
        out_specs=pl.BlockSpec((tq, G * vd), lambda h, s, qt, kt: (qt[s], h)),
        scratch_shapes=[pltpu.VMEM((2 * G, tq, LANES), F32), pltpu.VMEM((2 * G, tq, LANES), F32),
                        pltpu.VMEM((2 * G, tq, vd), F32), pltpu.VMEM((tq, tq), F32)],
    )
    return pl.pallas_call(
        functools.partial(_flash_kernel, G=G, hd=hd, vd=vd, tq=tq, lam_init=lam_init),
        grid_spec=grid_spec,
        out_shape=jax.ShapeDtypeStruct((T, HK * G * vd), out_dtype),
        compiler_params=_cparams("parallel", "arbitrary"),
        name="flash_diff_attn",
    )(qi_tab, ki_tab, q, kb, vb, *[x.reshape(1, -1) for x in lams], subln_w.reshape(1, vd))


def _decode_kernel(pt_ref, q_ref, kn_ref, vn_ref, *refs, HK, G, vd, pps, scale, lam_init):
    kc_refs, vc_refs = refs[:pps], refs[pps:2 * pps]
    (lq1_ref, lk1_ref, lq2_ref, lk2_ref, subw_ref, o_ref, s_sc, a_sc, anew_sc, acc_sc) = refs[2 * pps:]
    ph = pl.program_id(1)
    pg = pl.program_id(2)
    last = pl.num_programs(2) - 1
    NI = q_ref.shape[1]
    half = NI // 2
    q = q_ref[0]
    nk = q.shape[1] // LANES
    page = kc_refs[0].shape[0] // nk
    nv = vc_refs[0].shape[0] // page
    v_order = [et * HK + hk for hk in range(HK) for et in range(nv // HK)]

    def token_rows(ref, order):
        x = ref[...].reshape(page, len(order) * LANES)
        if list(order) != list(range(len(order))):
            x = jnp.concatenate([x[:, s * LANES:(s + 1) * LANES] for s in order], axis=1)
        return x.astype(BF16)

    row_hk = lax.broadcasted_iota(jnp.int32, (half, vd), 0) // G

    def own_head(x):
        out = x[:, :vd]
        for hk in range(1, HK):
            out = jnp.where(row_hk == hk, x[:, hk * vd:(hk + 1) * vd], out)
        return out

    @pl.when(ph == 0)
    def _():
        s_sc[pg] = scale * jnp.concatenate(
            [lax.dot_general(q, token_rows(kc, range(nk)), _NT, preferred_element_type=F32) for kc in kc_refs],
            axis=1)

    @pl.when(jnp.logical_and(ph == 0, pg == last))
    def _():
        lam = _diff_lambda_in_kernel(lq1_ref, lk1_ref, lq2_ref, lk2_ref, lam_init)
        s_new = scale * jnp.sum(q.astype(F32) * kn_ref[0].astype(BF16).astype(F32), axis=-1, keepdims=True)
        s = s_sc[...]
        m = jnp.maximum(jnp.max(jnp.max(s, axis=0), axis=-1, keepdims=True), s_new)
        e = jnp.exp(s - m[None])
        e_new = jnp.exp(s_new - m)
        denom = jnp.sum(jnp.sum(e, axis=0), axis=-1, keepdims=True) + e_new
        p = e / denom[None]
        p_new = e_new / denom
        a_sc[...] = (p[:, :half] - lam * p[:, half:]).astype(BF16)
        anew_sc[...] = jnp.broadcast_to(p_new[:half] - lam * p_new[half:], (half, LANES))
        acc_sc[...] = jnp.zeros(acc_sc.shape, F32)

    @pl.when(ph == 1)
    def _():
        a = a_sc[pg]
        pv = sum(jnp.dot(a[:, j * page:(j + 1) * page], token_rows(vc, v_order), preferred_element_type=F32)
                 for j, vc in enumerate(vc_refs))
        acc_sc[...] += own_head(pv)

    @pl.when(jnp.logical_and(ph == 1, pg == last))
    def _():
        a_new = anew_sc[...][:, :1]
        v_new = own_head(jnp.broadcast_to(vn_ref[0], (half, HK * vd)))
        o = acc_sc[...] + a_new * v_new
        ms = jnp.mean(o * o, axis=-1, keepdims=True)
        o_ref[0] = o * lax.rsqrt(ms + NORM_EPS) * subw_ref[...] * (1.0 - lam_init)


def _decode_diff_attn(q, k_new, v_new, cache_k, cache_v, page_table, lams, subln_w, *, HK, G, hd, vd, lam_init):
    DB, n_pages = page_table.shape
    n_phys, page = cache_k.shape[0], cache_k.shape[1]
    assert page % LANES == 0 and vd % LANES == 0 and hd == LANES
    kcw, vcw = HK * 2 * hd, HK * vd
    nk, nv = kcw // LANES, vcw // LANES
    kc = cache_k.reshape(n_phys * page * nk, LANES)
    vc = jnp.transpose(cache_v.reshape(n_phys, page, HK, vd // LANES, LANES),
                       (0, 1, 3, 2, 4)).reshape(n_phys * page * nv, LANES)
    NI = 2 * HK * G
    qt = jnp.transpose(q.reshape(DB, HK, G, 2, hd), (0, 3, 1, 2, 4))
    own = (jnp.eye(HK, dtype=q.dtype)[None, None, :, None, :, None, None]
           * jnp.eye(2, dtype=q.dtype)[None, :, None, None, None, :, None])
    qbd = (qt[:, :, :, :, None, None, :] * own).reshape(DB, NI, kcw)
    pps = math.gcd(n_pages, DECODE_PAGES_PER_STEP)
    steps = n_pages // pps
    per_b = lambda r, n: pl.BlockSpec((1, r, n), lambda b, ph, pg, pt: (b, 0, 0))
    vec = lambda n: pl.BlockSpec((1, n), lambda b, ph, pg, pt: (0, 0))

    def k_spec(j):
        return pl.BlockSpec((page * nk, LANES),
                            lambda b, ph, pg, pt: (pt[b, jnp.where(ph == 0, pg, steps - 1) * pps + j], 0))

    def v_spec(j):
        return pl.BlockSpec((page * nv, LANES),
                            lambda b, ph, pg, pt: (pt[b, jnp.where(ph == 1, pg, 0) * pps + j], 0))

    grid_spec = pltpu.PrefetchScalarGridSpec(
        num_scalar_prefetch=1,
        grid=(DB, 2, steps),
        in_specs=[per_b(NI, kcw), per_b(1, kcw), per_b(1, vcw)]
        + [k_spec(j) for j in range(pps)] + [v_spec(j) for j in range(pps)]
        + [vec(hd), vec(hd), vec(hd), vec(hd), vec(vd)],
        out_specs=pl.BlockSpec((1, NI // 2, vd), lambda b, ph, pg, pt: (b, 0, 0)),
        scratch_shapes=[pltpu.VMEM((steps, NI, pps * page), F32), pltpu.VMEM((steps, NI // 2, pps * page), BF16),
                        pltpu.VMEM((NI // 2, LANES), F32), pltpu.VMEM((NI // 2, vd), F32)],
    )
    out = pl.pallas_call(
        functools.partial(_decode_kernel, HK=HK, G=G, vd=vd, pps=pps, scale=hd ** -0.5, lam_init=lam_init),
        grid_spec=grid_spec,
        out_shape=jax.ShapeDtypeStruct((DB, NI // 2, vd), F32),
        compiler_params=_cparams("parallel", "arbitrary", "arbitrary"),
        name="decode_diff_attn",
    )(page_table, qbd, k_new.reshape(DB, 1, kcw), v_new.reshape(DB, 1, vcw), *([kc] * pps), *([vc] * pps),
      *[x.reshape(1, -1) for x in lams], subln_w.reshape(1, vd))
    return out.reshape(DB, HK * G * vd)


def _segsum_matrix(n, seg):
    r = lax.broadcasted_iota(jnp.int32, (n, n), 0) // seg
    c = lax.broadcasted_iota(jnp.int32, (n, n), 1) // seg
    return (r == c).astype(F32)


def _prep_kernel(*refs, halo, rh, exact, tc):
    if halo:
        (r_ref, k_ref, v_ref, l_ref, rp_ref, kp_ref, vp_ref, lp_ref, ri_ref, ki_ref, vi_ref, li_ref,
         mur_ref, muk_ref, muv_ref, mul_ref, w0_ref, w2_ref, a0_ref, a2_ref, g2_ref, kkw_ref, kaw_ref,
         ro_ref, lwo_ref, ko_ref, vo_ref, kko_ref, bo_ref, go_ref) = refs
    else:
        (r_ref, k_ref, v_ref, l_ref, rp_ref, kp_ref, vp_ref, lp_ref,
         mur_ref, muk_ref, muv_ref, mul_ref, w0_ref, w2_ref, a0_ref, a2_ref, g2_ref, kkw_ref, kaw_ref,
         ro_ref, lwo_ref, ko_ref, vo_ref, kko_ref, bo_ref, go_ref) = refs
    i = pl.program_id(0)

    def shifted(x_ref, prev_ref, init_ref):
        x = x_ref[...]
        if not halo:
            return x, prev_ref[...]
        last = prev_ref[SUBLANES - 1:SUBLANES, :]
        first = jnp.where(i == 0, init_ref[...], last)
        xs = pltpu.roll(x, 1, axis=0)
        rows = lax.broadcasted_iota(jnp.int32, x.shape, 0)
        return x, jnp.where(rows == 0, first, xs)

    def mix(x_ref, prev_ref, init_ref, mu_ref):
        x, xp = shifted(x_ref, prev_ref, init_ref)
        return x + (xp - x) * mu_ref[...]

    r = mix(r_ref, rp_ref, ri_ref if halo else None, mur_ref)
    k = mix(k_ref, kp_ref, ki_ref if halo else None, muk_ref)
    v = mix(v_ref, vp_ref, vi_ref if halo else None, muv_ref)
    lo = mix(l_ref, lp_ref, li_ref if halo else None, mul_ref)
    dl = w2_ref.shape[0]
    al = a2_ref.shape[0]
    xw = lo[:, :dl]
    xa = lo[:, dl:dl + al]
    xg = lo[:, dl + al:]
    wl = w0_ref[...] + _dg(jnp.tanh(xw), w2_ref[...], _NN, False)
    lw = -jnp.exp(-_softplus(-wl) - 0.5)
    a = _sigmoid(a0_ref[...] + _dg(xa, a2_ref[...], _NN, False))
    g = _dg(_sigmoid(xg), g2_ref[...], _NN, False)
    seg = _segsum_matrix(tc, rh)
    kk = k * kkw_ref[...]
    ss = _dg(kk * kk, seg, _NN, exact)
    kk = kk / jnp.maximum(jnp.sqrt(ss), 1e-12)
    kf = k * (1.0 + (a - 1.0) * kaw_ref[...])
    b = kk * a
    for pi in range(tc // LANES):
        sl = slice(pi * LANES, (pi + 1) * LANES)
        ro_ref[pi] = r[:, sl].astype(ro_ref.dtype)
        lwo_ref[pi] = lw[:, sl]
        ko_ref[pi] = kf[:, sl].astype(ko_ref.dtype)
        vo_ref[pi] = v[:, sl].astype(vo_ref.dtype)
        kko_ref[pi] = kk[:, sl].astype(kko_ref.dtype)
        bo_ref[pi] = b[:, sl].astype(bo_ref.dtype)
        go_ref[pi] = g[:, sl]


def _rwkv_prep(rkv, lora, prev, mu_rkv, mu_lora, w0, w2, a0, a2, g2p, k_k, k_a, *, rh, exact, tt=256, tc=512):
    T, W3 = rkv.shape
    W = W3 // 3
    LP = lora.shape[1]
    tc = _pick(W, tc)
    nj = W // tc
    halo = prev[0].shape[0] != T or T == 1
    tt = _pick(T, tt, SUBLANES)
    hb = tt // SUBLANES
    xblk = lambda off: pl.BlockSpec((tt, tc), lambda i, j: (i, off * nj + j))
    pblk = lambda off: pl.BlockSpec((SUBLANES, tc), lambda i, j: (jnp.maximum(i * hb - 1, 0), off * nj + j))
    cvec = lambda off: pl.BlockSpec((1, tc), lambda i, j: (0, off * nj + j))
    in_specs = [xblk(0), xblk(1), xblk(2), pl.BlockSpec((tt, LP), lambda i, j: (i, 0))]
    args = [rkv, rkv, rkv, lora]
    if halo:
        in_specs += [pblk(0), pblk(1), pblk(2),
                     pl.BlockSpec((SUBLANES, LP), lambda i, j: (jnp.maximum(i * hb - 1, 0), 0)),
                     cvec(0), cvec(1), cvec(2), pl.BlockSpec((1, LP), lambda i, j: (0, 0))]
        args += [rkv, rkv, rkv, lora, prev[0], prev[0], prev[0], prev[1]]
    else:
        in_specs += [xblk(0), xblk(1), xblk(2), pl.BlockSpec((tt, LP), lambda i, j: (i, 0))]
        args += [prev[0], prev[0], prev[0], prev[1]]
    in_specs += [cvec(0), cvec(1), cvec(2), pl.BlockSpec((1, LP), lambda i, j: (0, 0)),
                 cvec(0), pl.BlockSpec((w2.shape[0], tc), lambda i, j: (0, j)),
                 cvec(0), pl.BlockSpec((a2.shape[0], tc), lambda i, j: (0, j)),
                 pl.BlockSpec((g2p.shape[0], tc), lambda i, j: (0, j)),
                 cvec(0), cvec(0)]
    args += [mu_rkv, mu_rkv, mu_rkv, mu_lora, w0.reshape(1, W), w2, a0.reshape(1, W), a2, g2p,
             k_k.reshape(1, W), k_a.reshape(1, W)]
    npair = tc // LANES
    store = F32 if exact else BF16
    oshape = [jax.ShapeDtypeStruct((W // LANES, T, LANES), dt) for dt in (store, F32, store, store, store, store, F32)]
    ospec = pl.BlockSpec((npair, tt, LANES), lambda i, j: (j, i, 0))
    return pl.pallas_call(
        functools.partial(_prep_kernel, halo=halo, rh=rh, exact=exact, tc=tc),
        grid=(T // tt, nj),
        in_specs=in_specs,
        out_specs=[ospec] * 7,
        out_shape=oshape,
        compiler_params=_cparams("parallel", "arbitrary"),
        name="rwkv_prep",
    )(*args)


def _scan_kernel(r_ref, lw_ref, k_ref, v_ref, kk_ref, b_ref, g_ref, s0_ref, rk_ref, lnw_ref, lnb_ref,
                 y_ref, sout_ref, s_sc, *, C, rh, npair, precise, t_real):
    c = pl.program_id(1)
    R = 2 * C
    lane = lax.broadcasted_iota(jnp.int32, (C, LANES), 1)
    head0 = lane < rh
    ri = lax.broadcasted_iota(jnp.int32, (R, R), 0)
    ci = lax.broadcasted_iota(jnp.int32, (R, R), 1)
    same = (ri // C) == (ci // C)
    strict = jnp.logical_and(same, (ci % C) < (ri % C))
    incl = jnp.logical_and(same, (ci % C) <= (ri % C))
    tril_c = (lax.broadcasted_iota(jnp.int32, (C, C), 1)
              <= lax.broadcasted_iota(jnp.int32, (C, C), 0)).astype(F32)
    seg = _segsum_matrix(LANES, rh)
    same_head = seg > 0.5
    zero = jnp.zeros((rh, rh), F32)

    def stack_masked(x):
        return jnp.concatenate([jnp.where(head0, x, 0.0), jnp.where(head0, 0.0, x)], axis=0)

    def stack_dup(x):
        return jnp.concatenate([x, x], axis=0)

    @pl.when(c == 0)
    def _():
        def init(p, carry):
            top = jnp.concatenate([s0_ref[0, 2 * p], zero], axis=1)
            bot = jnp.concatenate([zero, s0_ref[0, 2 * p + 1]], axis=1)
            s_sc[p] = jnp.concatenate([top, bot], axis=0)
            return carry
        lax.fori_loop(0, npair, init, 0)

    def single_step(r, lw, k, v, kk, b, S):
        sa = yield -kk, S, _NT, False
        outer = yield jnp.concatenate([sa, v], axis=0), jnp.concatenate([b, k], axis=0), _TN, True
        s_new = S * jnp.exp(lw[0:1, :]) + jnp.where(same_head, outer, 0.0)
        y = yield r, s_new, _NT, False
        return y, s_new

    def chunk_steps(r, lw, k, v, kk, b, S):
        cum = yield tril_c, lw, _NN, True
        tot = cum[C - 1:C, :]
        e_incl = jnp.exp(cum)
        e_inv = jnp.exp(-cum)
        e_end = jnp.exp(tot - cum)
        a2 = stack_masked(-kk * jnp.exp(cum - lw))
        r2 = stack_masked(r * e_incl)
        v2 = stack_masked(v)
        bd2 = stack_masked(b * e_end)
        kd2 = stack_masked(k * e_end)
        b2 = stack_dup(b * e_inv)
        k2 = stack_dup(k * e_inv)
        ar = jnp.concatenate([a2, r2], axis=0)
        arb = yield ar, b2, _NT, precise
        ark = yield ar, k2, _NT, precise
        ars = yield ar, S, _NT, precise
        mrb = jnp.where(incl, arb[R:], 0.0)
        mrk = jnp.where(incl, ark[R:], 0.0)
        x = ars[:R]
        n_it = (t_real - 1).bit_length()
        if n_it:
            x = x + (yield jnp.where(strict, ark[:R], 0.0), v2, _NN, precise)
            pw = jnp.where(strict, arb[:R], 0.0)
        for it in range(n_it):
            x = x + (yield pw, x, _NN, precise)
            if it + 1 < n_it:
                pw = yield pw, pw, _NN, precise
        y2 = ars[R:] + (yield mrb, x, _NN, precise)
        y2 = y2 + (yield mrk, v2, _NN, precise)
        y = y2[:C] + y2[C:]
        s_new = S * jnp.exp(tot) + (yield x, bd2, _TN, precise)
        s_new = s_new + (yield v2, kd2, _TN, precise)
        return y, s_new

    def slab(r, lw, k, v, kk, b, g, S, rk, lnw, lnb):
        steps = single_step if t_real == 1 else chunk_steps
        y, s_new = yield from steps(r, lw, k, v, kk, b, S)
        mean = (yield y, seg, _NN, precise) * (1.0 / rh)
        d = y - mean
        var = (yield d * d, seg, _NN, precise) * (1.0 / rh)
        yn = d * lax.rsqrt(var + GN_EPS) * lnw + lnb
        bonus = (yield r * k * rk, seg, _NN, precise) * v
        return (yn + bonus) * g, s_new

    def run_lockstep(gens):
        reqs = [next(gen) for gen in gens]
        results = [None] * len(gens)
        live = list(range(len(gens)))
        while live:
            vals = [_dg(*reqs[i]) for i in live]
            nxt = []
            for i, val in zip(live, vals):
                try:
                    reqs[i] = gens[i].send(val)
                    nxt.append(i)
                except StopIteration as done:
                    results[i] = done.value
            live = nxt
        return results

    U = math.gcd(npair, SLABS_PER_GROUP)

    def body(i, carry):
        grp = pl.ds(pl.multiple_of(i * U, U), U)
        ins = [ref[grp].astype(F32) for ref in (r_ref, lw_ref, k_ref, v_ref, kk_ref, b_ref, g_ref, s_sc,
                                                rk_ref, lnw_ref, lnb_ref)]
        outs = run_lockstep([slab(*[a[u] for a in ins]) for u in range(U)])
        y_ref[grp] = jnp.stack([o[0] for o in outs]).astype(y_ref.dtype)
        s_sc[grp] = jnp.stack([o[1] for o in outs])
        return carry

    lax.fori_loop(0, npair // U, body, 0)

    @pl.when(c == pl.num_programs(1) - 1)
    def _():
        def fin(p, carry):
            S = s_sc[p]
            sout_ref[0, 2 * p] = S[:rh, :rh]
            sout_ref[0, 2 * p + 1] = S[rh:, rh:]
            return carry
        lax.fori_loop(0, npair, fin, 0)


def _rwkv_scan(prep, s0, r_k, ln_w, ln_b, *, B, C, rh, precise, out_dtype, t_real):
    npair, BT, _ = prep[0].shape
    Tb = BT // B
    nc = Tb // C
    H = s0.shape[1]
    xspec = pl.BlockSpec((npair, C, LANES), lambda b, c: (0, b * nc + c, 0))
    sspec = pl.BlockSpec((1, H, rh, rh), lambda b, c: (b, 0, 0, 0))
    pspec = pl.BlockSpec((npair, 1, LANES), lambda b, c: (0, 0, 0))
    y, s_out = pl.pallas_call(
        functools.partial(_scan_kernel, C=C, rh=rh, npair=npair, precise=precise, t_real=t_real),
        grid=(B, nc),
        in_specs=[xspec] * 7 + [sspec, pspec, pspec, pspec],
        out_specs=[xspec, sspec],
        out_shape=[jax.ShapeDtypeStruct((npair, BT, LANES), out_dtype),
                   jax.ShapeDtypeStruct((B, H, rh, rh), F32)],
        scratch_shapes=[pltpu.VMEM((npair, 2 * rh, 2 * rh), F32)],
        compiler_params=_cparams("parallel", "arbitrary"),
        name="rwkv_scan",
    )(*prep, s0, r_k.reshape(npair, 1, LANES), ln_w.reshape(npair, 1, LANES), ln_b.reshape(npair, 1, LANES))
    return y, s_out


def _router_kernel(x_ref, nw_ref, rw_ref, rb_ref, h_ref, lg_ref, *, precise):
    x = x_ref[...]
    ms = jnp.mean(x * x, axis=-1, keepdims=True)
    h = x * lax.rsqrt(ms + NORM_EPS) * nw_ref[...]
    h_ref[...] = h.astype(h_ref.dtype)
    lg_ref[...] = _dg(h, rw_ref[...], _NN, precise) + rb_ref[...]


def _router(x, norm_w, rw, rb, *, precise, h_dtype, tm=512):
    M, D = x.shape
    NR = rw.shape[1]
    tm = _pick(M, tm, SUBLANES)
    return pl.pallas_call(
        functools.partial(_router_kernel, precise=precise),
        grid=(M // tm,),
        in_specs=[pl.BlockSpec((tm, D), lambda i: (i, 0)), pl.BlockSpec((1, D), lambda i: (0, 0)),
                  pl.BlockSpec((D, NR), lambda i: (0, 0)), pl.BlockSpec((1, NR), lambda i: (0, 0))],
        out_specs=[pl.BlockSpec((tm, D), lambda i: (i, 0)), pl.BlockSpec((tm, NR), lambda i: (i, 0))],
        out_shape=[jax.ShapeDtypeStruct((M, D), h_dtype), jax.ShapeDtypeStruct((M, NR), F32)],
        compiler_params=_cparams("parallel"),
        name="moe_router",
    )(x, norm_w.reshape(1, D), rw, rb)


def _moe_kernel(be_ref, nu_ref, xr_ref, x_ref, wg_ref, wu_ref, wd_ref, sw_ref, o_ref):
    b = pl.program_id(0)
    j = pl.program_id(1)
    last = j == pl.num_programs(1) - 1
    used = b < nu_ref[0]

    @pl.when(jnp.logical_and(used, j == 0))
    def _():
        o_ref[...] = jnp.zeros(o_ref.shape, F32)

    @pl.when(used)
    def _():
        x = x_ref[...]
        tn = wg_ref.shape[2]
        wgu = jnp.concatenate([wg_ref[0].astype(BF16), wu_ref[0].astype(BF16)], axis=1)
        hgu = jnp.dot(x, wgu, preferred_element_type=F32)
        hg, hu = hgu[:, :tn], hgu[:, tn:]
        h = hg * _sigmoid(hg) * hu
        o_ref[...] += jnp.dot(h.astype(BF16), wd_ref[0].astype(BF16), preferred_element_type=F32)

    @pl.when(jnp.logical_and(used, last))
    def _():
        o_ref[...] = o_ref[...] * sw_ref[...]

    @pl.when(jnp.logical_and(jnp.logical_not(used), last))
    def _():
        o_ref[...] = jnp.zeros(o_ref.shape, F32)


def _moe_experts(xs, x_row0, blk_e, n_used, slot_w, wg, wu, wd, *, blk, tn):
    D = xs.shape[1]
    P = slot_w.shape[0]
    DE = wg.shape[2]
    tn = _pick(DE, tn)
    nj = DE // tn
    nb = P // blk

    def eff(b, j, nu):
        live = b < nu[0]
        return jnp.where(live, b, nu[0] - 1), jnp.where(live, j, nj - 1)

    def wg_map(b, j, be, nu, xr):
        bb, jj = eff(b, j, nu)
        return be[bb], 0, jj

    def wd_map(b, j, be, nu, xr):
        bb, jj = eff(b, j, nu)
        return be[bb], jj, 0

    grid_spec = pltpu.PrefetchScalarGridSpec(
        num_scalar_prefetch=3,
        grid=(nb, nj),
        in_specs=[pl.BlockSpec((pl.Element(blk), pl.Element(D)),
                               lambda b, j, be, nu, xr: (pl.multiple_of(xr[eff(b, j, nu)[0]], 2 * SUBLANES), 0)),
                  pl.BlockSpec((1, D, tn), wg_map), pl.BlockSpec((1, D, tn), wg_map),
                  pl.BlockSpec((1, tn, D), wd_map),
                  pl.BlockSpec((blk, 1), lambda b, j, be, nu, xr: (eff(b, j, nu)[0], 0))],
        out_specs=pl.BlockSpec((blk, D), lambda b, j, be, nu, xr: (b, 0)),
    )
    return pl.pallas_call(
        _moe_kernel,
        grid_spec=grid_spec,
        out_shape=jax.ShapeDtypeStruct((P, D), F32),
        compiler_params=_cparams("arbitrary", "arbitrary"),
        name="moe_experts",
    )(blk_e, n_used, x_row0, xs, wg, wu, wd, slot_w.reshape(P, 1))


def _hier_moe(x_groups, norm_w, rg_w, rg_b, re_w, re_b, wg, wu, wd, *, blk, tn):
    D = x_groups[0].shape[1]
    NG = rg_w.shape[1]
    E = re_w.shape[1]
    EPG = E // NG
    NR = -(-(NG + E) // LANES) * LANES
    rw = jnp.pad(jnp.concatenate([rg_w, re_w], axis=1), ((0, 0), (0, NR - NG - E))).astype(BF16)
    rb = jnp.pad(jnp.concatenate([rg_b, re_b]), (0, NR - NG - E)).reshape(1, NR)
    routed = [_router(x, norm_w, rw, rb, precise=False, h_dtype=BF16) for x in x_groups]
    logits = jnp.concatenate([r[1] for r in routed], axis=0)
    T = logits.shape[0]
    gprob = jax.nn.softmax(logits[:, :NG], axis=-1)
    gsel = jnp.argmax(gprob, axis=-1)
    gp = jnp.take_along_axis(gprob, gsel[:, None], axis=-1)[:, 0]
    elog = logits[:, NG:NG + E].reshape(T, NG, EPG)
    elog = jnp.take_along_axis(elog, gsel[:, None, None], axis=1)[:, 0]
    top_p, top_i = lax.top_k(jax.nn.softmax(elog, axis=-1), TOP_K)
    top_p = top_p / jnp.sum(top_p, axis=-1, keepdims=True)
    weights = gp[:, None] * top_p
    expert_idx = gsel[:, None].astype(jnp.int32) * EPG + top_i.astype(jnp.int32)

    TK = T * TOP_K
    nb = -(-TK // blk) + E
    P = nb * blk
    flat_e = expert_idx.reshape(-1)
    flat_w = weights.reshape(-1)
    flat_t = jnp.repeat(jnp.arange(T, dtype=jnp.int32), TOP_K)
    order = jnp.argsort(flat_e)
    se = flat_e[order]
    counts = jnp.bincount(flat_e, length=E)
    pcounts = (counts + blk - 1) // blk * blk
    pend = jnp.cumsum(pcounts)
    pstart = pend - pcounts
    start = jnp.cumsum(counts) - counts
    dest = (pstart[se] + jnp.arange(TK) - start[se]).astype(jnp.int32)
    blk_e = jnp.minimum(jnp.searchsorted(pend, jnp.arange(nb) * blk, side='right'), E - 1).astype(jnp.int32)
    n_used = (pend[-1] // blk).astype(jnp.int32).reshape(1)
    p_idx = jnp.arange(P, dtype=jnp.int32)
    e_p = blk_e[p_idx // blk]
    rank_p = p_idx - pstart[e_p].astype(jnp.int32)
    src_p = jnp.clip(start[e_p].astype(jnp.int32) + rank_p, 0, TK - 1)
    slot_w = jnp.where(rank_p < counts[e_p], flat_w[order][src_p], 0.0).astype(F32)
    seg = 2 * SUBLANES
    ccounts = (counts + seg - 1) // seg * seg
    cend = jnp.cumsum(ccounts)
    cstart = cend - ccounts
    n_rows = -(-(TK + seg * E) // seg) * seg + blk
    c_idx = jnp.arange(n_rows, dtype=jnp.int32)
    e_c = jnp.minimum(jnp.searchsorted(cend, c_idx, side='right'), E - 1)
    rank_c = c_idx - cstart[e_c].astype(jnp.int32)
    src_c = jnp.clip(start[e_c].astype(jnp.int32) + rank_c, 0, TK - 1)
    slot_t = jnp.where(rank_c < counts[e_c], flat_t[order][src_c], T).astype(jnp.int32)
    pos = dest[jnp.argsort(order)]

    x_row0 = (cstart[blk_e] + (jnp.arange(nb, dtype=jnp.int32) - pstart[blk_e] // blk) * blk).astype(jnp.int32)
    x_row0 = jnp.clip(x_row0, 0, n_rows - blk)
    xs = jnp.concatenate([r[0] for r in routed] + [jnp.zeros((1, D), BF16)], axis=0)[slot_t]
    ys = _moe_experts(xs, x_row0, blk_e, n_used, slot_w, wg, wu, wd, blk=blk, tn=tn)
    pos = pos.reshape(T, TOP_K)
    outs, t0 = [], 0
    for x in x_groups:
        pg = pos[t0:t0 + x.shape[0]]
        outs.append(x + ys[pg[:, 0]] + ys[pg[:, 1]])
        t0 += x.shape[0]
    return outs


def _pairs_to_rows(y):
    npair, T, _ = y.shape
    return jnp.transpose(y, (1, 0, 2)).reshape(T, npair * LANES)


def _mixer(x, positions, prev_rows, s0, B, cfg, w, *, exact, attn_fn, q_scale, emit_bf16):
    T, D = x.shape
    hd, HK, G, vd, rh = cfg['hd'], cfg['HK'], cfg['G'], cfg['vd'], cfg['rh']
    qcols, kcols, vcols = HK * G * 2 * hd, HK * 2 * hd, HK * vd
    W = cfg['W']
    mm = functools.partial(_matmul, precise=False)

    h = _rmsnorm(x, w['norm1_w'], BF16)
    qkv = mm(h, w['w_in'], qcols + kcols + vcols, 0)
    rkv = mm(h, w['w_in'], 3 * W, cfg['r_off'])
    lora = mm(h, w['w_in'], cfg['lp'], cfg['r_off'] + 3 * W, tn=cfg['lp'])
    gates = mm(h, w['w_gates'], 2 * D)

    cos_full, sin_signed = _rope_tables(positions, hd)
    outs = _qk_rope(qkv, cos_full, sin_signed, w['q_norm_w'], w['k_norm_w'], qcols, kcols, vcols,
                    q_scale=q_scale, q_dtype=BF16, emit_bf16=emit_bf16)
    q, k_new, v_new = outs[:3]
    o_attn = attn_fn(q, outs)

    prep = _rwkv_prep(rkv, lora, prev_rows, w['mu_rkv'], w['mu_lora'], w['rwkv_w0'], w['rwkv_w2'],
                      w['rwkv_a0'], w['rwkv_a2'], w['g2p'], w['rwkv_k_k'], w['rwkv_k_a'],
                      rh=rh, exact=exact)
    Tb = T // B
    if Tb % 64 == 0:
        C = 64
    else:
        C = SUBLANES
        pad = -Tb % C
        prep = [jnp.pad(a.reshape(-1, B, Tb, LANES), ((0, 0), (0, 0), (0, pad), (0, 0)))
                .reshape(a.shape[0], B * (Tb + pad), LANES) for a in prep]
    y_r, s_fin = _rwkv_scan(prep, s0, w['rwkv_r_k'], w['rwkv_ln_w'], w['rwkv_ln_b'],
                            B=B, C=C, rh=rh, precise=exact, out_dtype=BF16, t_real=min(Tb, C))
    if Tb % 64:
        y_r = y_r.reshape(y_r.shape[0], B, -1, LANES)[:, :, :Tb].reshape(y_r.shape[0], T, LANES)
    o_rwkv = _pairs_to_rows(y_r)

    mixed = _merge(o_attn, w['w_branch_attn'], o_rwkv, w['w_branch_rwkv'], gates, w['gate_b'],
                   precise=False, out_dtype=BF16)
    x1 = _matmul_residual(mixed, w['w_out'], x, precise=False)
    return x1, k_new, v_new, s_fin, rkv, lora


def kernel(x_prompt, x_sample, cache_k, cache_v, state_rwkv, state_shift, page_table, norm1_w, w_in, gate_b, q_norm_w, k_norm_w, lambda_q1, lambda_k1, lambda_q2, lambda_k2, subln_w, rwkv_mu, rwkv_w0, rwkv_w2, rwkv_a0, rwkv_a2, rwkv_g2, rwkv_k_k, rwkv_k_a, rwkv_r_k, rwkv_ln_w, rwkv_ln_b, w_branch_attn, w_branch_rwkv, w_out, norm2_w, router_group_w, router_group_b, router_expert_w, router_expert_b, expert_w_gate, expert_w_up, expert_w_down):
    B, T, D = x_prompt.shape
    DB, S, _ = x_sample.shape
    L = norm1_w.shape[0]
    assert L == 1 and B == 1 and S == 1, "one trunk layer, one prompt sequence, one new token per sample"
    hd = q_norm_w.shape[1]
    HK = cache_k.shape[3]
    vd = cache_v.shape[4]
    G = w_branch_attn.shape[1] // vd // HK
    H, rh = rwkv_r_k.shape[1], rwkv_r_k.shape[2]
    W = H * rh
    dl, al, gl = rwkv_w2.shape[1], rwkv_a2.shape[1], rwkv_g2.shape[1]
    lcols = dl + al + gl
    lp = -(-lcols // LANES) * LANES
    qkv_cols = HK * G * 2 * hd + HK * 2 * hd + HK * vd
    r_off = qkv_cols
    g_off = r_off + 3 * W + lcols
    n_pages, page = page_table.shape[1], cache_k.shape[2]
    past = n_pages * page
    cfg = dict(hd=hd, HK=HK, G=G, vd=vd, rh=rh, W=W, lp=lp, r_off=r_off)
    l = 0
    (cache_k, cache_v, state_rwkv, state_shift, norm1_w, w_in, gate_b, q_norm_w, k_norm_w, lambda_q1,
     lambda_k1, lambda_q2, lambda_k2, subln_w, rwkv_mu, rwkv_w0, rwkv_w2, rwkv_a0, rwkv_a2, rwkv_g2,
     rwkv_k_k, rwkv_k_a, rwkv_r_k, rwkv_ln_w, rwkv_ln_b, w_branch_attn, w_branch_rwkv, w_out, norm2_w,
     router_group_w, router_group_b, router_expert_w, router_expert_b, expert_w_gate, expert_w_up,
     expert_w_down) = [a.reshape(a.shape[1:]) for a in (
         cache_k, cache_v, state_rwkv, state_shift, norm1_w, w_in, gate_b, q_norm_w, k_norm_w, lambda_q1,
         lambda_k1, lambda_q2, lambda_k2, subln_w, rwkv_mu, rwkv_w0, rwkv_w2, rwkv_a0, rwkv_a2, rwkv_g2,
         rwkv_k_k, rwkv_k_a, rwkv_r_k, rwkv_ln_w, rwkv_ln_b, w_branch_attn, w_branch_rwkv, w_out, norm2_w,
         router_group_w, router_group_b, router_expert_w, router_expert_b, expert_w_gate, expert_w_up,
         expert_w_down)]
    lam_init = 0.8 - 0.6 * math.exp(-0.3 * l)
    lams = (lambda_q1, lambda_k1, lambda_q2, lambda_k2)

    mu = rwkv_mu
    g2p = jnp.pad(rwkv_g2, ((0, lp - lcols), (0, 0)))
    w_in_bf = w_in.astype(BF16)
    w = dict(
        norm1_w=norm1_w, q_norm_w=q_norm_w, k_norm_w=k_norm_w, gate_b=gate_b,
        mu_rkv=mu[:3 * W].reshape(1, 3 * W), mu_lora=jnp.pad(mu[3 * W:], (0, lp - lcols)).reshape(1, lp),
        rwkv_w0=rwkv_w0, rwkv_a0=rwkv_a0, rwkv_k_k=rwkv_k_k, rwkv_k_a=rwkv_k_a,
        rwkv_r_k=rwkv_r_k, rwkv_ln_w=rwkv_ln_w, rwkv_ln_b=rwkv_ln_b,
        norm2_w=norm2_w, router_group_w=router_group_w, router_group_b=router_group_b,
        router_expert_w=router_expert_w, router_expert_b=router_expert_b,
        expert_w_gate=expert_w_gate, expert_w_up=expert_w_up, expert_w_down=expert_w_down,
        w_in=w_in_bf, w_gates=w_in_bf[:, g_off:g_off + 2 * D],
        rwkv_w2=rwkv_w2.astype(BF16), rwkv_a2=rwkv_a2.astype(BF16), g2p=g2p.astype(BF16),
        w_branch_attn=w_branch_attn.astype(BF16), w_branch_rwkv=w_branch_rwkv.astype(BF16),
        w_out=w_out.astype(BF16))

    def prompt_attn(q, outs):
        return _flash_diff_attn(q, outs[3], outs[4], lams, subln_w, HK=HK, G=G, hd=hd, vd=vd,
                                lam_init=lam_init, out_dtype=BF16)

    zero_prev = (jnp.zeros((1, 3 * W), F32), jnp.zeros((1, lp), F32))
    xp1, kp, vp, sp, rkv_p, lora_p = _mixer(
        x_prompt.reshape(T, D), jnp.arange(T), zero_prev, jnp.zeros((1, H, rh, rh), F32), 1, cfg, w,
        exact=False, attn_fn=prompt_attn, q_scale=hd ** -0.5 * math.log2(math.e), emit_bf16=True)
    shift_p = jnp.concatenate([rkv_p[T - 1], lora_p[T - 1, :lcols]])

    def sample_attn(q, outs):
        return _decode_diff_attn(q, outs[1], outs[2], cache_k, cache_v, page_table, lams, subln_w,
                                 HK=HK, G=G, hd=hd, vd=vd, lam_init=lam_init)

    sh = state_shift
    prev_s = (sh[:, :3 * W], jnp.pad(sh[:, 3 * W:], ((0, 0), (0, lp - lcols))))
    xs1, ks, vs, ss, rkv_s, lora_s = _mixer(
        x_sample.reshape(DB, D), jnp.full((DB,), past, jnp.int32), prev_s, state_rwkv, DB, cfg, w,
        exact=True, attn_fn=sample_attn, q_scale=1.0, emit_bf16=False)
    shift_s = jnp.concatenate([rkv_s, lora_s[:, :lcols]], axis=1)


    yp, ys = _hier_moe([xp1, xs1], norm2_w, router_group_w, router_group_b, router_expert_w,
                       router_expert_b, expert_w_gate, expert_w_up, expert_w_down,
                       blk=min(MOE_ROW_BLOCK, max(2 * SUBLANES, T // 16)), tn=MOE_HIDDEN_TILE)

    return (yp.reshape(1, T, D), ys.reshape(DB, 1, D),
            kp.reshape(1, 1, T, HK, 2, hd), vp.reshape(1, 1, T, HK, vd),
            sp.reshape(1, 1, H, rh, rh), shift_p.reshape(1, 1, -1),
            ks.reshape(1, DB, 1, HK, 2, hd), vs.reshape(1, DB, 1, HK, vd),
            ss.reshape(1, DB, H, rh, rh), shift_s.reshape(1, DB, -1))
```

```python
import functools
import math

import jax
import jax.numpy as jnp
from jax import lax
from jax.experimental import pallas as pl
from jax.experimental.pallas import tpu as pltpu

F32 = jnp.float32
BF16 = jnp.bfloat16
HIGHEST = lax.Precision.HIGHEST

LANES = 128
SUBLANES = 8
VMEM_LIMIT = 56 * 1024 * 1024

ROPE_THETA = 10000.0
NORM_EPS = 1e-6
GN_EPS = 64e-5
TOP_K = 2
SLABS_PER_GROUP = 16
DECODE_PAGES_PER_STEP = 8
MOE_ROW_BLOCK = 512
MOE_HIDDEN_TILE = 256

_NN = (((1,), (0,)), ((), ()))
_NT = (((1,), (1,)), ((), ()))
_TN = (((0,), (0,)), ((), ()))


def _cparams(*sem):
    return pltpu.CompilerParams(dimension_semantics=sem, vmem_limit_bytes=VMEM_LIMIT)


def _pick(n, pref, unit=LANES):
    if n <= pref:
        return n
    t = pref // unit * unit
    while t > unit and n % t:
        t -= unit
    assert n % t == 0, (n, pref)
    return t


def _dg(a, b, dn, precise):
    if precise:
        return lax.dot_general(a.astype(F32), b.astype(F32), dn, precision=HIGHEST,
                               preferred_element_type=F32)
    return lax.dot_general(a.astype(BF16), b.astype(BF16), dn, preferred_element_type=F32)


def _sigmoid(x):
    return 1.0 / (1.0 + jnp.exp(-x))


def _softplus(x):
    return jnp.maximum(x, 0.0) + jnp.log(1.0 + jnp.exp(-jnp.abs(x)))


def _rms_kernel(x_ref, w_ref, o_ref, *, eps):
    x = x_ref[...]
    ms = jnp.mean(x * x, axis=-1, keepdims=True)
    o_ref[...] = (x * lax.rsqrt(ms + eps) * w_ref[...]).astype(o_ref.dtype)


def _rmsnorm(x, w, out_dtype, tm=512):
    M, D = x.shape
    tm = _pick(M, tm, SUBLANES)
    return pl.pallas_call(
        functools.partial(_rms_kernel, eps=NORM_EPS),
        grid=(M // tm,),
        in_specs=[pl.BlockSpec((tm, D), lambda i: (i, 0)),
                  pl.BlockSpec((1, D), lambda i: (0, 0))],
        out_specs=pl.BlockSpec((tm, D), lambda i: (i, 0)),
        out_shape=jax.ShapeDtypeStruct((M, D), out_dtype),
        compiler_params=_cparams("parallel"),
        name="rmsnorm",
    )(x, w.reshape(1, D))


def _mm_kernel(a_ref, b_ref, o_ref, *, precise):
    o_ref[...] = _dg(a_ref[...], b_ref[...], _NN, precise).astype(o_ref.dtype)


def _matmul(a, b, n, col0=0, *, precise, out_dtype=F32, tm=1024, tn=512):
    M, K = a.shape
    tm = _pick(M, tm, SUBLANES)
    tn = _pick(n, tn)
    if col0 % tn:
        b, col0 = b[:, col0:col0 + n], 0
    c0 = col0 // tn
    return pl.pallas_call(
        functools.partial(_mm_kernel, precise=precise),
        grid=(M // tm, n // tn),
        in_specs=[pl.BlockSpec((tm, K), lambda i, j: (i, 0)),
                  pl.BlockSpec((K, tn), lambda i, j: (0, c0 + j))],
        out_specs=pl.BlockSpec((tm, tn), lambda i, j: (i, j)),
        out_shape=jax.ShapeDtypeStruct((M, n), out_dtype),
        compiler_params=_cparams("parallel", "arbitrary"),
        name="matmul",
    )(a, b)


def _merge_kernel(a1_ref, b1_ref, a2_ref, b2_ref, g1_ref, g2_ref, gb1_ref, gb2_ref, o_ref, *, precise):
    o1 = _dg(a1_ref[...], b1_ref[...], _NN, precise)
    o2 = _dg(a2_ref[...], b2_ref[...], _NN, precise)
    s1 = _sigmoid(g1_ref[...] + gb1_ref[...])
    s2 = _sigmoid(g2_ref[...] + gb2_ref[...])
    o_ref[...] = (s1 * o1 + s2 * o2).astype(o_ref.dtype)


def _merge(a1, b1, a2, b2, gates, gate_b, *, precise, out_dtype, tm=512, tn=512):
    M, K = a1.shape
    K2 = a2.shape[1]
    D = b1.shape[1]
    tm = _pick(M, tm, SUBLANES)
    tn = _pick(D, tn)
    nj = D // tn
    gb = gate_b.reshape(1, 2 * D)
    return pl.pallas_call(
        functools.partial(_merge_kernel, precise=precise),
        grid=(M // tm, nj),
        in_specs=[pl.BlockSpec((tm, K), lambda i, j: (i, 0)),
                  pl.BlockSpec((K, tn), lambda i, j: (0, j)),
                  pl.BlockSpec((tm, K2), lambda i, j: (i, 0)),
                  pl.BlockSpec((K2, tn), lambda i, j: (0, j)),
                  pl.BlockSpec((tm, tn), lambda i, j: (i, j)),
                  pl.BlockSpec((tm, tn), lambda i, j: (i, nj + j)),
                  pl.BlockSpec((1, tn), lambda i, j: (0, j)),
                  pl.BlockSpec((1, tn), lambda i, j: (0, nj + j))],
        out_specs=pl.BlockSpec((tm, tn), lambda i, j: (i, j)),
        out_shape=jax.ShapeDtypeStruct((M, D), out_dtype),
        compiler_params=_cparams("parallel", "arbitrary"),
        name="merge",
    )(a1, b1, a2, b2, gates, gates, gb, gb)


def _mm_res_kernel(a_ref, b_ref, r_ref, o_ref, *, precise):
    o_ref[...] = r_ref[...] + _dg(a_ref[...], b_ref[...], _NN, precise)


def _matmul_residual(a, b, res, *, precise, tm=1024, tn=512):
    M, K = a.shape
    N = b.shape[1]
    tm = _pick(M, tm, SUBLANES)
    tn = _pick(N, tn)
    return pl.pallas_call(
        functools.partial(_mm_res_kernel, precise=precise),
        grid=(M // tm, N // tn),
        in_specs=[pl.BlockSpec((tm, K), lambda i, j: (i, 0)),
                  pl.BlockSpec((K, tn), lambda i, j: (0, j)),
                  pl.BlockSpec((tm, tn), lambda i, j: (i, j))],
        out_specs=pl.BlockSpec((tm, tn), lambda i, j: (i, j)),
        out_shape=jax.ShapeDtypeStruct((M, N), F32),
        compiler_params=_cparams("parallel", "arbitrary"),
        name="matmul_residual",
    )(a, b, res)


def _qkrope_kernel(qkv_ref, cos_ref, sin_ref, qw_ref, kw_ref, *out_refs, nq, nk, hd, vcols, vd, scale, emit_bf16):
    if emit_bf16:
        q_ref, k_ref, v_ref, kb_ref, vb_ref = out_refs
    else:
        q_ref, k_ref, v_ref = out_refs
    cos = cos_ref[...]
    sin = sin_ref[...]

    def norm_rope(x, w):
        ms = jnp.mean(x * x, axis=-1, keepdims=True)
        xn = x * lax.rsqrt(ms + NORM_EPS) * w
        return xn * cos + pltpu.roll(xn, hd // 2, axis=1) * sin

    qw = qw_ref[...]
    kw = kw_ref[...]
    for c in range(nq):
        o = norm_rope(qkv_ref[:, c * hd:(c + 1) * hd], qw) * scale
        q_ref[:, c * hd:(c + 1) * hd] = o.astype(q_ref.dtype)
    ks = []
    for c in range(nk):
        o = norm_rope(qkv_ref[:, (nq + c) * hd:(nq + c + 1) * hd], kw)
        ks.append(o)
        if emit_bf16:
            kb_ref[:, c * hd:(c + 1) * hd] = o.astype(BF16)
    v = qkv_ref[:, (nq + nk) * hd:(nq + nk) * hd + vcols]
    if emit_bf16:
        vb_ref[...] = v.astype(BF16)
    tq = v.shape[0]
    k_ref[...] = jnp.concatenate(ks, axis=1).reshape(tq * nk, hd)
    nt = vd // LANES
    vt = [v[:, (hk * nt + et) * LANES:(hk * nt + et + 1) * LANES] for et in range(nt) for hk in range(vcols // vd)]
    v_ref[...] = jnp.concatenate(vt, axis=1).reshape(tq * (vcols // LANES), LANES)


def _kv_rows(k_rows, v_rows, T, HK, hd, vd):
    k = k_rows.reshape(T, HK, 2, hd)
    v = jnp.transpose(v_rows.reshape(T, vd // LANES, HK, LANES), (0, 2, 1, 3)).reshape(T, HK, vd)
    return k, v


def _qk_rope(qkv, cos_full, sin_signed, q_norm_w, k_norm_w, qcols, kcols, vcols, vd, *, q_scale, q_dtype,
             emit_bf16, tq=256):
    T = qkv.shape[0]
    hd = q_norm_w.shape[0]
    assert hd == LANES and vd % LANES == 0
    tq = _pick(T, tq, SUBLANES)
    nq, nk = qcols // hd, kcols // hd
    nv = vcols // LANES
    row = lambda i: (i, 0)
    out_shape = [jax.ShapeDtypeStruct((T, qcols), q_dtype),
                 jax.ShapeDtypeStruct((T * nk, hd), F32),
                 jax.ShapeDtypeStruct((T * nv, LANES), F32)]
    out_specs = [pl.BlockSpec((tq, qcols), row), pl.BlockSpec((tq * nk, hd), row), pl.BlockSpec((tq * nv, LANES), row)]
    if emit_bf16:
        out_shape += [jax.ShapeDtypeStruct((T, kcols), BF16), jax.ShapeDtypeStruct((T, vcols), BF16)]
        out_specs += [pl.BlockSpec((tq, kcols), row), pl.BlockSpec((tq, vcols), row)]
    return pl.pallas_call(
        functools.partial(_qkrope_kernel, nq=nq, nk=nk, hd=hd, vcols=vcols, vd=vd, scale=q_scale,
                          emit_bf16=emit_bf16),
        grid=(T // tq,),
        in_specs=[pl.BlockSpec((tq, qkv.shape[1]), row),
                  pl.BlockSpec((tq, hd), row), pl.BlockSpec((tq, hd), row),
                  pl.BlockSpec((1, hd), lambda i: (0, 0)), pl.BlockSpec((1, hd), lambda i: (0, 0))],
        out_specs=out_specs,
        out_shape=out_shape,
        compiler_params=_cparams("parallel"),
        name="qk_norm_rope",
    )(qkv, cos_full, sin_signed, q_norm_w.reshape(1, hd), k_norm_w.reshape(1, hd))


def _rope_tables(positions, hd):
    inv = 1.0 / (ROPE_THETA ** (jnp.arange(0, hd, 2, dtype=F32) / hd))
    ang = positions.astype(F32)[:, None] * inv[None, :]
    cos, sin = jnp.cos(ang), jnp.sin(ang)
    return jnp.concatenate([cos, cos], axis=-1), jnp.concatenate([-sin, sin], axis=-1)


def _diff_lambda_in_kernel(lq1_ref, lk1_ref, lq2_ref, lk2_ref, lam_init):
    e1 = jnp.exp(jnp.sum(lq1_ref[...] * lk1_ref[...], axis=-1, keepdims=True))
    e2 = jnp.exp(jnp.sum(lq2_ref[...] * lk2_ref[...], axis=-1, keepdims=True))
    return e1 - e2 + lam_init


def _attn_finalize(acc1, l1, acc2, l2, lam, subw, lam_init):
    o = acc1 / l1 - lam * (acc2 / l2)
    ms = jnp.mean(o * o, axis=-1, keepdims=True)
    return o * lax.rsqrt(ms + NORM_EPS) * subw * (1.0 - lam_init)


def _flash_kernel(qi_ref, ki_ref, q_ref, k_ref, v_ref, lq1_ref, lk1_ref, lq2_ref, lk2_ref, subw_ref, o_ref,
                  m_sc, l_sc, acc_sc, s_sc, *, G, hd, vd, tq, lam_init):
    qi = qi_ref[pl.program_id(1)]
    ki = ki_ref[pl.program_id(1)]

    @pl.when(ki == 0)
    def _():
        m_sc[...] = jnp.full(m_sc.shape, -jnp.inf, F32)
        l_sc[...] = jnp.zeros(l_sc.shape, F32)
        acc_sc[...] = jnp.zeros(acc_sc.shape, F32)

    def tile(diagonal):
        v = v_ref[...]
        for m in range(2):
            kb = k_ref[:, m * hd:(m + 1) * hd]
            for g in range(G):
                idx = g * 2 + m
                q = q_ref[:, idx * hd:(idx + 1) * hd]
                s = lax.dot_general(q, kb, _NT, preferred_element_type=F32)
                if diagonal:
                    rows = lax.broadcasted_iota(jnp.int32, (tq, tq), 0)
                    cols = lax.broadcasted_iota(jnp.int32, (tq, tq), 1)
                    s = jnp.where(cols <= rows, s, -jnp.inf)
                s_sc[...] = s
                m_prev = m_sc[idx]
                m_new = jnp.maximum(m_prev, jnp.max(s_sc[...], axis=-1, keepdims=True))
                m_sc[idx] = m_new
                alpha = jnp.exp2(m_prev - m_new)
                p = jnp.exp2(s_sc[...] - jnp.tile(m_new, (1, tq // LANES)))
                l_sc[idx] = alpha * l_sc[idx] + jnp.sum(p, axis=-1, keepdims=True)
                pv = jnp.dot(p.astype(BF16), v, preferred_element_type=F32)
                acc_sc[idx] = jnp.tile(alpha, (1, vd // LANES)) * acc_sc[idx] + pv

    @pl.when(ki < qi)
    def _():
        tile(False)

    @pl.when(ki == qi)
    def _():
        tile(True)
        lam = _diff_lambda_in_kernel(lq1_ref, lk1_ref, lq2_ref, lk2_ref, lam_init)
        subw = subw_ref[...]
        for g in range(G):
            o = _attn_finalize(acc_sc[2 * g], l_sc[2 * g][:, :1], acc_sc[2 * g + 1], l_sc[2 * g + 1][:, :1],
                               lam, subw, lam_init)
            o_ref[:, g * vd:(g + 1) * vd] = o.astype(o_ref.dtype)


def _flash_diff_attn(q, kb, vb, lams, subln_w, *, HK, G, hd, vd, lam_init, out_dtype, tq=512):
    T = q.shape[0]
    tq = _pick(T, tq)
    nq = T // tq
    vec = lambda n: pl.BlockSpec((1, n), lambda h, s, qt, kt: (0, 0))
    pairs = [(i, j) for i in range(nq) for j in range(i + 1)]
    qi_tab = jnp.asarray([p[0] for p in pairs], jnp.int32)
    ki_tab = jnp.asarray([p[1] for p in pairs], jnp.int32)
    grid_spec = pltpu.PrefetchScalarGridSpec(
        num_scalar_prefetch=2,
        grid=(HK, len(pairs)),
        in_specs=[pl.BlockSpec((tq, G * 2 * hd), lambda h, s, qt, kt: (qt[s], h)),
                  pl.BlockSpec((tq, 2 * hd), lambda h, s, qt, kt: (kt[s], h)),
                  pl.BlockSpec((tq, vd), lambda h, s, qt, kt: (kt[s], h)),
                  vec(hd), vec(hd), vec(hd), vec(hd), vec(vd)],
        out_specs=pl.BlockSpec((tq, G * vd), lambda h, s, qt, kt: (qt[s], h)),
        scratch_shapes=[pltpu.VMEM((2 * G, tq, LANES), F32), pltpu.VMEM((2 * G, tq, LANES), F32),
                        pltpu.VMEM((2 * G, tq, vd), F32), pltpu.VMEM((tq, tq), F32)],
    )
    return pl.pallas_call(
        functools.partial(_flash_kernel, G=G, hd=hd, vd=vd, tq=tq, lam_init=lam_init),
        grid_spec=grid_spec,
        out_shape=jax.ShapeDtypeStruct((T, HK * G * vd), out_dtype),
        compiler_params=_cparams("parallel", "arbitrary"),
        name="flash_diff_attn",
    )(qi_tab, ki_tab, q, kb, vb, *[x.reshape(1, -1) for x in lams], subln_w.reshape(1, vd))


def _decode_kernel(pt_ref, q_ref, kn_ref, vn_ref, *refs, HK, G, vd, pps, scale, lam_init):
    kc_refs, vc_refs = refs[:pps], refs[pps:2 * pps]
    (lq1_ref, lk1_ref, lq2_ref, lk2_ref, subw_ref, o_ref, s_sc, a_sc, anew_sc, acc_sc) = refs[2 * pps:]
    ph = pl.program_id(1)
    pg = pl.program_id(2)
    last = pl.num_programs(2) - 1
    NI = q_ref.shape[1]
    half = NI // 2
    q = q_ref[0]
    nk = q.shape[1] // LANES
    page = kc_refs[0].shape[0] // nk
    nv = vc_refs[0].shape[0] // page
    v_order = [et * HK + hk for hk in range(HK) for et in range(nv // HK)]

    def token_rows(ref, order):
        x = ref[...].reshape(page, len(order) * LANES)
        if list(order) != list(range(len(order))):
            x = jnp.concatenate([x[:, s * LANES:(s + 1) * LANES] for s in order], axis=1)
        return x.astype(BF16)

    row_hk = lax.broadcasted_iota(jnp.int32, (half, vd), 0) // G

    def own_head(x):
        out = x[:, :vd]
        for hk in range(1, HK):
            out = jnp.where(row_hk == hk, x[:, hk * vd:(hk + 1) * vd], out)
        return out

    @pl.when(ph == 0)
    def _():
        s_sc[pg] = scale * jnp.concatenate(
            [lax.dot_general(q, token_rows(kc, range(nk)), _NT, preferred_element_type=F32) for kc in kc_refs],
            axis=1)

    @pl.when(jnp.logical_and(ph == 0, pg == last))
    def _():
        lam = _diff_lambda_in_kernel(lq1_ref, lk1_ref, lq2_ref, lk2_ref, lam_init)
        s_new = scale * jnp.sum(q.astype(F32) * kn_ref[0].astype(BF16).astype(F32), axis=-1, keepdims=True)
        s = s_sc[...]
        m = jnp.maximum(jnp.max(jnp.max(s, axis=0), axis=-1, keepdims=True), s_new)
        e = jnp.exp(s - m[None])
        e_new = jnp.exp(s_new - m)
        denom = jnp.sum(jnp.sum(e, axis=0), axis=-1, keepdims=True) + e_new
        p = e / denom[None]
        p_new = e_new / denom
        a_sc[...] = (p[:, :half] - lam * p[:, half:]).astype(BF16)
        anew_sc[...] = jnp.broadcast_to(p_new[:half] - lam * p_new[half:], (half, LANES))
        acc_sc[...] = jnp.zeros(acc_sc.shape, F32)

    @pl.when(ph == 1)
    def _():
        a = a_sc[pg]
        pv = sum(jnp.dot(a[:, j * page:(j + 1) * page], token_rows(vc, v_order), preferred_element_type=F32)
                 for j, vc in enumerate(vc_refs))
        acc_sc[...] += own_head(pv)

    @pl.when(jnp.logical_and(ph == 1, pg == last))
    def _():
        a_new = anew_sc[...][:, :1]
        v_new = own_head(jnp.broadcast_to(vn_ref[0], (half, HK * vd)))
        o = acc_sc[...] + a_new * v_new
        ms = jnp.mean(o * o, axis=-1, keepdims=True)
        o_ref[0] = o * lax.rsqrt(ms + NORM_EPS) * subw_ref[...] * (1.0 - lam_init)


def _decode_diff_attn(q, k_new, v_new, cache_k, cache_v, page_table, lams, subln_w, *, HK, G, hd, vd, lam_init):
    DB, n_pages = page_table.shape
    n_phys, page = cache_k.shape[0], cache_k.shape[1]
    assert page % LANES == 0 and vd % LANES == 0 and hd == LANES
    kcw, vcw = HK * 2 * hd, HK * vd
    nk, nv = kcw // LANES, vcw // LANES
    kc = cache_k.reshape(n_phys * page * nk, LANES)
    vc = jnp.transpose(cache_v.reshape(n_phys, page, HK, vd // LANES, LANES),
                       (0, 1, 3, 2, 4)).reshape(n_phys * page * nv, LANES)
    NI = 2 * HK * G
    qt = jnp.transpose(q.reshape(DB, HK, G, 2, hd), (0, 3, 1, 2, 4))
    own = (jnp.eye(HK, dtype=q.dtype)[None, None, :, None, :, None, None]
           * jnp.eye(2, dtype=q.dtype)[None, :, None, None, None, :, None])
    qbd = (qt[:, :, :, :, None, None, :] * own).reshape(DB, NI, kcw)
    pps = math.gcd(n_pages, DECODE_PAGES_PER_STEP)
    steps = n_pages // pps
    per_b = lambda r, n: pl.BlockSpec((1, r, n), lambda b, ph, pg, pt: (b, 0, 0))
    vec = lambda n: pl.BlockSpec((1, n), lambda b, ph, pg, pt: (0, 0))

    def k_spec(j):
        return pl.BlockSpec((page * nk, LANES),
                            lambda b, ph, pg, pt: (pt[b, jnp.where(ph == 0, pg, steps - 1) * pps + j], 0))

    def v_spec(j):
        return pl.BlockSpec((page * nv, LANES),
                            lambda b, ph, pg, pt: (pt[b, jnp.where(ph == 1, pg, 0) * pps + j], 0))

    grid_spec = pltpu.PrefetchScalarGridSpec(
        num_scalar_prefetch=1,
        grid=(DB, 2, steps),
        in_specs=[per_b(NI, kcw), per_b(1, kcw), per_b(1, vcw)]
        + [k_spec(j) for j in range(pps)] + [v_spec(j) for j in range(pps)]
        + [vec(hd), vec(hd), vec(hd), vec(hd), vec(vd)],
        out_specs=pl.BlockSpec((1, NI // 2, vd), lambda b, ph, pg, pt: (b, 0, 0)),
        scratch_shapes=[pltpu.VMEM((steps, NI, pps * page), F32), pltpu.VMEM((steps, NI // 2, pps * page), BF16),
                        pltpu.VMEM((NI // 2, LANES), F32), pltpu.VMEM((NI // 2, vd), F32)],
    )
    out = pl.pallas_call(
        functools.partial(_decode_kernel, HK=HK, G=G, vd=vd, pps=pps, scale=hd ** -0.5, lam_init=lam_init),
        grid_spec=grid_spec,
        out_shape=jax.ShapeDtypeStruct((DB, NI // 2, vd), F32),
        compiler_params=_cparams("parallel", "arbitrary", "arbitrary"),
        name="decode_diff_attn",
    )(page_table, qbd, k_new.reshape(DB, 1, kcw), v_new.reshape(DB, 1, vcw), *([kc] * pps), *([vc] * pps),
      *[x.reshape(1, -1) for x in lams], subln_w.reshape(1, vd))
    return out.reshape(DB, HK * G * vd)


def _segsum_matrix(n, seg):
    r = lax.broadcasted_iota(jnp.int32, (n, n), 0) // seg
    c = lax.broadcasted_iota(jnp.int32, (n, n), 1) // seg
    return (r == c).astype(F32)


def _prep_kernel(*refs, halo, rh, exact, tc):
    if halo:
        (r_ref, k_ref, v_ref, l_ref, rp_ref, kp_ref, vp_ref, lp_ref, ri_ref, ki_ref, vi_ref, li_ref,
         mur_ref, muk_ref, muv_ref, mul_ref, w0_ref, w2_ref, a0_ref, a2_ref, g2_ref, kkw_ref, kaw_ref,
         ro_ref, lwo_ref, ko_ref, vo_ref, kko_ref, bo_ref, go_ref) = refs
    else:
        (r_ref, k_ref, v_ref, l_ref, rp_ref, kp_ref, vp_ref, lp_ref,
         mur_ref, muk_ref, muv_ref, mul_ref, w0_ref, w2_ref, a0_ref, a2_ref, g2_ref, kkw_ref, kaw_ref,
         ro_ref, lwo_ref, ko_ref, vo_ref, kko_ref, bo_ref, go_ref) = refs
    i = pl.program_id(0)

    def shifted(x_ref, prev_ref, init_ref):
        x = x_ref[...]
        if not halo:
            return x, prev_ref[...]
        last = prev_ref[SUBLANES - 1:SUBLANES, :]
        first = jnp.where(i == 0, init_ref[...], last)
        xs = pltpu.roll(x, 1, axis=0)
        rows = lax.broadcasted_iota(jnp.int32, x.shape, 0)
        return x, jnp.where(rows == 0, first, xs)

    def mix(x_ref, prev_ref, init_ref, mu_ref):
        x, xp = shifted(x_ref, prev_ref, init_ref)
        return x + (xp - x) * mu_ref[...]

    r = mix(r_ref, rp_ref, ri_ref if halo else None, mur_ref)
    k = mix(k_ref, kp_ref, ki_ref if halo else None, muk_ref)
    v = mix(v_ref, vp_ref, vi_ref if halo else None, muv_ref)
    lo = mix(l_ref, lp_ref, li_ref if halo else None, mul_ref)
    dl = w2_ref.shape[0]
    al = a2_ref.shape[0]
    xw = lo[:, :dl]
    xa = lo[:, dl:dl + al]
    xg = lo[:, dl + al:]
    wl = w0_ref[...] + _dg(jnp.tanh(xw), w2_ref[...], _NN, False)
    lw = -jnp.exp(-_softplus(-wl) - 0.5)
    a = _sigmoid(a0_ref[...] + _dg(xa, a2_ref[...], _NN, False))
    g = _dg(_sigmoid(xg), g2_ref[...], _NN, False)
    seg = _segsum_matrix(tc, rh)
    kk = k * kkw_ref[...]
    ss = _dg(kk * kk, seg, _NN, exact)
    kk = kk / jnp.maximum(jnp.sqrt(ss), 1e-12)
    kf = k * (1.0 + (a - 1.0) * kaw_ref[...])
    b = kk * a
    for pi in range(tc // LANES):
        sl = slice(pi * LANES, (pi + 1) * LANES)
        ro_ref[pi] = r[:, sl].astype(ro_ref.dtype)
        lwo_ref[pi] = lw[:, sl]
        ko_ref[pi] = kf[:, sl].astype(ko_ref.dtype)
        vo_ref[pi] = v[:, sl].astype(vo_ref.dtype)
        kko_ref[pi] = kk[:, sl].astype(kko_ref.dtype)
        bo_ref[pi] = b[:, sl].astype(bo_ref.dtype)
        go_ref[pi] = g[:, sl]


def _rwkv_prep(rkv, lora, prev, mu_rkv, mu_lora, w0, w2, a0, a2, g2p, k_k, k_a, *, rh, exact, tt=256, tc=512):
    T, W3 = rkv.shape
    W = W3 // 3
    LP = lora.shape[1]
    tc = _pick(W, tc)
    nj = W // tc
    halo = prev[0].shape[0] != T or T == 1
    tt = _pick(T, tt, SUBLANES)
    hb = tt // SUBLANES
    xblk = lambda off: pl.BlockSpec((tt, tc), lambda i, j: (i, off * nj + j))
    pblk = lambda off: pl.BlockSpec((SUBLANES, tc), lambda i, j: (jnp.maximum(i * hb - 1, 0), off * nj + j))
    cvec = lambda off: pl.BlockSpec((1, tc), lambda i, j: (0, off * nj + j))
    in_specs = [xblk(0), xblk(1), xblk(2), pl.BlockSpec((tt, LP), lambda i, j: (i, 0))]
    args = [rkv, rkv, rkv, lora]
    if halo:
        in_specs += [pblk(0), pblk(1), pblk(2),
                     pl.BlockSpec((SUBLANES, LP), lambda i, j: (jnp.maximum(i * hb - 1, 0), 0)),
                     cvec(0), cvec(1), cvec(2), pl.BlockSpec((1, LP), lambda i, j: (0, 0))]
        args += [rkv, rkv, rkv, lora, prev[0], prev[0], prev[0], prev[1]]
    else:
        in_specs += [xblk(0), xblk(1), xblk(2), pl.BlockSpec((tt, LP), lambda i, j: (i, 0))]
        args += [prev[0], prev[0], prev[0], prev[1]]
    in_specs += [cvec(0), cvec(1), cvec(2), pl.BlockSpec((1, LP), lambda i, j: (0, 0)),
                 cvec(0), pl.BlockSpec((w2.shape[0], tc), lambda i, j: (0, j)),
                 cvec(0), pl.BlockSpec((a2.shape[0], tc), lambda i, j: (0, j)),
                 pl.BlockSpec((g2p.shape[0], tc), lambda i, j: (0, j)),
                 cvec(0), cvec(0)]
    args += [mu_rkv, mu_rkv, mu_rkv, mu_lora, w0.reshape(1, W), w2, a0.reshape(1, W), a2, g2p,
             k_k.reshape(1, W), k_a.reshape(1, W)]
    npair = tc // LANES
    store = F32 if exact else BF16
    oshape = [jax.ShapeDtypeStruct((W // LANES, T, LANES), dt) for dt in (store, F32, store, store, store, store, F32)]
    ospec = pl.BlockSpec((npair, tt, LANES), lambda i, j: (j, i, 0))
    return pl.pallas_call(
        functools.partial(_prep_kernel, halo=halo, rh=rh, exact=exact, tc=tc),
        grid=(T // tt, nj),
        in_specs=in_specs,
        out_specs=[ospec] * 7,
        out_shape=oshape,
        compiler_params=_cparams("parallel", "arbitrary"),
        name="rwkv_prep",
    )(*args)


def _scan_kernel(r_ref, lw_ref, k_ref, v_ref, kk_ref, b_ref, g_ref, s0_ref, rk_ref, lnw_ref, lnb_ref,
                 y_ref, sout_ref, s_sc, *, C, rh, npair, precise, t_real):
    c = pl.program_id(1)
    R = 2 * C
    lane = lax.broadcasted_iota(jnp.int32, (C, LANES), 1)
    head0 = lane < rh
    ri = lax.broadcasted_iota(jnp.int32, (R, R), 0)
    ci = lax.broadcasted_iota(jnp.int32, (R, R), 1)
    same = (ri // C) == (ci // C)
    strict = jnp.logical_and(same, (ci % C) < (ri % C))
    incl = jnp.logical_and(same, (ci % C) <= (ri % C))
    tril_c = (lax.broadcasted_iota(jnp.int32, (C, C), 1)
              <= lax.broadcasted_iota(jnp.int32, (C, C), 0)).astype(F32)
    seg = _segsum_matrix(LANES, rh)
    same_head = seg > 0.5
    zero = jnp.zeros((rh, rh), F32)

    def stack_masked(x):
        return jnp.concatenate([jnp.where(head0, x, 0.0), jnp.where(head0, 0.0, x)], axis=0)

    def stack_dup(x):
        return jnp.concatenate([x, x], axis=0)

    @pl.when(c == 0)
    def _():
        def init(p, carry):
            top = jnp.concatenate([s0_ref[0, 2 * p], zero], axis=1)
            bot = jnp.concatenate([zero, s0_ref[0, 2 * p + 1]], axis=1)
            s_sc[p] = jnp.concatenate([top, bot], axis=0)
            return carry
        lax.fori_loop(0, npair, init, 0)

    def single_step(r, lw, k, v, kk, b, S):
        sa = yield -kk, S, _NT, False
        outer = yield jnp.concatenate([sa, v], axis=0), jnp.concatenate([b, k], axis=0), _TN, True
        s_new = S * jnp.exp(lw[0:1, :]) + jnp.where(same_head, outer, 0.0)
        y = yield r, s_new, _NT, False
        return y, s_new

    def chunk_steps(r, lw, k, v, kk, b, S):
        cum = yield tril_c, lw, _NN, True
        tot = cum[C - 1:C, :]
        e_incl = jnp.exp(cum)
        e_inv = jnp.exp(-cum)
        e_end = jnp.exp(tot - cum)
        a2 = stack_masked(-kk * jnp.exp(cum - lw))
        r2 = stack_masked(r * e_incl)
        v2 = stack_masked(v)
        bd2 = stack_masked(b * e_end)
        kd2 = stack_masked(k * e_end)
        b2 = stack_dup(b * e_inv)
        k2 = stack_dup(k * e_inv)
        ar = jnp.concatenate([a2, r2], axis=0)
        arb = yield ar, b2, _NT, precise
        ark = yield ar, k2, _NT, precise
        ars = yield ar, S, _NT, precise
        mrb = jnp.where(incl, arb[R:], 0.0)
        mrk = jnp.where(incl, ark[R:], 0.0)
        x = ars[:R]
        n_it = (t_real - 1).bit_length()
        if n_it:
            x = x + (yield jnp.where(strict, ark[:R], 0.0), v2, _NN, precise)
            pw = jnp.where(strict, arb[:R], 0.0)
        for it in range(n_it):
            x = x + (yield pw, x, _NN, precise)
            if it + 1 < n_it:
                pw = yield pw, pw, _NN, precise
        y2 = ars[R:] + (yield mrb, x, _NN, precise)
        y2 = y2 + (yield mrk, v2, _NN, precise)
        y = y2[:C] + y2[C:]
        s_new = S * jnp.exp(tot) + (yield x, bd2, _TN, precise)
        s_new = s_new + (yield v2, kd2, _TN, precise)
        return y, s_new

    def slab(r, lw, k, v, kk, b, g, S, rk, lnw, lnb):
        steps = single_step if t_real == 1 else chunk_steps
        y, s_new = yield from steps(r, lw, k, v, kk, b, S)
        mean = (yield y, seg, _NN, precise) * (1.0 / rh)
        d = y - mean
        var = (yield d * d, seg, _NN, precise) * (1.0 / rh)
        yn = d * lax.rsqrt(var + GN_EPS) * lnw + lnb
        bonus = (yield r * k * rk, seg, _NN, precise) * v
        return (yn + bonus) * g, s_new

    def run_lockstep(gens):
        reqs = [next(gen) for gen in gens]
        results = [None] * len(gens)
        live = list(range(len(gens)))
        while live:
            vals = [_dg(*reqs[i]) for i in live]
            nxt = []
            for i, val in zip(live, vals):
                try:
                    reqs[i] = gens[i].send(val)
                    nxt.append(i)
                except StopIteration as done:
                    results[i] = done.value
            live = nxt
        return results

    U = math.gcd(npair, SLABS_PER_GROUP)

    def body(i, carry):
        grp = pl.ds(pl.multiple_of(i * U, U), U)
        ins = [ref[grp].astype(F32) for ref in (r_ref, lw_ref, k_ref, v_ref, kk_ref, b_ref, g_ref, s_sc,
                                                rk_ref, lnw_ref, lnb_ref)]
        outs = run_lockstep([slab(*[a[u] for a in ins]) for u in range(U)])
        y_ref[grp] = jnp.stack([o[0] for o in outs]).astype(y_ref.dtype)
        s_sc[grp] = jnp.stack([o[1] for o in outs])
        return carry

    lax.fori_loop(0, npair // U, body, 0)

    @pl.when(c == pl.num_programs(1) - 1)
    def _():
        def fin(p, carry):
            S = s_sc[p]
            sout_ref[0, 2 * p] = S[:rh, :rh]
            sout_ref[0, 2 * p + 1] = S[rh:, rh:]
            return carry
        lax.fori_loop(0, npair, fin, 0)


def _rwkv_scan(prep, s0, r_k, ln_w, ln_b, *, B, C, rh, precise, out_dtype, t_real):
    npair, BT, _ = prep[0].shape
    Tb = BT // B
    nc = Tb // C
    H = s0.shape[1]
    xspec = pl.BlockSpec((npair, C, LANES), lambda b, c: (0, b * nc + c, 0))
    sspec = pl.BlockSpec((1, H, rh, rh), lambda b, c: (b, 0, 0, 0))
    pspec = pl.BlockSpec((npair, 1, LANES), lambda b, c: (0, 0, 0))
    y, s_out = pl.pallas_call(
        functools.partial(_scan_kernel, C=C, rh=rh, npair=npair, precise=precise, t_real=t_real),
        grid=(B, nc),
        in_specs=[xspec] * 7 + [sspec, pspec, pspec, pspec],
        out_specs=[xspec, sspec],
        out_shape=[jax.ShapeDtypeStruct((npair, BT, LANES), out_dtype),
                   jax.ShapeDtypeStruct((B, H, rh, rh), F32)],
        scratch_shapes=[pltpu.VMEM((npair, 2 * rh, 2 * rh), F32)],
        compiler_params=_cparams("parallel", "arbitrary"),
        name="rwkv_scan",
    )(*prep, s0, r_k.reshape(npair, 1, LANES), ln_w.reshape(npair, 1, LANES), ln_b.reshape(npair, 1, LANES))
    return y, s_out


def _router_kernel(x_ref, nw_ref, rw_ref, rb_ref, h_ref, lg_ref, *, precise):
    x = x_ref[...]
    ms = jnp.mean(x * x, axis=-1, keepdims=True)
    h = x * lax.rsqrt(ms + NORM_EPS) * nw_ref[...]
    h_ref[...] = h.astype(h_ref.dtype)
    lg_ref[...] = _dg(h, rw_ref[...], _NN, precise) + rb_ref[...]


def _router(x, norm_w, rw, rb, *, precise, h_dtype, tm=512):
    M, D = x.shape
    NR = rw.shape[1]
    tm = _pick(M, tm, SUBLANES)
    return pl.pallas_call(
        functools.partial(_router_kernel, precise=precise),
        grid=(M // tm,),
        in_specs=[pl.BlockSpec((tm, D), lambda i: (i, 0)), pl.BlockSpec((1, D), lambda i: (0, 0)),
                  pl.BlockSpec((D, NR), lambda i: (0, 0)), pl.BlockSpec((1, NR), lambda i: (0, 0))],
        out_specs=[pl.BlockSpec((tm, D), lambda i: (i, 0)), pl.BlockSpec((tm, NR), lambda i: (i, 0))],
        out_shape=[jax.ShapeDtypeStruct((M, D), h_dtype), jax.ShapeDtypeStruct((M, NR), F32)],
        compiler_params=_cparams("parallel"),
        name="moe_router",
    )(x, norm_w.reshape(1, D), rw, rb)


def _moe_kernel(be_ref, nu_ref, xr_ref, x_ref, wg_ref, wu_ref, wd_ref, sw_ref, o_ref):
    b = pl.program_id(0)
    j = pl.program_id(1)
    last = j == pl.num_programs(1) - 1
    used = b < nu_ref[0]

    @pl.when(jnp.logical_and(used, j == 0))
    def _():
        o_ref[...] = jnp.zeros(o_ref.shape, F32)

    @pl.when(used)
    def _():
        x = x_ref[...]
        tn = wg_ref.shape[2]
        wgu = jnp.concatenate([wg_ref[0].astype(BF16), wu_ref[0].astype(BF16)], axis=1)
        hgu = jnp.dot(x, wgu, preferred_element_type=F32)
        hg, hu = hgu[:, :tn], hgu[:, tn:]
        h = hg * _sigmoid(hg) * hu
        o_ref[...] += jnp.dot(h.astype(BF16), wd_ref[0].astype(BF16), preferred_element_type=F32)

    @pl.when(jnp.logical_and(used, last))
    def _():
        o_ref[...] = o_ref[...] * sw_ref[...]

    @pl.when(jnp.logical_and(jnp.logical_not(used), last))
    def _():
        o_ref[...] = jnp.zeros(o_ref.shape, F32)


def _moe_experts(xs, x_row0, blk_e, n_used, slot_w, wg, wu, wd, *, blk, tn):
    D = xs.shape[1]
    P = slot_w.shape[0]
    DE = wg.shape[2]
    tn = _pick(DE, tn)
    nj = DE // tn
    nb = P // blk

    def eff(b, j, nu):
        live = b < nu[0]
        return jnp.where(live, b, nu[0] - 1), jnp.where(live, j, nj - 1)

    def wg_map(b, j, be, nu, xr):
        bb, jj = eff(b, j, nu)
        return be[bb], 0, jj

    def wd_map(b, j, be, nu, xr):
        bb, jj = eff(b, j, nu)
        return be[bb], jj, 0

    grid_spec = pltpu.PrefetchScalarGridSpec(
        num_scalar_prefetch=3,
        grid=(nb, nj),
        in_specs=[pl.BlockSpec((pl.Element(blk), pl.Element(D)),
                               lambda b, j, be, nu, xr: (pl.multiple_of(xr[eff(b, j, nu)[0]], 2 * SUBLANES), 0)),
                  pl.BlockSpec((1, D, tn), wg_map), pl.BlockSpec((1, D, tn), wg_map),
                  pl.BlockSpec((1, tn, D), wd_map),
                  pl.BlockSpec((blk, 1), lambda b, j, be, nu, xr: (eff(b, j, nu)[0], 0))],
        out_specs=pl.BlockSpec((blk, D), lambda b, j, be, nu, xr: (b, 0)),
    )
    return pl.pallas_call(
        _moe_kernel,
        grid_spec=grid_spec,
        out_shape=jax.ShapeDtypeStruct((P, D), F32),
        compiler_params=_cparams("arbitrary", "arbitrary"),
        name="moe_experts",
    )(blk_e, n_used, x_row0, xs, wg, wu, wd, slot_w.reshape(P, 1))


def _hier_moe(x_groups, norm_w, rg_w, rg_b, re_w, re_b, wg, wu, wd, *, blk, tn):
    D = x_groups[0].shape[1]
    NG = rg_w.shape[1]
    E = re_w.shape[1]
    EPG = E // NG
    NR = -(-(NG + E) // LANES) * LANES
    rw = jnp.pad(jnp.concatenate([rg_w, re_w], axis=1), ((0, 0), (0, NR - NG - E))).astype(BF16)
    rb = jnp.pad(jnp.concatenate([rg_b, re_b]), (0, NR - NG - E)).reshape(1, NR)
    routed = [_router(x, norm_w, rw, rb, precise=False, h_dtype=BF16) for x in x_groups]
    logits = jnp.concatenate([r[1] for r in routed], axis=0)
    T = logits.shape[0]
    gprob = jax.nn.softmax(logits[:, :NG], axis=-1)
    gsel = jnp.argmax(gprob, axis=-1)
    gp = jnp.take_along_axis(gprob, gsel[:, None], axis=-1)[:, 0]
    elog = logits[:, NG:NG + E].reshape(T, NG, EPG)
    elog = jnp.take_along_axis(elog, gsel[:, None, None], axis=1)[:, 0]
    top_p, top_i = lax.top_k(jax.nn.softmax(elog, axis=-1), TOP_K)
    top_p = top_p / jnp.sum(top_p, axis=-1, keepdims=True)
    weights = gp[:, None] * top_p
    expert_idx = gsel[:, None].astype(jnp.int32) * EPG + top_i.astype(jnp.int32)

    TK = T * TOP_K
    nb = -(-TK // blk) + E
    P = nb * blk
    flat_e = expert_idx.reshape(-1)
    flat_w = weights.reshape(-1)
    flat_t = jnp.repeat(jnp.arange(T, dtype=jnp.int32), TOP_K)
    order = jnp.argsort(flat_e)
    se = flat_e[order]
    counts = jnp.bincount(flat_e, length=E)
    pcounts = (counts + blk - 1) // blk * blk
    pend = jnp.cumsum(pcounts)
    pstart = pend - pcounts
    start = jnp.cumsum(counts) - counts
    dest = (pstart[se] + jnp.arange(TK) - start[se]).astype(jnp.int32)
    slot_w = jnp.zeros((P,), F32).at[dest].set(flat_w[order])
    seg = 2 * SUBLANES
    ccounts = (counts + seg - 1) // seg * seg
    cstart = jnp.cumsum(ccounts) - ccounts
    n_rows = -(-(TK + seg * E) // seg) * seg + blk
    cdest = (cstart[se] + jnp.arange(TK) - start[se]).astype(jnp.int32)
    slot_t = jnp.full((n_rows,), T, jnp.int32).at[cdest].set(flat_t[order])
    blk_e = jnp.minimum(jnp.searchsorted(pend, jnp.arange(nb) * blk, side='right'), E - 1).astype(jnp.int32)
    n_used = (pend[-1] // blk).astype(jnp.int32).reshape(1)
    pos = jnp.zeros((TK,), jnp.int32).at[order].set(dest)

    x_row0 = (cstart[blk_e] + (jnp.arange(nb, dtype=jnp.int32) - pstart[blk_e] // blk) * blk).astype(jnp.int32)
    x_row0 = jnp.clip(x_row0, 0, n_rows - blk)
    xs = jnp.concatenate([r[0] for r in routed] + [jnp.zeros((1, D), BF16)], axis=0)[slot_t]
    ys = _moe_experts(xs, x_row0, blk_e, n_used, slot_w, wg, wu, wd, blk=blk, tn=tn)
    pos = pos.reshape(T, TOP_K)
    outs, t0 = [], 0
    for x in x_groups:
        pg = pos[t0:t0 + x.shape[0]]
        outs.append(x + ys[pg[:, 0]] + ys[pg[:, 1]])
        t0 += x.shape[0]
    return outs


def _pairs_to_rows(y):
    npair, T, _ = y.shape
    return jnp.transpose(y, (1, 0, 2)).reshape(T, npair * LANES)


def _mixer(x, positions, prev_rows, s0, B, cfg, w, *, exact, attn_fn, q_scale, emit_bf16):
    T, D = x.shape
    hd, HK, G, vd, rh = cfg['hd'], cfg['HK'], cfg['G'], cfg['vd'], cfg['rh']
    qcols, kcols, vcols = HK * G * 2 * hd, HK * 2 * hd, HK * vd
    W = cfg['W']
    mm = functools.partial(_matmul, precise=False)

    h = _rmsnorm(x, w['norm1_w'], BF16)
    qkv = mm(h, w['w_in'], qcols + kcols + vcols, 0)
    rkv = mm(h, w['w_in'], 3 * W, cfg['r_off'])
    lora = mm(h, w['w_in'], cfg['lp'], cfg['r_off'] + 3 * W, tn=cfg['lp'])
    gates = mm(h, w['w_gates'], 2 * D)

    cos_full, sin_signed = _rope_tables(positions, hd)
    outs = _qk_rope(qkv, cos_full, sin_signed, w['q_norm_w'], w['k_norm_w'], qcols, kcols, vcols, vd,
                    q_scale=q_scale, q_dtype=BF16, emit_bf16=emit_bf16)
    q = outs[0]
    k_new, v_new = _kv_rows(outs[1], outs[2], T, HK, hd, vd)
    outs = (q, k_new.reshape(T, kcols), v_new.reshape(T, vcols)) + tuple(outs[3:])
    o_attn = attn_fn(q, outs)

    prep = _rwkv_prep(rkv, lora, prev_rows, w['mu_rkv'], w['mu_lora'], w['rwkv_w0'], w['rwkv_w2'],
                      w['rwkv_a0'], w['rwkv_a2'], w['g2p'], w['rwkv_k_k'], w['rwkv_k_a'],
                      rh=rh, exact=exact)
    Tb = T // B
    if Tb % 64 == 0:
        C = 64
    else:
        C = SUBLANES
        pad = -Tb % C
        prep = [jnp.pad(a.reshape(-1, B, Tb, LANES), ((0, 0), (0, 0), (0, pad), (0, 0)))
                .reshape(a.shape[0], B * (Tb + pad), LANES) for a in prep]
    y_r, s_fin = _rwkv_scan(prep, s0, w['rwkv_r_k'], w['rwkv_ln_w'], w['rwkv_ln_b'],
                            B=B, C=C, rh=rh, precise=exact, out_dtype=BF16, t_real=min(Tb, C))
    if Tb % 64:
        y_r = y_r.reshape(y_r.shape[0], B, -1, LANES)[:, :, :Tb].reshape(y_r.shape[0], T, LANES)
    o_rwkv = _pairs_to_rows(y_r)

    mixed = _merge(o_attn, w['w_branch_attn'], o_rwkv, w['w_branch_rwkv'], gates, w['gate_b'],
                   precise=False, out_dtype=BF16)
    x1 = _matmul_residual(mixed, w['w_out'], x, precise=False)
    return x1, k_new, v_new, s_fin, rkv, lora


def kernel(x_prompt, x_sample, cache_k, cache_v, state_rwkv, state_shift, page_table, norm1_w, w_in, gate_b, q_norm_w, k_norm_w, lambda_q1, lambda_k1, lambda_q2, lambda_k2, subln_w, rwkv_mu, rwkv_w0, rwkv_w2, rwkv_a0, rwkv_a2, rwkv_g2, rwkv_k_k, rwkv_k_a, rwkv_r_k, rwkv_ln_w, rwkv_ln_b, w_branch_attn, w_branch_rwkv, w_out, norm2_w, router_group_w, router_group_b, router_expert_w, router_expert_b, expert_w_gate, expert_w_up, expert_w_down):
    B, T, D = x_prompt.shape
    DB, S, _ = x_sample.shape
    L = norm1_w.shape[0]
    assert L == 1 and B == 1 and S == 1, "one trunk layer, one prompt sequence, one new token per sample"
    hd = q_norm_w.shape[1]
    HK = cache_k.shape[3]
    vd = cache_v.shape[4]
    G = w_branch_attn.shape[1] // vd // HK
    H, rh = rwkv_r_k.shape[1], rwkv_r_k.shape[2]
    W = H * rh
    dl, al, gl = rwkv_w2.shape[1], rwkv_a2.shape[1], rwkv_g2.shape[1]
    lcols = dl + al + gl
    lp = -(-lcols // LANES) * LANES
    qkv_cols = HK * G * 2 * hd + HK * 2 * hd + HK * vd
    r_off = qkv_cols
    g_off = r_off + 3 * W + lcols
    n_pages, page = page_table.shape[1], cache_k.shape[2]
    past = n_pages * page
    cfg = dict(hd=hd, HK=HK, G=G, vd=vd, rh=rh, W=W, lp=lp, r_off=r_off)
    l = 0
    (cache_k, cache_v, state_rwkv, state_shift, norm1_w, w_in, gate_b, q_norm_w, k_norm_w, lambda_q1,
     lambda_k1, lambda_q2, lambda_k2, subln_w, rwkv_mu, rwkv_w0, rwkv_w2, rwkv_a0, rwkv_a2, rwkv_g2,
     rwkv_k_k, rwkv_k_a, rwkv_r_k, rwkv_ln_w, rwkv_ln_b, w_branch_attn, w_branch_rwkv, w_out, norm2_w,
     router_group_w, router_group_b, router_expert_w, router_expert_b, expert_w_gate, expert_w_up,
     expert_w_down) = [a.reshape(a.shape[1:]) for a in (
         cache_k, cache_v, state_rwkv, state_shift, norm1_w, w_in, gate_b, q_norm_w, k_norm_w, lambda_q1,
         lambda_k1, lambda_q2, lambda_k2, subln_w, rwkv_mu, rwkv_w0, rwkv_w2, rwkv_a0, rwkv_a2, rwkv_g2,
         rwkv_k_k, rwkv_k_a, rwkv_r_k, rwkv_ln_w, rwkv_ln_b, w_branch_attn, w_branch_rwkv, w_out, norm2_w,
         router_group_w, router_group_b, router_expert_w, router_expert_b, expert_w_gate, expert_w_up,
         expert_w_down)]
    lam_init = 0.8 - 0.6 * math.exp(-0.3 * l)
    lams = (lambda_q1, lambda_k1, lambda_q2, lambda_k2)

    mu = rwkv_mu
    g2p = jnp.pad(rwkv_g2, ((0, lp - lcols), (0, 0)))
    w_in_bf = w_in.astype(BF16)
    w = dict(
        norm1_w=norm1_w, q_norm_w=q_norm_w, k_norm_w=k_norm_w, gate_b=gate_b,
        mu_rkv=mu[:3 * W].reshape(1, 3 * W), mu_lora=jnp.pad(mu[3 * W:], (0, lp - lcols)).reshape(1, lp),
        rwkv_w0=rwkv_w0, rwkv_a0=rwkv_a0, rwkv_k_k=rwkv_k_k, rwkv_k_a=rwkv_k_a,
        rwkv_r_k=rwkv_r_k, rwkv_ln_w=rwkv_ln_w, rwkv_ln_b=rwkv_ln_b,
        norm2_w=norm2_w, router_group_w=router_group_w, router_group_b=router_group_b,
        router_expert_w=router_expert_w, router_expert_b=router_expert_b,
        expert_w_gate=expert_w_gate, expert_w_up=expert_w_up, expert_w_down=expert_w_down,
        w_in=w_in_bf, w_gates=w_in_bf[:, g_off:g_off + 2 * D],
        rwkv_w2=rwkv_w2.astype(BF16), rwkv_a2=rwkv_a2.astype(BF16), g2p=g2p.astype(BF16),
        w_branch_attn=w_branch_attn.astype(BF16), w_branch_rwkv=w_branch_rwkv.astype(BF16),
        w_out=w_out.astype(BF16))

    def prompt_attn(q, outs):
        return _flash_diff_attn(q, outs[3], outs[4], lams, subln_w, HK=HK, G=G, hd=hd, vd=vd,
                                lam_init=lam_init, out_dtype=BF16)

    zero_prev = (jnp.zeros((1, 3 * W), F32), jnp.zeros((1, lp), F32))
    xp1, kp, vp, sp, rkv_p, lora_p = _mixer(
        x_prompt.reshape(T, D), jnp.arange(T), zero_prev, jnp.zeros((1, H, rh, rh), F32), 1, cfg, w,
        exact=False, attn_fn=prompt_attn, q_scale=hd ** -0.5 * math.log2(math.e), emit_bf16=True)
    shift_p = jnp.concatenate([rkv_p[T - 1], lora_p[T - 1, :lcols]])

    def sample_attn(q, outs):
        return _decode_diff_attn(q, outs[1], outs[2], cache_k, cache_v, page_table, lams, subln_w,
                                 HK=HK, G=G, hd=hd, vd=vd, lam_init=lam_init)

    sh = state_shift
    prev_s = (sh[:, :3 * W], jnp.pad(sh[:, 3 * W:], ((0, 0), (0, lp - lcols))))
    xs1, ks, vs, ss, rkv_s, lora_s = _mixer(
        x_sample.reshape(DB, D), jnp.full((DB,), past, jnp.int32), prev_s, state_rwkv, DB, cfg, w,
        exact=True, attn_fn=sample_attn, q_scale=1.0, emit_bf16=False)
    shift_s = jnp.concatenate([rkv_s, lora_s[:, :lcols]], axis=1)


    yp, ys = _hier_moe([xp1, xs1], norm2_w, router_group_w, router_group_b, router_expert_w,
                       router_expert_b, expert_w_gate, expert_w_up, expert_w_down,
                       blk=min(MOE_ROW_BLOCK, max(2 * SUBLANES, T // 16)), tn=MOE_HIDDEN_TILE)

    return (yp.reshape(1, T, D), ys.reshape(DB, 1, D),
            kp.reshape(1, 1, T, HK, 2, hd), vp.reshape(1, 1, T, HK, vd),
            sp.reshape(1, 1, H, rh, rh), shift_p.reshape(1, 1, -1),
            ks.reshape(1, DB, 1, HK, 2, hd), vs.reshape(1, DB, 1, HK, vd),
            ss.reshape(1, DB, H, rh, rh), shift_s.reshape(1, DB, -1))
```
